```python
import math
import jax, jax.numpy as jnp
from jax import lax
import numpy as np

D_MODEL = 1024
BATCH = 8
SEQ = 4096
DEPTH = 4

HEAD_DIM = 64
BLOCK = 128
RMS_EPS = 1e-6
SUBLN_EPS = 1e-5
A_HEADS = 4
A_QK = A_HEADS * 2 * HEAD_DIM
A_WIDTH = A_HEADS * 2 * HEAD_DIM
B_HEADS = 8
B_KV_HEADS = 2
B_GROUP = B_HEADS // B_KV_HEADS
B_WIDTH = B_HEADS * HEAD_DIM
B_KV = B_KV_HEADS * HEAD_DIM
WINDOW = 128
C_HEADS = 8
C_WIDTH = C_HEADS * HEAD_DIM
N_BRANCH = 3
SPLITS = (A_QK, A_QK, A_WIDTH, A_WIDTH,
          B_WIDTH, B_KV, B_KV, B_WIDTH,
          C_WIDTH, C_WIDTH, C_WIDTH, C_HEADS, C_WIDTH,
          N_BRANCH * D_MODEL)
D_IN = sum(SPLITS)

kernel_name = "hybrid_diff_swa_fox_gated_block"


def rms_norm(x, gain, eps):
    xf = x.astype(jnp.float32)
    y = xf * lax.rsqrt(jnp.mean(xf * xf, axis=-1, keepdims=True) + eps)
    return (y * gain.astype(jnp.float32)).astype(x.dtype)


def alibi_slopes(n_heads):
    return 2.0 ** (-8.0 * jnp.arange(1, n_heads + 1, dtype=jnp.float32) / n_heads)


def diff_attention(q, k, v, lam, slopes):
    B, S, H, _, d = q.shape
    nb = S // BLOCK
    scale = d ** -0.5
    qb = jnp.moveaxis(q.reshape(B, nb, BLOCK, H, 2, d), 1, 0)
    k_pos = jnp.arange(S)

    def one_block(args):
        qblk, start = args
        s = jnp.einsum('bqhmd,bkhmd->bhmqk', qblk, k).astype(jnp.float32) * scale
        dist = (start + jnp.arange(BLOCK))[:, None] - k_pos[None, :]
        bias = jnp.where(dist >= 0, -slopes[:, None, None] * dist.astype(jnp.float32), -jnp.inf)
        p = jax.nn.softmax(s + bias[None, :, None], axis=-1)
        w = p[:, :, 0] - lam * p[:, :, 1]
        return jnp.einsum('bhqk,bkhe->bqhe', w.astype(v.dtype), v)

    out = lax.map(one_block, (qb, jnp.arange(nb) * BLOCK))
    return jnp.moveaxis(out, 0, 1).reshape(B, S, H, v.shape[-1])


def sliding_window_attention(q, k, v, sinks, slopes):
    B, S, HQ, d = q.shape
    KV = k.shape[2]
    G = HQ // KV
    nb = S // BLOCK
    scale = d ** -0.5
    qb = q.reshape(B, nb, BLOCK, KV, G, d)

    def banded(t):
        tb = t.reshape(B, nb, BLOCK, KV, d)
        prev = jnp.concatenate([jnp.zeros_like(tb[:, :1]), tb[:, :-1]], axis=1)
        return jnp.concatenate([prev, tb], axis=2)

    kb, vb = banded(k), banded(v)
    s = jnp.einsum('bnqhgd,bnkhd->bnhgqk', qb, kb).astype(jnp.float32) * scale
    i = jnp.arange(BLOCK)[:, None]
    j = jnp.arange(2 * BLOCK)[None, :]
    dist = i - j + BLOCK
    key_pos = jnp.arange(nb)[:, None, None] * BLOCK + j - BLOCK
    valid = (dist >= 0) & (dist < WINDOW) & (key_pos >= 0)
    bias = -slopes.reshape(KV, G)[:, :, None, None] * dist.astype(jnp.float32)
    s = jnp.where(valid[None, :, None, None], s + bias[None, None], -jnp.inf)
    sink = sinks.astype(jnp.float32).reshape(KV, G)[None, None, :, :, None, None]
    m = jnp.maximum(jnp.max(s, axis=-1, keepdims=True), sink)
    e = jnp.exp(s - m)
    p = e / (jnp.sum(e, axis=-1, keepdims=True) + jnp.exp(sink - m))
    out = jnp.einsum('bnhgqk,bnkhd->bnqhgd', p.astype(v.dtype), vb)
    return out.reshape(B, S, HQ * d)


def forgetting_attention(q, k, v, logf):
    B, S, H, d = q.shape
    nb = S // BLOCK
    scale = d ** -0.5
    c = jnp.cumsum(logf, axis=1)
    c_k = jnp.transpose(c, (0, 2, 1))
    qb = jnp.moveaxis(q.reshape(B, nb, BLOCK, H, d), 1, 0)
    cb = jnp.moveaxis(c.reshape(B, nb, BLOCK, H), 1, 0)
    k_pos = jnp.arange(S)

    def one_block(args):
        qblk, cblk, start = args
        s = jnp.einsum('bqhd,bkhd->bhqk', qblk, k).astype(jnp.float32) * scale
        decay = jnp.transpose(cblk, (0, 2, 1))[:, :, :, None] - c_k[:, :, None, :]
        dist = (start + jnp.arange(BLOCK))[:, None] - k_pos[None, :]
        s = jnp.where((dist >= 0)[None, None], s + decay, -jnp.inf)
        p = jax.nn.softmax(s, axis=-1)
        return jnp.einsum('bhqk,bkhd->bqhd', p.astype(v.dtype), v)

    out = lax.map(one_block, (qb, cb, jnp.arange(nb) * BLOCK))
    return jnp.moveaxis(out, 0, 1).reshape(B, S, H * d)


def setup_inputs(seed: int = 0) -> dict:
    key = jax.random.key(seed)
    ks = jax.random.split(key, 16)
    f32 = jnp.float32
    nrm = lambda k, shape: jax.random.normal(k, shape, dtype=f32)
    return {
        "x": nrm(ks[0], (BATCH, SEQ, D_MODEL)),
        "norm_gain": 1.0 + 0.05 * nrm(ks[1], (DEPTH, D_MODEL)),
        "w_in": nrm(ks[2], (DEPTH, D_MODEL, D_IN)) * D_MODEL ** -0.5,
        "b_forget": 0.1 * nrm(ks[3], (DEPTH, C_HEADS)),
        "lambda_q1": 0.1 * nrm(ks[4], (DEPTH, HEAD_DIM)),
        "lambda_k1": 0.1 * nrm(ks[5], (DEPTH, HEAD_DIM)),
        "lambda_q2": 0.1 * nrm(ks[6], (DEPTH, HEAD_DIM)),
        "lambda_k2": 0.1 * nrm(ks[7], (DEPTH, HEAD_DIM)),
        "subln_gain": 1.0 + 0.05 * nrm(ks[8], (DEPTH, 2 * HEAD_DIM)),
        "sinks": 0.5 * nrm(ks[9], (DEPTH, B_HEADS)),
        "w_up_a": nrm(ks[10], (DEPTH, A_WIDTH, D_MODEL)) * A_WIDTH ** -0.5,
        "w_up_b": nrm(ks[11], (DEPTH, B_WIDTH, D_MODEL)) * B_WIDTH ** -0.5,
        "w_up_c": nrm(ks[12], (DEPTH, C_WIDTH, D_MODEL)) * C_WIDTH ** -0.5,
        "w_o": nrm(ks[13], (DEPTH, D_MODEL, D_MODEL)) * D_MODEL ** -0.5,
        "final_gain": 1.0 + 0.05 * nrm(ks[14], (D_MODEL,)),
    }


def reference(x, norm_gain, w_in, b_forget, lambda_q1, lambda_k1, lambda_q2, lambda_k2,
              subln_gain, sinks, w_up_a, w_up_b, w_up_c, w_o, final_gain):
    B, S, D = x.shape
    offsets = np.cumsum(SPLITS)[:-1].tolist()
    slopes_a = alibi_slopes(A_HEADS)
    slopes_b = alibi_slopes(B_HEADS)
    for l in range(DEPTH):
        h = rms_norm(x, norm_gain[l], RMS_EPS)
        proj = jnp.einsum('bsd,de->bse', h, w_in[l])
        (qa, ka, va, ga, qb, kb, vb, gb,
         qc, kc, vc, fc, gc, gm) = jnp.split(proj, offsets, axis=-1)

        lam_init = 0.8 - 0.6 * math.exp(-0.3 * l)
        lam = (jnp.exp(jnp.sum(lambda_q1[l].astype(jnp.float32) * lambda_k1[l].astype(jnp.float32)))
               - jnp.exp(jnp.sum(lambda_q2[l].astype(jnp.float32) * lambda_k2[l].astype(jnp.float32)))
               + lam_init)
        ya = diff_attention(qa.reshape(B, S, A_HEADS, 2, HEAD_DIM),
                            ka.reshape(B, S, A_HEADS, 2, HEAD_DIM),
                            va.reshape(B, S, A_HEADS, 2 * HEAD_DIM), lam, slopes_a)
        ya = (rms_norm(ya, subln_gain[l], SUBLN_EPS) * (1.0 - lam_init)).reshape(B, S, A_WIDTH)

        yb = sliding_window_attention(qb.reshape(B, S, B_HEADS, HEAD_DIM),
                                      kb.reshape(B, S, B_KV_HEADS, HEAD_DIM),
                                      vb.reshape(B, S, B_KV_HEADS, HEAD_DIM),
                                      sinks[l], slopes_b)

        logf = jax.nn.log_sigmoid(fc.astype(jnp.float32) + b_forget[l].astype(jnp.float32))
        yc = forgetting_attention(qc.reshape(B, S, C_HEADS, HEAD_DIM),
                                  kc.reshape(B, S, C_HEADS, HEAD_DIM),
                                  vc.reshape(B, S, C_HEADS, HEAD_DIM), logf)

        ua = jnp.einsum('bse,ed->bsd', ya * jax.nn.silu(ga), w_up_a[l])
        ub = jnp.einsum('bse,ed->bsd', yb * jax.nn.silu(gb), w_up_b[l])
        uc = jnp.einsum('bse,ed->bsd', yc * jax.nn.silu(gc), w_up_c[l])
        gates = jax.nn.sigmoid(gm.reshape(B, S, N_BRANCH, D))
        merged = gates[:, :, 0] * ua + gates[:, :, 1] * ub + gates[:, :, 2] * uc
        x = x + jnp.einsum('bsd,de->bse', merged, w_o[l])
    return rms_norm(x, final_gain, RMS_EPS)
```

```python
import functools
import math

import jax
import jax.numpy as jnp
from jax import lax
from jax.experimental import pallas as pl
from jax.experimental.pallas import tpu as pltpu

F32 = jnp.float32
BF16 = jnp.bfloat16

D_MODEL = 1024
HEAD_DIM = 64
LANES = 128
RMS_EPS = 1e-6
SUBLN_EPS = 1e-5
A_HEADS = 4
B_HEADS = 8
B_KV_HEADS = 2
B_GROUP = B_HEADS // B_KV_HEADS
C_HEADS = 8
WINDOW = 128
SWA_BLOCK = 128
N_BRANCH = 3
BRANCH_WIDTH = 512

_SPLITS = (512, 512, 512, 512, 512, 128, 128, 512, 512, 512, 512, 8, 512, 3 * D_MODEL)
_OFF = [0]
for _s in _SPLITS:
    _OFF.append(_OFF[-1] + _s)
(O_QA, O_KA, O_VA, O_GA, O_QB, O_KB, O_VB, O_GB, O_QC, O_KC, O_VC, O_FC, O_GC, O_GM, _D_IN) = _OFF

P_QA, P_KA, P_VA = 0, 4, 8
P_QC, P_KC, P_VC = 12, 16, 20
P_QB = 24
P_KB = 32
P_VB = 33
P_BLOCKS = 35
P_WIDTH = P_BLOCKS * LANES

VMEM_LIMIT = 56 * 1024 * 1024


def _cparams(sem):
    return pltpu.CompilerParams(dimension_semantics=sem, vmem_limit_bytes=VMEM_LIMIT)


def _rms(x, gain, eps):
    ms = jnp.mean(x * x, axis=-1, keepdims=True)
    return x * lax.rsqrt(ms + eps) * gain


def _dot_nt(a, b):
    return lax.dot_general(a, b, (((1,), (1,)), ((), ())), preferred_element_type=F32)


def _split3(v):
    hi = v.astype(BF16)
    r = v - hi.astype(F32)
    mid = r.astype(BF16)
    lo = (r - mid.astype(F32)).astype(BF16)
    return hi, mid, lo


def _pre_kernel(x_ref, g_ref, w_ref, wf_ref, bf_ref, proj_ref, negc_ref, carry_ref, *, tm, chunk):
    i = pl.program_id(1)

    @pl.when(i == 0)
    def _():
        carry_ref[...] = jnp.zeros_like(carry_ref)

    h = _rms(x_ref[...], g_ref[...], RMS_EPS).astype(BF16)
    for c0 in range(0, P_WIDTH, chunk):
        c1 = min(c0 + chunk, P_WIDTH)
        proj_ref[:, c0:c1] = jnp.dot(h, w_ref[:, c0:c1], preferred_element_type=F32).astype(BF16)

    z = _dot_nt(wf_ref[...], h) + bf_ref[...]
    logf = jnp.minimum(z, 0.0) - jnp.log1p(jnp.exp(-jnp.abs(z)))
    row = lax.broadcasted_iota(jnp.int32, (tm, tm), 0)
    col = lax.broadcasted_iota(jnp.int32, (tm, tm), 1)
    upper = (row <= col).astype(BF16)
    hi, mid, lo = _split3(logf)
    c = (jnp.dot(hi, upper, preferred_element_type=F32)
         + jnp.dot(mid, upper, preferred_element_type=F32)
         + jnp.dot(lo, upper, preferred_element_type=F32)) + carry_ref[:, 0:1]
    negc_ref[...] = -c
    carry_ref[...] = jnp.broadcast_to(c[:, tm - 1:tm], carry_ref.shape)


def _pre_call(x, norm_gain, w_pre, wf_t, b_f, layer, *, tm=512, chunk=512):
    B, S, D = x.shape
    kern = functools.partial(_pre_kernel, tm=tm, chunk=chunk)
    return pl.pallas_call(
        kern,
        grid=(B, S // tm),
        in_specs=[
            pl.BlockSpec((None, tm, D), lambda b, i: (b, i, 0)),
            pl.BlockSpec((None, 1, D), lambda b, i: (layer, 0, 0)),
            pl.BlockSpec((None, D, P_WIDTH), lambda b, i: (layer, 0, 0)),
            pl.BlockSpec((None, C_HEADS, D), lambda b, i: (layer, 0, 0)),
            pl.BlockSpec((None, C_HEADS, 1), lambda b, i: (layer, 0, 0)),
        ],
        out_specs=[
            pl.BlockSpec((None, tm, P_WIDTH), lambda b, i: (b, i, 0)),
            pl.BlockSpec((None, C_HEADS, tm), lambda b, i: (b, 0, i)),
        ],
        out_shape=[
            jax.ShapeDtypeStruct((B, S, P_WIDTH), BF16),
            jax.ShapeDtypeStruct((B, C_HEADS, S), F32),
        ],
        scratch_shapes=[pltpu.VMEM((C_HEADS, LANES), F32)],
        compiler_params=_cparams(("parallel", "arbitrary")),
        name="pre",
    )(x, norm_gain, w_pre, wf_t, b_f)


def _flash_kernel(*refs, mode, tq, tk):
    if mode == "diff":
        (consts_ref, q_ref, k_ref, v_ref, b0_ref, b1_ref, lam_ref, sg_ref,
         o_ref, qz_ref, m_ref, l_ref, acc_ref) = refs
    else:
        (q_ref, k_ref, v_ref, b0_ref, b1_ref, o_ref, qz_ref, m_ref, l_ref, acc_ref) = refs
    bias_refs = (b0_ref, b1_ref)
    qi = pl.program_id(2)

    lane = lax.broadcasted_iota(jnp.int32, (tq, LANES), 1)
    q = q_ref[...]
    zero = jnp.zeros_like(q)
    qz_ref[0] = jnp.where(lane < HEAD_DIM, q, zero)
    qz_ref[1] = jnp.where(lane >= HEAD_DIM, q, zero)
    m_ref[...] = jnp.full(m_ref.shape, -jnp.inf, F32)
    l_ref[...] = jnp.zeros_like(l_ref)
    acc_ref[...] = jnp.zeros_like(acc_ref)

    def step(j, masked):
        start = pl.multiple_of(j * tk, tk)
        kc = k_ref[pl.ds(start, tk), :]
        vc = v_ref[pl.ds(start, tk), :]
        for m in range(2):
            s = _dot_nt(qz_ref[m], kc) + bias_refs[m][:, pl.ds(start, tk)]
            if masked:
                r = lax.broadcasted_iota(jnp.int32, (tq, tk), 0)
                c = lax.broadcasted_iota(jnp.int32, (tq, tk), 1)
                s = jnp.where(c <= r, s, -jnp.inf)
            m_prev = m_ref[m]
            m_new = jnp.maximum(m_prev, jnp.max(s, axis=1, keepdims=True))
            alpha = jnp.exp(m_prev - m_new)
            p = jnp.exp(s - m_new[:, 0:1])
            l_ref[m] = alpha * l_ref[m] + jnp.sum(p, axis=1, keepdims=True)
            acc_ref[m] = alpha * acc_ref[m] + jnp.dot(p.astype(BF16), vc, preferred_element_type=F32)
            m_ref[m] = m_new

    def body(j, carry):
        step(j, False)
        return carry

    lax.fori_loop(0, qi, body, 0)
    step(qi, True)

    o0 = acc_ref[0] / l_ref[0]
    o1 = acc_ref[1] / l_ref[1]
    if mode == "diff":
        lam_init = consts_ref[0]
        lp = lam_ref[...]
        lam = (jnp.exp(jnp.sum(lp[0:1] * lp[1:2], axis=1, keepdims=True))
               - jnp.exp(jnp.sum(lp[2:3] * lp[3:4], axis=1, keepdims=True)) + lam_init)
        d = o0 - lam * o1
        y = _rms(d, sg_ref[...], SUBLN_EPS) * consts_ref[1]
    else:
        y = jnp.where(lane < HEAD_DIM, o0, o1)
    o_ref[...] = y.astype(o_ref.dtype)


def _flash_call(mode, proj, bias, q_blk, k_blk, v_blk, extra=(), *, tq=512):
    B, S, _ = proj.shape
    tk = tq
    n_pairs = 4
    kern = functools.partial(_flash_kernel, mode=mode, tq=tq, tk=tk)
    in_specs = [
        pl.BlockSpec((None, tq, LANES), lambda b, p, i: (b, i, q_blk + p)),
        pl.BlockSpec((None, S, LANES), lambda b, p, i: (b, 0, k_blk + p)),
        pl.BlockSpec((None, S, LANES), lambda b, p, i: (b, 0, v_blk + p)),
    ]
    if mode == "diff":
        consts, lam_rows, subln = extra
        in_specs = [pl.BlockSpec(memory_space=pltpu.SMEM)] + in_specs + [
            pl.BlockSpec((None, 1, S), lambda b, p, i: (p, 0, 0)),
            pl.BlockSpec((None, 1, S), lambda b, p, i: (p, 0, 0)),
            pl.BlockSpec((8, LANES), lambda b, p, i: (0, 0)),
            pl.BlockSpec((1, LANES), lambda b, p, i: (0, 0)),
        ]
        args = (consts, proj, proj, proj, bias, bias, lam_rows, subln)
    else:
        in_specs = in_specs + [
            pl.BlockSpec((None, None, 1, S), lambda b, p, i: (b, 2 * p, 0, 0)),
            pl.BlockSpec((None, None, 1, S), lambda b, p, i: (b, 2 * p + 1, 0, 0)),
        ]
        args = (proj, proj, proj, bias, bias)
    return pl.pallas_call(
        kern,
        grid=(B, n_pairs, S // tq),
        in_specs=in_specs,
        out_specs=pl.BlockSpec((None, tq, LANES), lambda b, p, i: (b, i, p)),
        out_shape=jax.ShapeDtypeStruct((B, S, BRANCH_WIDTH), BF16),
        scratch_shapes=[
            pltpu.VMEM((2, tq, LANES), BF16),
            pltpu.VMEM((2, tq, LANES), F32),
            pltpu.VMEM((2, tq, LANES), F32),
            pltpu.VMEM((2, tq, LANES), F32),
        ],
        compiler_params=_cparams(("parallel", "parallel", "arbitrary")),
        name="flash_" + mode,
    )(*args)


def _swa_kernel(slopes_ref, sinks_ref, q_ref, k_ref, v_ref, o_ref, *, tq):
    kv = pl.program_id(1)
    qi = pl.program_id(2)
    nb = tq // SWA_BLOCK
    lane = lax.broadcasted_iota(jnp.int32, (SWA_BLOCK, LANES), 1)
    r = lax.broadcasted_iota(jnp.int32, (SWA_BLOCK, 2 * SWA_BLOCK), 0)
    c = lax.broadcasted_iota(jnp.int32, (SWA_BLOCK, 2 * SWA_BLOCK), 1)
    for n in range(nb):
        q_start = qi * tq + n * SWA_BLOCK
        k_start = pl.multiple_of(jnp.maximum(q_start - SWA_BLOCK, 0), SWA_BLOCK)
        kw = k_ref[pl.ds(k_start, 2 * SWA_BLOCK), :]
        vw = v_ref[pl.ds(k_start, 2 * SWA_BLOCK), :]
        qs = jnp.concatenate(
            [q_ref[n * SWA_BLOCK:(n + 1) * SWA_BLOCK, g * LANES:(g + 1) * LANES] for g in range(B_GROUP)],
            axis=0)
        s_all = _dot_nt(qs, kw)
        dist = (q_start - k_start) + r - c
        valid = (dist >= 0) & (dist < WINDOW)
        distf = dist.astype(F32)
        ps = []
        for g in range(B_GROUP):
            slope = slopes_ref[kv * B_GROUP + g]
            sink = sinks_ref[kv * B_GROUP + g]
            s = s_all[g * SWA_BLOCK:(g + 1) * SWA_BLOCK]
            s = jnp.where(valid, s - slope * distf, -jnp.inf)
            mx = jnp.maximum(jnp.max(s, axis=1, keepdims=True), sink)
            e = jnp.exp(s - mx)
            denom = jnp.sum(e, axis=1, keepdims=True) + jnp.exp(sink - mx)
            ps.append((e / denom).astype(BF16))
        o_all = jnp.dot(jnp.concatenate(ps, axis=0), vw, preferred_element_type=F32)
        for h2 in range(B_GROUP // 2):
            oa = o_all[(2 * h2) * SWA_BLOCK:(2 * h2 + 1) * SWA_BLOCK]
            ob = o_all[(2 * h2 + 1) * SWA_BLOCK:(2 * h2 + 2) * SWA_BLOCK]
            o_ref[n * SWA_BLOCK:(n + 1) * SWA_BLOCK, h2 * LANES:(h2 + 1) * LANES] = (
                jnp.where(lane < HEAD_DIM, oa, ob).astype(o_ref.dtype))


def _swa_call(proj, slopes, sinks, *, tq=512):
    B, S, _ = proj.shape
    kern = functools.partial(_swa_kernel, tq=tq)
    qw = B_GROUP * LANES
    return pl.pallas_call(
        kern,
        grid=(B, B_KV_HEADS, S // tq),
        in_specs=[
            pl.BlockSpec(memory_space=pltpu.SMEM),
            pl.BlockSpec(memory_space=pltpu.SMEM),
            pl.BlockSpec((None, tq, qw), lambda b, kv, i: (b, i, P_QB // B_GROUP + kv)),
            pl.BlockSpec((None, S, LANES), lambda b, kv, i: (b, 0, P_KB)),
            pl.BlockSpec((None, S, LANES), lambda b, kv, i: (b, 0, P_VB + kv)),
        ],
        out_specs=pl.BlockSpec((None, tq, 2 * LANES), lambda b, kv, i: (b, i, kv)),
        out_shape=jax.ShapeDtypeStruct((B, S, BRANCH_WIDTH), BF16),
        compiler_params=_cparams(("parallel", "parallel", "arbitrary")),
        name="swa",
    )(slopes, sinks, proj, proj, proj)


def _post_kernel(x_ref, g_ref, fg_ref, ya_ref, yb_ref, yc_ref, wg_ref, wgm_ref, wua_ref, wub_ref, wuc_ref,
                 wo_ref, o_ref, *, final):
    x = x_ref[...]
    h = _rms(x, g_ref[...], RMS_EPS).astype(BF16)
    merged = None
    for br, (y_ref, wu_ref) in enumerate(((ya_ref, wua_ref), (yb_ref, wub_ref), (yc_ref, wuc_ref))):
        g = jnp.dot(h, wg_ref[:, br * BRANCH_WIDTH:(br + 1) * BRANCH_WIDTH], preferred_element_type=F32)
        t = (y_ref[...].astype(F32) * (g * jax.nn.sigmoid(g))).astype(BF16)
        u = jnp.dot(t, wu_ref[...], preferred_element_type=F32)
        gate = jax.nn.sigmoid(
            jnp.dot(h, wgm_ref[:, br * D_MODEL:(br + 1) * D_MODEL], preferred_element_type=F32))
        merged = gate * u if merged is None else merged + gate * u
    out = x + jnp.dot(merged.astype(BF16), wo_ref[...], preferred_element_type=F32)
    if final:
        out = _rms(out, fg_ref[...], RMS_EPS)
    o_ref[...] = out


def _post_call(x, norm_gain, final_gain, ya, yb, yc, w_g, w_gm, w_ua, w_ub, w_uc, w_o, layer, final, *, tm=512):
    B, S, D = x.shape
    kern = functools.partial(_post_kernel, final=final)
    tok = lambda w: pl.BlockSpec((None, tm, w), lambda b, i: (b, i, 0))
    lay = lambda r, c: pl.BlockSpec((None, r, c), lambda b, i: (layer, 0, 0))
    return pl.pallas_call(
        kern,
        grid=(B, S // tm),
        in_specs=[
            tok(D), lay(1, D), pl.BlockSpec((1, D), lambda b, i: (0, 0)),
            tok(BRANCH_WIDTH), tok(BRANCH_WIDTH), tok(BRANCH_WIDTH),
            lay(D, N_BRANCH * BRANCH_WIDTH), lay(D, N_BRANCH * D),
            lay(BRANCH_WIDTH, D), lay(BRANCH_WIDTH, D), lay(BRANCH_WIDTH, D), lay(D, D),
        ],
        out_specs=tok(D),
        out_shape=jax.ShapeDtypeStruct((B, S, D), F32),
        compiler_params=_cparams(("parallel", "parallel")),
        name="post",
    )(x, norm_gain, final_gain, ya, yb, yc, w_g, w_gm, w_ua, w_ub, w_uc, w_o)


def _alibi_slopes(n_heads):
    return 2.0 ** (-8.0 * jnp.arange(1, n_heads + 1, dtype=F32) / n_heads)


def _prepare_weights(w_in):
    scale = HEAD_DIM ** -0.5
    sl = lambda o, n: w_in[:, :, o:o + n]
    zeros64 = jnp.zeros(w_in.shape[:2] + (HEAD_DIM,), w_in.dtype)
    qb = sl(O_QB, 512) * scale
    qb_blocks = []
    for n in range(B_HEADS):
        qn = qb[:, :, n * HEAD_DIM:(n + 1) * HEAD_DIM]
        qb_blocks += [qn, zeros64] if n // B_GROUP == 0 else [zeros64, qn]
    vb = sl(O_VB, 128)
    v0, v1 = vb[:, :, :HEAD_DIM], vb[:, :, HEAD_DIM:]
    w_pre = jnp.concatenate(
        [sl(O_QA, 512) * scale, sl(O_KA, 512), sl(O_VA, 512),
         sl(O_QC, 512) * scale, sl(O_KC, 512), sl(O_VC, 512)]
        + qb_blocks + [sl(O_KB, 128), v0, v0, v1, v1], axis=-1).astype(BF16)
    wf_t = jnp.swapaxes(sl(O_FC, C_HEADS), 1, 2).astype(BF16)
    w_g = jnp.concatenate([sl(O_GA, 512), sl(O_GB, 512), sl(O_GC, 512)], axis=-1).astype(BF16)
    w_gm = sl(O_GM, N_BRANCH * D_MODEL).astype(BF16)
    return w_pre, wf_t, w_g, w_gm


def kernel(x, norm_gain, w_in, b_forget, lambda_q1, lambda_k1, lambda_q2, lambda_k2, subln_gain, sinks,
           w_up_a, w_up_b, w_up_c, w_o, final_gain):
    B, S, D = x.shape
    depth = w_in.shape[0]
    w_pre, wf_t, w_g, w_gm = _prepare_weights(w_in)
    w_ua, w_ub, w_uc, w_ob = (w.astype(BF16) for w in (w_up_a, w_up_b, w_up_c, w_o))
    gain3 = norm_gain.reshape(depth, 1, D)
    fgain = final_gain.reshape(1, D)
    bf3 = b_forget.astype(F32).reshape(depth, C_HEADS, 1)
    pos = jnp.arange(S, dtype=F32)
    alibi_a = (_alibi_slopes(A_HEADS)[:, None] * pos[None, :]).reshape(A_HEADS, 1, S)
    slopes_b = _alibi_slopes(B_HEADS)
    pad = lambda v: jnp.pad(v.astype(F32), (0, LANES - v.shape[0]))

    for l in range(depth):
        lam_init = 0.8 - 0.6 * math.exp(-0.3 * l)
        proj, negc = _pre_call(x, gain3, w_pre, wf_t, bf3, l)
        consts = jnp.array([lam_init, 1.0 - lam_init], F32)
        lam_rows = jnp.zeros((8, LANES), F32).at[0:4].set(
            jnp.stack([pad(lambda_q1[l]), pad(lambda_k1[l]), pad(lambda_q2[l]), pad(lambda_k2[l])]))
        ya = _flash_call("diff", proj, alibi_a, P_QA, P_KA, P_VA,
                         (consts, lam_rows, subln_gain[l].astype(F32).reshape(1, LANES)))
        yc = _flash_call("fox", proj, negc.reshape(B, C_HEADS, 1, S), P_QC, P_KC, P_VC)
        yb = _swa_call(proj, slopes_b, sinks[l].astype(F32))
        x = _post_call(x, gain3, fgain, ya, yb, yc, w_g, w_gm, w_ua, w_ub, w_uc, w_ob, l, l == depth - 1)
    return x
```

```python
import functools
import math

import numpy as np
import jax
import jax.numpy as jnp
from jax import lax
from jax.experimental import pallas as pl
from jax.experimental.pallas import tpu as pltpu

F32 = jnp.float32
BF16 = jnp.bfloat16

D_MODEL = 1024
HEAD_DIM = 64
LANES = 128
RMS_EPS = 1e-6
SUBLN_EPS = 1e-5
A_HEADS = 4
B_HEADS = 8
B_KV_HEADS = 2
B_GROUP = B_HEADS // B_KV_HEADS
C_HEADS = 8
WINDOW = 128
SWA_BLOCK = 128
N_BRANCH = 3
BRANCH_WIDTH = 512
N_PAIRS = 4
N_PIECES = 3
LOG2E = math.log2(math.e)

_SPLITS = (512, 512, 512, 512, 512, 128, 128, 512, 512, 512, 512, 8, 512, 3 * D_MODEL)
_OFF = [0]
for _s in _SPLITS:
    _OFF.append(_OFF[-1] + _s)
(O_QA, O_KA, O_VA, O_GA, O_QB, O_KB, O_VB, O_GB, O_QC, O_KC, O_VC, O_FC, O_GC, O_GM, _D_IN) = _OFF

P_QA, P_KA, P_VA = 0, 4, 8
P_QC, P_KC, P_VC = 12, 16, 20
P_QB = 24
P_KB = 32
P_VB = 33
P_MM = 35
P_AUG = 35
P_BLOCKS = 39
P_WIDTH = P_BLOCKS * LANES
MM_WIDTH = P_MM * LANES

VMEM_LIMIT = 56 * 1024 * 1024


def _cparams(sem):
    return pltpu.CompilerParams(dimension_semantics=sem, vmem_limit_bytes=VMEM_LIMIT)


def _rms(x, gain, eps):
    ms = jnp.mean(x * x, axis=-1, keepdims=True)
    return x * lax.rsqrt(ms + eps) * gain


def _dot(a, b):
    return jnp.dot(a, b, preferred_element_type=F32)


def _dot_nt(a, b):
    return lax.dot_general(a, b, (((1,), (1,)), ((), ())), preferred_element_type=F32)


def _split3(v):
    hi = v.astype(BF16)
    r = v - hi.astype(F32)
    mid = r.astype(BF16)
    lo = (r - mid.astype(F32)).astype(BF16)
    return hi, mid, lo


def _pre_kernel(x_ref, g_ref, w_ref, wf_ref, bf_ref, place_ref, proj_ref, carry_ref, *, tm, chunk):
    i = pl.program_id(1)

    @pl.when(i == 0)
    def _():
        carry_ref[...] = jnp.zeros_like(carry_ref)

    h = _rms(x_ref[...], g_ref[...], RMS_EPS).astype(BF16)
    for c0 in range(0, MM_WIDTH, chunk):
        c1 = min(c0 + chunk, MM_WIDTH)
        proj_ref[:, c0:c1] = _dot(h, w_ref[:, c0:c1]).astype(BF16)

    z = _dot(h, wf_ref[...]) + bf_ref[...]
    logf = jnp.minimum(z, 0.0) - jnp.log1p(jnp.exp(-jnp.abs(z)))
    row = lax.broadcasted_iota(jnp.int32, (tm, tm), 0)
    col = lax.broadcasted_iota(jnp.int32, (tm, tm), 1)
    lower = (col <= row).astype(BF16)
    hi, mid, lo = _split3(logf)
    c = _dot(lower, hi) + _dot(lower, mid) + _dot(lower, lo) + carry_ref[0:1, :]
    carry_ref[...] = jnp.broadcast_to(c[tm - 1:tm, :], carry_ref.shape)
    b_hi, b_mid, b_lo = _split3(c * (-LOG2E))
    aug = _dot(b_hi, place_ref[0]) + _dot(b_mid, place_ref[1]) + _dot(b_lo, place_ref[2])
    proj_ref[:, MM_WIDTH:P_WIDTH] = aug.astype(BF16)


def _pre_call(x, norm_gain, w_pre, wf, b_f, place, layer, *, tm=512, chunk=512):
    B, S, D = x.shape
    kern = functools.partial(_pre_kernel, tm=tm, chunk=chunk)
    return pl.pallas_call(
        kern,
        grid=(B, S // tm),
        in_specs=[
            pl.BlockSpec((None, tm, D), lambda b, i: (b, i, 0)),
            pl.BlockSpec((None, 1, D), lambda b, i: (layer, 0, 0)),
            pl.BlockSpec((None, D, MM_WIDTH), lambda b, i: (layer, 0, 0)),
            pl.BlockSpec((None, D, LANES), lambda b, i: (layer, 0, 0)),
            pl.BlockSpec((None, 1, LANES), lambda b, i: (layer, 0, 0)),
            pl.BlockSpec((N_PIECES, LANES, N_PAIRS * LANES), lambda b, i: (0, 0, 0)),
        ],
        out_specs=pl.BlockSpec((None, tm, P_WIDTH), lambda b, i: (b, i, 0)),
        out_shape=jax.ShapeDtypeStruct((B, S, P_WIDTH), BF16),
        scratch_shapes=[pltpu.VMEM((8, LANES), F32)],
        compiler_params=_cparams(("parallel", "arbitrary")),
        name="pre",
    )(x, norm_gain, w_pre, wf, b_f, place)


def _flash_kernel(*refs, mode, tq):
    if mode == "diff":
        (consts_ref, q_ref, k_ref, v_ref, aug_ref, lam_ref, sg_ref, o_ref, qe_ref, m_ref, acc_ref) = refs
    else:
        (q_ref, k_ref, v_ref, aug_ref, o_ref, qe_ref, m_ref, acc_ref) = refs
    qi = pl.program_id(2)

    lane = lax.broadcasted_iota(jnp.int32, (tq, LANES), 1)
    q = q_ref[...]
    zero = jnp.zeros_like(q)
    for m in range(2):
        in_map = (lane >= HEAD_DIM) if m else (lane < HEAD_DIM)
        first = N_PIECES * m if mode == "fox" else 0
        qe_ref[m, :, 0:LANES] = jnp.where(in_map, q, zero)
        ones_at_pieces = jnp.where(lane >= first, 1.0, 0.0) * jnp.where(lane < first + N_PIECES, 1.0, 0.0)
        qe_ref[m, :, LANES:2 * LANES] = ones_at_pieces.astype(BF16)
    m_ref[...] = jnp.full(m_ref.shape, -jnp.inf, F32)
    acc_ref[...] = jnp.zeros_like(acc_ref)

    def step(start, width, masked):
        kc = jnp.concatenate([k_ref[pl.ds(start, width), :], aug_ref[pl.ds(start, width), :]], axis=1)
        vc = jnp.concatenate([v_ref[pl.ds(start, width), :], jnp.ones((width, LANES), BF16)], axis=1)
        for m in range(2):
            s = _dot_nt(qe_ref[m], kc)
            if masked:
                r = lax.broadcasted_iota(jnp.int32, (tq, width), 0)
                c = lax.broadcasted_iota(jnp.int32, (tq, width), 1)
                s = jnp.where(c <= r, s, -jnp.inf)
            cols = [s[:, c0:c0 + LANES] for c0 in range(0, width, LANES)]
            mx = cols[0]
            for sc in cols[1:]:
                mx = jnp.maximum(mx, sc)
            m_prev = m_ref[m]
            m_new = jnp.maximum(m_prev, jnp.max(mx, axis=1, keepdims=True))
            alpha = jnp.exp2(m_prev - m_new)
            p = jnp.concatenate([jnp.exp2(sc - m_new).astype(BF16) for sc in cols], axis=1)
            pv = _dot(p, vc)
            acc_ref[m, :, 0:LANES] = alpha * acc_ref[m, :, 0:LANES] + pv[:, 0:LANES]
            acc_ref[m, :, LANES:2 * LANES] = alpha * acc_ref[m, :, LANES:2 * LANES] + pv[:, LANES:2 * LANES]
            m_ref[m] = m_new

    wide = 2 * tq

    def body(j, carry):
        step(pl.multiple_of(j * wide, wide), wide, False)
        return carry

    lax.fori_loop(0, qi // 2, body, 0)

    @pl.when(qi % 2 == 1)
    def _():
        step(pl.multiple_of((qi - 1) * tq, tq), tq, False)

    step(pl.multiple_of(qi * tq, tq), tq, True)

    o0 = acc_ref[0, :, 0:LANES] / acc_ref[0, :, LANES:2 * LANES]
    o1 = acc_ref[1, :, 0:LANES] / acc_ref[1, :, LANES:2 * LANES]
    if mode == "diff":
        lam_init = consts_ref[0]
        lp = lam_ref[...]
        lam = (jnp.exp(jnp.sum(lp[0:1] * lp[1:2], axis=1, keepdims=True))
               - jnp.exp(jnp.sum(lp[2:3] * lp[3:4], axis=1, keepdims=True)) + lam_init)
        d = o0 - lam * o1
        y = _rms(d, sg_ref[...], SUBLN_EPS) * consts_ref[1]
    else:
        y = jnp.where(lane < HEAD_DIM, o0, o1)
    o_ref[...] = y.astype(o_ref.dtype)


def _flash_call(mode, proj, q_blk, k_blk, v_blk, extra=(), *, tq=512):
    B, S, _ = proj.shape
    kern = functools.partial(_flash_kernel, mode=mode, tq=tq)
    in_specs = [
        pl.BlockSpec((None, tq, LANES), lambda b, p, i: (b, i, q_blk + p)),
        pl.BlockSpec((None, S, LANES), lambda b, p, i: (b, 0, k_blk + p)),
        pl.BlockSpec((None, S, LANES), lambda b, p, i: (b, 0, v_blk + p)),
    ]
    if mode == "diff":
        consts, alibi_aug, lam_rows, subln = extra
        in_specs = [pl.BlockSpec(memory_space=pltpu.SMEM)] + in_specs + [
            pl.BlockSpec((None, S, LANES), lambda b, p, i: (p, 0, 0)),
            pl.BlockSpec((8, LANES), lambda b, p, i: (0, 0)),
            pl.BlockSpec((1, LANES), lambda b, p, i: (0, 0)),
        ]
        args = (consts, proj, proj, proj, alibi_aug, lam_rows, subln)
    else:
        in_specs = in_specs + [pl.BlockSpec((None, S, LANES), lambda b, p, i: (b, 0, P_AUG + p))]
        args = (proj, proj, proj, proj)
    return pl.pallas_call(
        kern,
        grid=(B, N_PAIRS, S // tq),
        in_specs=in_specs,
        out_specs=pl.BlockSpec((None, tq, LANES), lambda b, p, i: (b, i, p)),
        out_shape=jax.ShapeDtypeStruct((B, S, BRANCH_WIDTH), BF16),
        scratch_shapes=[
            pltpu.VMEM((2, tq, 2 * LANES), BF16),
            pltpu.VMEM((2, tq, LANES), F32),
            pltpu.VMEM((2, tq, 2 * LANES), F32),
        ],
        compiler_params=_cparams(("parallel", "parallel", "arbitrary")),
        name="flash_" + mode,
    )(*args)


def _swa_kernel(slopes_ref, sinks_ref, q_ref, k_ref, v_ref, o_ref, *, tq):
    kv = pl.program_id(1)
    qi = pl.program_id(2)
    nb = tq // SWA_BLOCK
    lane = lax.broadcasted_iota(jnp.int32, (SWA_BLOCK, LANES), 1)
    r = lax.broadcasted_iota(jnp.int32, (SWA_BLOCK, 2 * SWA_BLOCK), 0)
    c = lax.broadcasted_iota(jnp.int32, (SWA_BLOCK, 2 * SWA_BLOCK), 1)
    for n in range(nb):
        q_start = qi * tq + n * SWA_BLOCK
        k_start = pl.multiple_of(jnp.maximum(q_start - SWA_BLOCK, 0), SWA_BLOCK)
        kw = k_ref[pl.ds(k_start, 2 * SWA_BLOCK), :]
        vw = v_ref[pl.ds(k_start, 2 * SWA_BLOCK), :]
        qs = jnp.concatenate(
            [q_ref[n * SWA_BLOCK:(n + 1) * SWA_BLOCK, g * LANES:(g + 1) * LANES] for g in range(B_GROUP)],
            axis=0)
        s_all = _dot_nt(qs, kw)
        dist = (q_start - k_start) + r - c
        valid = (dist >= 0) & (dist < WINDOW)
        distf = dist.astype(F32)
        ps = []
        for g in range(B_GROUP):
            slope = slopes_ref[kv * B_GROUP + g]
            sink = sinks_ref[kv * B_GROUP + g]
            s = s_all[g * SWA_BLOCK:(g + 1) * SWA_BLOCK]
            s = jnp.where(valid, s - slope * distf, -jnp.inf)
            mx = jnp.maximum(jnp.max(s, axis=1, keepdims=True), sink)
            e = jnp.exp(s - mx)
            denom = jnp.sum(e, axis=1, keepdims=True) + jnp.exp(sink - mx)
            ps.append((e / denom).astype(BF16))
        o_all = _dot(jnp.concatenate(ps, axis=0), vw)
        for h2 in range(B_GROUP // 2):
            oa = o_all[(2 * h2) * SWA_BLOCK:(2 * h2 + 1) * SWA_BLOCK]
            ob = o_all[(2 * h2 + 1) * SWA_BLOCK:(2 * h2 + 2) * SWA_BLOCK]
            o_ref[n * SWA_BLOCK:(n + 1) * SWA_BLOCK, h2 * LANES:(h2 + 1) * LANES] = (
                jnp.where(lane < HEAD_DIM, oa, ob).astype(o_ref.dtype))


def _swa_call(proj, slopes, sinks, *, tq=512):
    B, S, _ = proj.shape
    kern = functools.partial(_swa_kernel, tq=tq)
    qw = B_GROUP * LANES
    return pl.pallas_call(
        kern,
        grid=(B, B_KV_HEADS, S // tq),
        in_specs=[
            pl.BlockSpec(memory_space=pltpu.SMEM),
            pl.BlockSpec(memory_space=pltpu.SMEM),
            pl.BlockSpec((None, tq, qw), lambda b, kv, i: (b, i, P_QB // B_GROUP + kv)),
            pl.BlockSpec((None, S, LANES), lambda b, kv, i: (b, 0, P_KB)),
            pl.BlockSpec((None, S, LANES), lambda b, kv, i: (b, 0, P_VB + kv)),
        ],
        out_specs=pl.BlockSpec((None, tq, 2 * LANES), lambda b, kv, i: (b, i, kv)),
        out_shape=jax.ShapeDtypeStruct((B, S, BRANCH_WIDTH), BF16),
        compiler_params=_cparams(("parallel", "parallel", "arbitrary")),
        name="swa",
    )(slopes, sinks, proj, proj, proj)


def _post_kernel(x_ref, g_ref, fg_ref, ya_ref, yb_ref, yc_ref, wg_ref, wgm_ref, wua_ref, wub_ref, wuc_ref,
                 wo_ref, o_ref, *, final):
    x = x_ref[...]
    h = _rms(x, g_ref[...], RMS_EPS).astype(BF16)
    merged = None
    for br, (y_ref, wu_ref) in enumerate(((ya_ref, wua_ref), (yb_ref, wub_ref), (yc_ref, wuc_ref))):
        g = _dot(h, wg_ref[:, br * BRANCH_WIDTH:(br + 1) * BRANCH_WIDTH])
        t = (y_ref[...].astype(F32) * (g * jax.nn.sigmoid(g))).astype(BF16)
        u = _dot(t, wu_ref[...])
        gate = jax.nn.sigmoid(_dot(h, wgm_ref[:, br * D_MODEL:(br + 1) * D_MODEL]))
        merged = gate * u if merged is None else merged + gate * u
    out = x + _dot(merged.astype(BF16), wo_ref[...])
    if final:
        out = _rms(out, fg_ref[...], RMS_EPS)
    o_ref[...] = out


def _post_call(x, norm_gain, final_gain, ya, yb, yc, w_g, w_gm, w_ua, w_ub, w_uc, w_o, layer, final, *, tm=512):
    B, S, D = x.shape
    kern = functools.partial(_post_kernel, final=final)
    tok = lambda w: pl.BlockSpec((None, tm, w), lambda b, i: (b, i, 0))
    lay = lambda r, c: pl.BlockSpec((None, r, c), lambda b, i: (layer, 0, 0))
    return pl.pallas_call(
        kern,
        grid=(B, S // tm),
        in_specs=[
            tok(D), lay(1, D), pl.BlockSpec((1, D), lambda b, i: (0, 0)),
            tok(BRANCH_WIDTH), tok(BRANCH_WIDTH), tok(BRANCH_WIDTH),
            lay(D, N_BRANCH * BRANCH_WIDTH), lay(D, N_BRANCH * D),
            lay(BRANCH_WIDTH, D), lay(BRANCH_WIDTH, D), lay(BRANCH_WIDTH, D), lay(D, D),
        ],
        out_specs=tok(D),
        out_shape=jax.ShapeDtypeStruct((B, S, D), F32),
        compiler_params=_cparams(("parallel", "parallel")),
        name="post",
    )(x, norm_gain, final_gain, ya, yb, yc, w_g, w_gm, w_ua, w_ub, w_uc, w_o)


def _alibi_slopes(n_heads):
    return 2.0 ** (-8.0 * jnp.arange(1, n_heads + 1, dtype=F32) / n_heads)


def _prepare_weights(w_in):
    scale = HEAD_DIM ** -0.5
    scale2 = scale * LOG2E
    sl = lambda o, n: w_in[:, :, o:o + n]
    zeros64 = jnp.zeros(w_in.shape[:2] + (HEAD_DIM,), w_in.dtype)
    qb = sl(O_QB, 512) * scale
    qb_blocks = []
    for n in range(B_HEADS):
        qn = qb[:, :, n * HEAD_DIM:(n + 1) * HEAD_DIM]
        qb_blocks += [qn, zeros64] if n // B_GROUP == 0 else [zeros64, qn]
    vb = sl(O_VB, 128)
    v0, v1 = vb[:, :, :HEAD_DIM], vb[:, :, HEAD_DIM:]
    w_pre = jnp.concatenate(
        [sl(O_QA, 512) * scale2, sl(O_KA, 512), sl(O_VA, 512),
         sl(O_QC, 512) * scale2, sl(O_KC, 512), sl(O_VC, 512)]
        + qb_blocks + [sl(O_KB, 128), v0, v0, v1, v1], axis=-1).astype(BF16)
    wf = jnp.pad(sl(O_FC, C_HEADS), ((0, 0), (0, 0), (0, LANES - C_HEADS))).astype(BF16)
    w_g = jnp.concatenate([sl(O_GA, 512), sl(O_GB, 512), sl(O_GC, 512)], axis=-1).astype(BF16)
    w_gm = sl(O_GM, N_BRANCH * D_MODEL).astype(BF16)
    return w_pre, wf, w_g, w_gm


def _placement():
    place = np.zeros((N_PIECES, LANES, N_PAIRS * LANES), np.float32)
    for x in range(N_PIECES):
        for h in range(C_HEADS):
            place[x, h, (h // 2) * LANES + N_PIECES * (h % 2) + x] = 1.0
    return jnp.asarray(place, BF16)


def _alibi_aug(seq):
    slopes = 2.0 ** (-8.0 * np.arange(1, A_HEADS + 1, dtype=np.float64) / A_HEADS)
    rest = (slopes[:, None] * (LOG2E * np.arange(seq, dtype=np.float64))[None, :]).astype(np.float32)
    table = np.zeros((A_HEADS, seq, LANES), np.float32)
    for x in range(N_PIECES):
        piece = rest.astype(BF16).astype(np.float32)
        table[:, :, x] = piece
        rest = rest - piece
    return jnp.asarray(table, BF16)


def kernel(x, norm_gain, w_in, b_forget, lambda_q1, lambda_k1, lambda_q2, lambda_k2, subln_gain, sinks,
           w_up_a, w_up_b, w_up_c, w_o, final_gain):
    B, S, D = x.shape
    depth = w_in.shape[0]
    w_pre, wf, w_g, w_gm = _prepare_weights(w_in)
    w_ua, w_ub, w_uc, w_ob = (w.astype(BF16) for w in (w_up_a, w_up_b, w_up_c, w_o))
    gain3 = norm_gain.reshape(depth, 1, D)
    fgain = final_gain.reshape(1, D)
    bf3 = jnp.pad(b_forget.astype(F32), ((0, 0), (0, LANES - C_HEADS))).reshape(depth, 1, LANES)
    place = _placement()
    alibi_aug = _alibi_aug(S)
    slopes_b = _alibi_slopes(B_HEADS)
    pad = lambda v: jnp.pad(v.astype(F32), (0, LANES - v.shape[0]))

    for l in range(depth):
        lam_init = 0.8 - 0.6 * math.exp(-0.3 * l)
        proj = _pre_call(x, gain3, w_pre, wf, bf3, place, l)
        consts = jnp.array([lam_init, 1.0 - lam_init], F32)
        lam_rows = jnp.zeros((8, LANES), F32).at[0:4].set(
            jnp.stack([pad(lambda_q1[l]), pad(lambda_k1[l]), pad(lambda_q2[l]), pad(lambda_k2[l])]))
        ya = _flash_call("diff", proj, P_QA, P_KA, P_VA,
                         (consts, alibi_aug, lam_rows, subln_gain[l].astype(F32).reshape(1, LANES)))
        yc = _flash_call("fox", proj, P_QC, P_KC, P_VC)
        yb = _swa_call(proj, slopes_b, sinks[l].astype(F32))
        x = _post_call(x, gain3, fgain, ya, yb, yc, w_g, w_gm, w_ua, w_ub, w_uc, w_ob, l, l == depth - 1)
    return x
```

```python
import functools
import math

import numpy as np
import jax
import jax.numpy as jnp
from jax import lax
from jax.experimental import pallas as pl
from jax.experimental.pallas import tpu as pltpu

F32 = jnp.float32
BF16 = jnp.bfloat16

D_MODEL = 1024
HEAD_DIM = 64
LANES = 128
RMS_EPS = 1e-6
SUBLN_EPS = 1e-5
A_HEADS = 4
B_HEADS = 8
B_KV_HEADS = 2
B_GROUP = B_HEADS // B_KV_HEADS
C_HEADS = 8
WINDOW = 128
SWA_BLOCK = 128
N_BRANCH = 3
BRANCH_WIDTH = 512
N_PAIRS = 4
N_PIECES = 3
LOG2E = math.log2(math.e)

_SPLITS = (512, 512, 512, 512, 512, 128, 128, 512, 512, 512, 512, 8, 512, 3 * D_MODEL)
_OFF = [0]
for _s in _SPLITS:
    _OFF.append(_OFF[-1] + _s)
(O_QA, O_KA, O_VA, O_GA, O_QB, O_KB, O_VB, O_GB, O_QC, O_KC, O_VC, O_FC, O_GC, O_GM, _D_IN) = _OFF

P_QA, P_KA, P_VA = 0, 4, 8
P_QC, P_KC, P_VC = 12, 16, 20
P_QB = 24
P_KB = 32
P_VB = 33
P_MM = 35
P_AUG = 35
P_BLOCKS = 39
P_WIDTH = P_BLOCKS * LANES
MM_WIDTH = P_MM * LANES

VMEM_LIMIT = 56 * 1024 * 1024


def _cparams(sem):
    return pltpu.CompilerParams(dimension_semantics=sem, vmem_limit_bytes=VMEM_LIMIT)


def _rms(x, gain, eps):
    ms = jnp.mean(x * x, axis=-1, keepdims=True)
    return x * lax.rsqrt(ms + eps) * gain


def _dot(a, b):
    return jnp.dot(a, b, preferred_element_type=F32)


def _dot_nt(a, b):
    return lax.dot_general(a, b, (((1,), (1,)), ((), ())), preferred_element_type=F32)


def _split3(v):
    hi = v.astype(BF16)
    r = v - hi.astype(F32)
    mid = r.astype(BF16)
    lo = (r - mid.astype(F32)).astype(BF16)
    return hi, mid, lo


def _pre_kernel(x_ref, g_ref, w_ref, wf_ref, bf_ref, place_ref, proj_ref, carry_ref, *, tm, chunk):
    i = pl.program_id(1)

    @pl.when(i == 0)
    def _():
        carry_ref[...] = jnp.zeros_like(carry_ref)

    h = _rms(x_ref[...], g_ref[...], RMS_EPS).astype(BF16)
    for c0 in range(0, MM_WIDTH, chunk):
        c1 = min(c0 + chunk, MM_WIDTH)
        proj_ref[:, c0:c1] = _dot(h, w_ref[:, c0:c1]).astype(BF16)

    z = _dot(h, wf_ref[...]) + bf_ref[...]
    logf = jnp.minimum(z, 0.0) - jnp.log1p(jnp.exp(-jnp.abs(z)))
    row = lax.broadcasted_iota(jnp.int32, (tm, tm), 0)
    col = lax.broadcasted_iota(jnp.int32, (tm, tm), 1)
    lower = (col <= row).astype(BF16)
    hi, mid, lo = _split3(logf)
    c = _dot(lower, hi) + _dot(lower, mid) + _dot(lower, lo) + carry_ref[0:1, :]
    carry_ref[...] = jnp.broadcast_to(c[tm - 1:tm, :], carry_ref.shape)
    b_hi, b_mid, b_lo = _split3(c * (-LOG2E))
    aug = _dot(b_hi, place_ref[0]) + _dot(b_mid, place_ref[1]) + _dot(b_lo, place_ref[2])
    proj_ref[:, MM_WIDTH:P_WIDTH] = aug.astype(BF16)


def _pre_call(x, norm_gain, w_pre, wf, b_f, place, layer, *, tm=512, chunk=512):
    B, S, D = x.shape
    kern = functools.partial(_pre_kernel, tm=tm, chunk=chunk)
    return pl.pallas_call(
        kern,
        grid=(B, S // tm),
        in_specs=[
            pl.BlockSpec((None, tm, D), lambda b, i: (b, i, 0)),
            pl.BlockSpec((None, 1, D), lambda b, i: (layer, 0, 0)),
            pl.BlockSpec((None, D, MM_WIDTH), lambda b, i: (layer, 0, 0)),
            pl.BlockSpec((None, D, LANES), lambda b, i: (layer, 0, 0)),
            pl.BlockSpec((None, 1, LANES), lambda b, i: (layer, 0, 0)),
            pl.BlockSpec((N_PIECES, LANES, N_PAIRS * LANES), lambda b, i: (0, 0, 0)),
        ],
        out_specs=pl.BlockSpec((None, tm, P_WIDTH), lambda b, i: (b, i, 0)),
        out_shape=jax.ShapeDtypeStruct((B, S, P_WIDTH), BF16),
        scratch_shapes=[pltpu.VMEM((8, LANES), F32)],
        compiler_params=_cparams(("parallel", "arbitrary")),
        name="pre",
    )(x, norm_gain, w_pre, wf, b_f, place)


def _chunk_order(nq):
    order = [(t, t) for t in range(nq)]
    order += [(qi, c) for qi in range(1, nq) for c in range(qi)]
    return order


def _flash_kernel(*refs, mode, tq, nq):
    if mode == "diff":
        (consts_ref, tab_ref, q_ref, k_ref, v_ref, aug_ref, lam_ref, sg_ref, o_ref,
         qe_ref, m_ref, acc_ref, sa_ref, sb_ref) = refs
    else:
        (tab_ref, q_ref, k_ref, v_ref, aug_ref, o_ref, qe_ref, m_ref, acc_ref, sa_ref, sb_ref) = refs
    tk = tq
    rows = 2 * tq
    n_chunks = nq * (nq + 1) // 2
    lane = lax.broadcasted_iota(jnp.int32, (tq, LANES), 1)

    def build_qe(t, carry):
        q = q_ref[pl.ds(pl.multiple_of(t * tq, tq), tq), :]
        zero = jnp.zeros_like(q)
        for m in range(2):
            in_map = (lane >= HEAD_DIM) if m else (lane < HEAD_DIM)
            first = N_PIECES * m if mode == "fox" else 0
            ones_at_pieces = jnp.where(lane >= first, 1.0, 0.0) * jnp.where(lane < first + N_PIECES, 1.0, 0.0)
            qe_ref[t, m * tq:(m + 1) * tq, 0:LANES] = jnp.where(in_map, q, zero)
            qe_ref[t, m * tq:(m + 1) * tq, LANES:2 * LANES] = ones_at_pieces.astype(BF16)
        return carry

    lax.fori_loop(0, nq, build_qe, 0)

    def qk(t):
        qi = tab_ref[0, t]
        start = pl.multiple_of(tab_ref[1, t] * tk, tk)
        kc = jnp.concatenate([k_ref[pl.ds(start, tk), :], aug_ref[pl.ds(start, tk), :]], axis=1)
        return _dot_nt(qe_ref[qi], kc)

    def softmax_pv(s_ref, t, diagonal):
        qi = tab_ref[0, t]
        start = pl.multiple_of(tab_ref[1, t] * tk, tk)
        vc = jnp.concatenate([v_ref[pl.ds(start, tk), :], jnp.ones((tk, LANES), BF16)], axis=1)
        s = s_ref[...]
        if diagonal:
            r = lax.broadcasted_iota(jnp.int32, (rows, tk), 0)
            c = lax.broadcasted_iota(jnp.int32, (rows, tk), 1)
            s = jnp.where(c <= jnp.where(r >= tq, r - tq, r), s, -jnp.inf)
        cols = [s[:, c0:c0 + LANES] for c0 in range(0, tk, LANES)]
        mx = cols[0]
        for sc in cols[1:]:
            mx = jnp.maximum(mx, sc)
        m_cur = jnp.max(mx, axis=1, keepdims=True)
        if diagonal:
            m_new = jnp.broadcast_to(m_cur, (rows, LANES))
        else:
            m_prev = m_ref[qi]
            m_new = jnp.maximum(m_prev, m_cur)
            alpha = jnp.exp2(m_prev - m_new)
        p = jnp.concatenate([jnp.exp2(sc - m_new).astype(BF16) for sc in cols], axis=1)
        pv = _dot(p, vc)
        if diagonal:
            acc_ref[qi] = pv
        else:
            acc_ref[qi, :, 0:LANES] = alpha * acc_ref[qi, :, 0:LANES] + pv[:, 0:LANES]
            acc_ref[qi, :, LANES:2 * LANES] = alpha * acc_ref[qi, :, LANES:2 * LANES] + pv[:, LANES:2 * LANES]
        m_ref[qi] = m_new

    def two_chunks(diagonal):
        def body(i, carry):
            t = 2 * i
            sb_ref[...] = qk(t + 1)
            softmax_pv(sa_ref, t, diagonal)
            sa_ref[...] = qk(t + 2)
            softmax_pv(sb_ref, t + 1, diagonal)
            return carry
        return body

    sa_ref[...] = qk(0)
    lax.fori_loop(0, nq // 2, two_chunks(True), 0)
    lax.fori_loop(nq // 2, n_chunks // 2, two_chunks(False), 0)

    def finalize(t, carry):
        o0 = acc_ref[t, 0:tq, 0:LANES] / acc_ref[t, 0:tq, LANES:2 * LANES]
        o1 = acc_ref[t, tq:rows, 0:LANES] / acc_ref[t, tq:rows, LANES:2 * LANES]
        if mode == "diff":
            lam_init = consts_ref[0]
            lp = lam_ref[...]
            lam = (jnp.exp(jnp.sum(lp[0:1] * lp[1:2], axis=1, keepdims=True))
                   - jnp.exp(jnp.sum(lp[2:3] * lp[3:4], axis=1, keepdims=True)) + lam_init)
            y = _rms(o0 - lam * o1, sg_ref[...], SUBLN_EPS) * consts_ref[1]
        else:
            y = jnp.where(lane < HEAD_DIM, o0, o1)
        o_ref[pl.ds(pl.multiple_of(t * tq, tq), tq), :] = y.astype(o_ref.dtype)
        return carry

    lax.fori_loop(0, nq, finalize, 0)


def _flash_call(mode, proj, q_blk, k_blk, v_blk, extra=(), *, tq=512):
    B, S, _ = proj.shape
    nq = S // tq
    order = _chunk_order(nq)
    assert nq % 2 == 0 and len(order) % 2 == 0, "the chunk loop handles two chunks per trip"
    table = jnp.asarray(np.array(order + [order[-1]], np.int32).T)
    kern = functools.partial(_flash_kernel, mode=mode, tq=tq, nq=nq)
    seq = lambda blk: pl.BlockSpec((None, S, LANES), lambda b, p: (b, 0, blk + p))
    smem = pl.BlockSpec(memory_space=pltpu.SMEM)
    in_specs = [smem, seq(q_blk), seq(k_blk), seq(v_blk)]
    if mode == "diff":
        consts, alibi_aug, lam_rows, subln = extra
        in_specs = [smem] + in_specs + [
            pl.BlockSpec((None, S, LANES), lambda b, p: (p, 0, 0)),
            pl.BlockSpec((8, LANES), lambda b, p: (0, 0)),
            pl.BlockSpec((1, LANES), lambda b, p: (0, 0)),
        ]
        args = (consts, table, proj, proj, proj, alibi_aug, lam_rows, subln)
    else:
        in_specs = in_specs + [seq(P_AUG)]
        args = (table, proj, proj, proj, proj)
    return pl.pallas_call(
        kern,
        grid=(B, N_PAIRS),
        in_specs=in_specs,
        out_specs=pl.BlockSpec((None, S, LANES), lambda b, p: (b, 0, p)),
        out_shape=jax.ShapeDtypeStruct((B, S, BRANCH_WIDTH), BF16),
        scratch_shapes=[
            pltpu.VMEM((nq, 2 * tq, 2 * LANES), BF16),
            pltpu.VMEM((nq, 2 * tq, LANES), F32),
            pltpu.VMEM((nq, 2 * tq, 2 * LANES), F32),
            pltpu.VMEM((2 * tq, tq), F32),
            pltpu.VMEM((2 * tq, tq), F32),
        ],
        compiler_params=_cparams(("parallel", "parallel")),
        name="flash_" + mode,
    )(*args)


def _swa_kernel(slopes_ref, sinks_ref, q_ref, k_ref, v_ref, o_ref, *, tq):
    kv = pl.program_id(1)
    qi = pl.program_id(2)
    nb = tq // SWA_BLOCK
    lane = lax.broadcasted_iota(jnp.int32, (SWA_BLOCK, LANES), 1)
    r = lax.broadcasted_iota(jnp.int32, (SWA_BLOCK, 2 * SWA_BLOCK), 0)
    c = lax.broadcasted_iota(jnp.int32, (SWA_BLOCK, 2 * SWA_BLOCK), 1)
    for n in range(nb):
        q_start = qi * tq + n * SWA_BLOCK
        k_start = pl.multiple_of(jnp.maximum(q_start - SWA_BLOCK, 0), SWA_BLOCK)
        kw = k_ref[pl.ds(k_start, 2 * SWA_BLOCK), :]
        vw = v_ref[pl.ds(k_start, 2 * SWA_BLOCK), :]
        qs = jnp.concatenate(
            [q_ref[n * SWA_BLOCK:(n + 1) * SWA_BLOCK, g * LANES:(g + 1) * LANES] for g in range(B_GROUP)],
            axis=0)
        s_all = _dot_nt(qs, kw)
        dist = (q_start - k_start) + r - c
        valid = (dist >= 0) & (dist < WINDOW)
        distf = dist.astype(F32)
        ps = []
        for g in range(B_GROUP):
            slope = slopes_ref[kv * B_GROUP + g]
            sink = sinks_ref[kv * B_GROUP + g]
            s = s_all[g * SWA_BLOCK:(g + 1) * SWA_BLOCK]
            s = jnp.where(valid, s - slope * distf, -jnp.inf)
            mx = jnp.maximum(jnp.max(s, axis=1, keepdims=True), sink)
            e = jnp.exp(s - mx)
            denom = jnp.sum(e, axis=1, keepdims=True) + jnp.exp(sink - mx)
            ps.append((e / denom).astype(BF16))
        o_all = _dot(jnp.concatenate(ps, axis=0), vw)
        for h2 in range(B_GROUP // 2):
            oa = o_all[(2 * h2) * SWA_BLOCK:(2 * h2 + 1) * SWA_BLOCK]
            ob = o_all[(2 * h2 + 1) * SWA_BLOCK:(2 * h2 + 2) * SWA_BLOCK]
            o_ref[n * SWA_BLOCK:(n + 1) * SWA_BLOCK, h2 * LANES:(h2 + 1) * LANES] = (
                jnp.where(lane < HEAD_DIM, oa, ob).astype(o_ref.dtype))


def _swa_call(proj, slopes, sinks, *, tq=512):
    B, S, _ = proj.shape
    kern = functools.partial(_swa_kernel, tq=tq)
    qw = B_GROUP * LANES
    return pl.pallas_call(
        kern,
        grid=(B, B_KV_HEADS, S // tq),
        in_specs=[
            pl.BlockSpec(memory_space=pltpu.SMEM),
            pl.BlockSpec(memory_space=pltpu.SMEM),
            pl.BlockSpec((None, tq, qw), lambda b, kv, i: (b, i, P_QB // B_GROUP + kv)),
            pl.BlockSpec((None, S, LANES), lambda b, kv, i: (b, 0, P_KB)),
            pl.BlockSpec((None, S, LANES), lambda b, kv, i: (b, 0, P_VB + kv)),
        ],
        out_specs=pl.BlockSpec((None, tq, 2 * LANES), lambda b, kv, i: (b, i, kv)),
        out_shape=jax.ShapeDtypeStruct((B, S, BRANCH_WIDTH), BF16),
        compiler_params=_cparams(("parallel", "parallel", "arbitrary")),
        name="swa",
    )(slopes, sinks, proj, proj, proj)


def _post_kernel(x_ref, g_ref, fg_ref, ya_ref, yb_ref, yc_ref, wg_ref, wgm_ref, wua_ref, wub_ref, wuc_ref,
                 wo_ref, o_ref, *, final):
    x = x_ref[...]
    h = _rms(x, g_ref[...], RMS_EPS).astype(BF16)
    merged = None
    for br, (y_ref, wu_ref) in enumerate(((ya_ref, wua_ref), (yb_ref, wub_ref), (yc_ref, wuc_ref))):
        g = _dot(h, wg_ref[:, br * BRANCH_WIDTH:(br + 1) * BRANCH_WIDTH])
        t = (y_ref[...].astype(F32) * (g * jax.nn.sigmoid(g))).astype(BF16)
        u = _dot(t, wu_ref[...])
        gate = jax.nn.sigmoid(_dot(h, wgm_ref[:, br * D_MODEL:(br + 1) * D_MODEL]))
        merged = gate * u if merged is None else merged + gate * u
    out = x + _dot(merged.astype(BF16), wo_ref[...])
    if final:
        out = _rms(out, fg_ref[...], RMS_EPS)
    o_ref[...] = out


def _post_call(x, norm_gain, final_gain, ya, yb, yc, w_g, w_gm, w_ua, w_ub, w_uc, w_o, layer, final, *, tm=512):
    B, S, D = x.shape
    kern = functools.partial(_post_kernel, final=final)
    tok = lambda w: pl.BlockSpec((None, tm, w), lambda b, i: (b, i, 0))
    lay = lambda r, c: pl.BlockSpec((None, r, c), lambda b, i: (layer, 0, 0))
    return pl.pallas_call(
        kern,
        grid=(B, S // tm),
        in_specs=[
            tok(D), lay(1, D), pl.BlockSpec((1, D), lambda b, i: (0, 0)),
            tok(BRANCH_WIDTH), tok(BRANCH_WIDTH), tok(BRANCH_WIDTH),
            lay(D, N_BRANCH * BRANCH_WIDTH), lay(D, N_BRANCH * D),
            lay(BRANCH_WIDTH, D), lay(BRANCH_WIDTH, D), lay(BRANCH_WIDTH, D), lay(D, D),
        ],
        out_specs=tok(D),
        out_shape=jax.ShapeDtypeStruct((B, S, D), F32),
        compiler_params=_cparams(("parallel", "parallel")),
        name="post",
    )(x, norm_gain, final_gain, ya, yb, yc, w_g, w_gm, w_ua, w_ub, w_uc, w_o)


def _alibi_slopes(n_heads):
    return 2.0 ** (-8.0 * jnp.arange(1, n_heads + 1, dtype=F32) / n_heads)


def _prepare_weights(w_in):
    scale = HEAD_DIM ** -0.5
    scale2 = scale * LOG2E
    sl = lambda o, n: w_in[:, :, o:o + n]
    zeros64 = jnp.zeros(w_in.shape[:2] + (HEAD_DIM,), w_in.dtype)
    qb = sl(O_QB, 512) * scale
    qb_blocks = []
    for n in range(B_HEADS):
        qn = qb[:, :, n * HEAD_DIM:(n + 1) * HEAD_DIM]
        qb_blocks += [qn, zeros64] if n // B_GROUP == 0 else [zeros64, qn]
    vb = sl(O_VB, 128)
    v0, v1 = vb[:, :, :HEAD_DIM], vb[:, :, HEAD_DIM:]
    w_pre = jnp.concatenate(
        [sl(O_QA, 512) * scale2, sl(O_KA, 512), sl(O_VA, 512),
         sl(O_QC, 512) * scale2, sl(O_KC, 512), sl(O_VC, 512)]
        + qb_blocks + [sl(O_KB, 128), v0, v0, v1, v1], axis=-1).astype(BF16)
    wf = jnp.pad(sl(O_FC, C_HEADS), ((0, 0), (0, 0), (0, LANES - C_HEADS))).astype(BF16)
    w_g = jnp.concatenate([sl(O_GA, 512), sl(O_GB, 512), sl(O_GC, 512)], axis=-1).astype(BF16)
    w_gm = sl(O_GM, N_BRANCH * D_MODEL).astype(BF16)
    return w_pre, wf, w_g, w_gm


def _placement():
    place = np.zeros((N_PIECES, LANES, N_PAIRS * LANES), np.float32)
    for x in range(N_PIECES):
        for h in range(C_HEADS):
            place[x, h, (h // 2) * LANES + N_PIECES * (h % 2) + x] = 1.0
    return jnp.asarray(place, BF16)


def _alibi_aug(seq):
    slopes = 2.0 ** (-8.0 * np.arange(1, A_HEADS + 1, dtype=np.float64) / A_HEADS)
    rest = (slopes[:, None] * (LOG2E * np.arange(seq, dtype=np.float64))[None, :]).astype(np.float32)
    table = np.zeros((A_HEADS, seq, LANES), np.float32)
    for x in range(N_PIECES):
        piece = rest.astype(BF16).astype(np.float32)
        table[:, :, x] = piece
        rest = rest - piece
    return jnp.asarray(table, BF16)


def kernel(x, norm_gain, w_in, b_forget, lambda_q1, lambda_k1, lambda_q2, lambda_k2, subln_gain, sinks,
           w_up_a, w_up_b, w_up_c, w_o, final_gain):
    B, S, D = x.shape
    depth = w_in.shape[0]
    w_pre, wf, w_g, w_gm = _prepare_weights(w_in)
    w_ua, w_ub, w_uc, w_ob = (w.astype(BF16) for w in (w_up_a, w_up_b, w_up_c, w_o))
    gain3 = norm_gain.reshape(depth, 1, D)
    fgain = final_gain.reshape(1, D)
    bf3 = jnp.pad(b_forget.astype(F32), ((0, 0), (0, LANES - C_HEADS))).reshape(depth, 1, LANES)
    place = _placement()
    alibi_aug = _alibi_aug(S)
    slopes_b = _alibi_slopes(B_HEADS)
    pad = lambda v: jnp.pad(v.astype(F32), (0, LANES - v.shape[0]))

    for l in range(depth):
        lam_init = 0.8 - 0.6 * math.exp(-0.3 * l)
        proj = _pre_call(x, gain3, w_pre, wf, bf3, place, l)
        consts = jnp.array([lam_init, 1.0 - lam_init], F32)
        lam_rows = jnp.zeros((8, LANES), F32).at[0:4].set(
            jnp.stack([pad(lambda_q1[l]), pad(lambda_k1[l]), pad(lambda_q2[l]), pad(lambda_k2[l])]))
        ya = _flash_call("diff", proj, P_QA, P_KA, P_VA,
                         (consts, alibi_aug, lam_rows, subln_gain[l].astype(F32).reshape(1, LANES)))
        yc = _flash_call("fox", proj, P_QC, P_KC, P_VC)
        yb = _swa_call(proj, slopes_b, sinks[l].astype(F32))
        x = _post_call(x, gain3, fgain, ya, yb, yc, w_g, w_gm, w_ua, w_ub, w_uc, w_ob, l, l == depth - 1)
    return x
```

```python
import functools
import math

import numpy as np
import jax
import jax.numpy as jnp
from jax import lax
from jax.experimental import pallas as pl
from jax.experimental.pallas import tpu as pltpu

F32 = jnp.float32
BF16 = jnp.bfloat16

D_MODEL = 1024
HEAD_DIM = 64
LANES = 128
RMS_EPS = 1e-6
SUBLN_EPS = 1e-5
A_HEADS = 4
B_HEADS = 8
B_KV_HEADS = 2
B_GROUP = B_HEADS // B_KV_HEADS
C_HEADS = 8
WINDOW = 128
SWA_BLOCK = 128
N_BRANCH = 3
BRANCH_WIDTH = 512
N_PAIRS = 4
N_PIECES = 3
LOG2E = math.log2(math.e)

_SPLITS = (512, 512, 512, 512, 512, 128, 128, 512, 512, 512, 512, 8, 512, 3 * D_MODEL)
_OFF = [0]
for _s in _SPLITS:
    _OFF.append(_OFF[-1] + _s)
(O_QA, O_KA, O_VA, O_GA, O_QB, O_KB, O_VB, O_GB, O_QC, O_KC, O_VC, O_FC, O_GC, O_GM, _D_IN) = _OFF

P_QA, P_KA, P_VA = 0, 4, 8
P_QC, P_KC, P_VC = 12, 16, 20
P_QB = 24
P_KB = 32
P_VB = 33
P_MM = 35
P_AUG = 35
P_BLOCKS = 39
P_WIDTH = P_BLOCKS * LANES
MM_WIDTH = P_MM * LANES

VMEM_LIMIT = 56 * 1024 * 1024


def _cparams(sem):
    return pltpu.CompilerParams(dimension_semantics=sem, vmem_limit_bytes=VMEM_LIMIT)


def _rms(x, gain, eps):
    ms = jnp.mean(x * x, axis=-1, keepdims=True)
    return x * lax.rsqrt(ms + eps) * gain


def _dot(a, b):
    return jnp.dot(a, b, preferred_element_type=F32)


def _dot_nt(a, b):
    return lax.dot_general(a, b, (((1,), (1,)), ((), ())), preferred_element_type=F32)


def _split3(v):
    hi = v.astype(BF16)
    r = v - hi.astype(F32)
    mid = r.astype(BF16)
    lo = (r - mid.astype(F32)).astype(BF16)
    return hi, mid, lo


def _pre_kernel(x_ref, g_ref, w_ref, wf_ref, bf_ref, place_ref, proj_ref, carry_ref, *, tm, chunk):
    i = pl.program_id(1)

    @pl.when(i == 0)
    def _():
        carry_ref[...] = jnp.zeros_like(carry_ref)

    h = _rms(x_ref[...], g_ref[...], RMS_EPS).astype(BF16)

    lane = lax.broadcasted_iota(jnp.int32, (tm, LANES), 1)

    def pieces_by_group(v):
        p0, p1, p2 = (p.astype(F32) for p in _split3(v))
        by_group = jnp.where(lane < C_HEADS, p0, jnp.where(lane < 2 * C_HEADS, p1, p2))
        return jnp.where(lane < N_PIECES * C_HEADS, by_group, 0.0).astype(BF16)

    z = _dot(h, wf_ref[...]) + bf_ref[...]
    logf = jnp.minimum(z, 0.0) - jnp.log1p(jnp.exp(-jnp.abs(z)))
    row = lax.broadcasted_iota(jnp.int32, (tm, tm), 0)
    col = lax.broadcasted_iota(jnp.int32, (tm, tm), 1)
    lower = (col <= row).astype(BF16)
    part = _dot(lower, pieces_by_group(logf))
    total = part + pltpu.roll(part, LANES - C_HEADS, 1) + pltpu.roll(part, LANES - 2 * C_HEADS, 1)
    total = jnp.where(lane < C_HEADS, total, 0.0)
    c = total + pltpu.roll(total, C_HEADS, 1) + pltpu.roll(total, 2 * C_HEADS, 1) + carry_ref[0:1, :]
    carry_ref[...] = jnp.broadcast_to(c[tm - 1:tm, :], carry_ref.shape)
    proj_ref[:, MM_WIDTH:P_WIDTH] = _dot(pieces_by_group(c * (-LOG2E)), place_ref[...]).astype(BF16)

    for c0 in range(0, MM_WIDTH, chunk):
        c1 = min(c0 + chunk, MM_WIDTH)
        proj_ref[:, c0:c1] = _dot(h, w_ref[:, c0:c1]).astype(BF16)


def _pre_call(x, norm_gain, w_pre, wf, b_f, place, layer, *, tm=512, chunk=512):
    B, S, D = x.shape
    kern = functools.partial(_pre_kernel, tm=tm, chunk=chunk)
    return pl.pallas_call(
        kern,
        grid=(B, S // tm),
        in_specs=[
            pl.BlockSpec((None, tm, D), lambda b, i: (b, i, 0)),
            pl.BlockSpec((None, 1, D), lambda b, i: (layer, 0, 0)),
            pl.BlockSpec((None, D, MM_WIDTH), lambda b, i: (layer, 0, 0)),
            pl.BlockSpec((None, D, LANES), lambda b, i: (layer, 0, 0)),
            pl.BlockSpec((None, 1, LANES), lambda b, i: (layer, 0, 0)),
            pl.BlockSpec((LANES, N_PAIRS * LANES), lambda b, i: (0, 0)),
        ],
        out_specs=pl.BlockSpec((None, tm, P_WIDTH), lambda b, i: (b, i, 0)),
        out_shape=jax.ShapeDtypeStruct((B, S, P_WIDTH), BF16),
        scratch_shapes=[pltpu.VMEM((8, LANES), F32)],
        compiler_params=_cparams(("parallel", "arbitrary")),
        name="pre",
    )(x, norm_gain, w_pre, wf, b_f, place)


def _chunk_order(nq):
    order = [(t, t) for t in range(nq)]
    order += [(qi, c) for qi in range(1, nq) for c in range(qi)]
    return order


def _flash_kernel(*refs, mode, tq, nq, unroll):
    if mode == "diff":
        (consts_ref, tab_ref, q_ref, k_ref, v_ref, aug_ref, lam_ref, sg_ref, o_ref,
         qe_ref, m_ref, acc_ref, sa_ref, sb_ref) = refs
    else:
        (tab_ref, q_ref, k_ref, v_ref, aug_ref, o_ref, qe_ref, m_ref, acc_ref, sa_ref, sb_ref) = refs
    tk = tq
    rows = 2 * tq
    n_chunks = nq * (nq + 1) // 2
    lane = lax.broadcasted_iota(jnp.int32, (tq, LANES), 1)

    def build_qe(t, carry):
        q = q_ref[pl.ds(pl.multiple_of(t * tq, tq), tq), :]
        zero = jnp.zeros_like(q)
        for m in range(2):
            in_map = (lane >= HEAD_DIM) if m else (lane < HEAD_DIM)
            first = N_PIECES * m if mode == "fox" else 0
            ones_at_pieces = jnp.where(lane >= first, 1.0, 0.0) * jnp.where(lane < first + N_PIECES, 1.0, 0.0)
            qe_ref[t, m * tq:(m + 1) * tq, 0:LANES] = jnp.where(in_map, q, zero)
            qe_ref[t, m * tq:(m + 1) * tq, LANES:2 * LANES] = ones_at_pieces.astype(BF16)
        return carry

    lax.fori_loop(0, nq, build_qe, 0)

    def qk(t):
        qi = tab_ref[0, t]
        start = pl.multiple_of(tab_ref[1, t] * tk, tk)
        kc = jnp.concatenate([k_ref[pl.ds(start, tk), :], aug_ref[pl.ds(start, tk), :]], axis=1)
        return _dot_nt(qe_ref[qi], kc)

    def softmax_pv(s_ref, t, diagonal):
        qi = tab_ref[0, t]
        start = pl.multiple_of(tab_ref[1, t] * tk, tk)
        vc = jnp.concatenate([v_ref[pl.ds(start, tk), :], jnp.ones((tk, LANES), BF16)], axis=1)
        s = s_ref[...]
        if diagonal:
            r = lax.broadcasted_iota(jnp.int32, (rows, tk), 0)
            c = lax.broadcasted_iota(jnp.int32, (rows, tk), 1)
            s = jnp.where(c <= jnp.where(r >= tq, r - tq, r), s, -jnp.inf)
        cols = [s[:, c0:c0 + LANES] for c0 in range(0, tk, LANES)]
        mx = cols[0]
        for sc in cols[1:]:
            mx = jnp.maximum(mx, sc)
        m_cur = jnp.max(mx, axis=1, keepdims=True)
        if diagonal:
            m_new = jnp.broadcast_to(m_cur, (rows, LANES))
        else:
            m_prev = m_ref[qi]
            m_new = jnp.maximum(m_prev, m_cur)
            alpha = jnp.exp2(m_prev - m_new)
        p = jnp.concatenate([jnp.exp2(sc - m_new).astype(BF16) for sc in cols], axis=1)
        pv = _dot(p, vc)
        if diagonal:
            acc_ref[qi] = pv
        else:
            acc_ref[qi, :, 0:LANES] = alpha * acc_ref[qi, :, 0:LANES] + pv[:, 0:LANES]
            acc_ref[qi, :, LANES:2 * LANES] = alpha * acc_ref[qi, :, LANES:2 * LANES] + pv[:, LANES:2 * LANES]
        m_ref[qi] = m_new

    s_refs = (sa_ref, sb_ref)

    def run(first, last, diagonal, unroll):
        def body(i, carry):
            for u in range(unroll):
                t = first + i * unroll + u
                s_refs[(u + 1) % 2][...] = qk(t + 1)
                softmax_pv(s_refs[u % 2], t, diagonal)
            return carry
        lax.fori_loop(0, (last - first) // unroll, body, 0)

    def run_all(first, last, diagonal):
        main = first + (last - first) // unroll * unroll
        if main > first:
            run(first, main, diagonal, unroll)
        if last > main:
            run(main, last, diagonal, 2)

    sa_ref[...] = qk(0)
    run_all(0, nq, True)
    run_all(nq, n_chunks, False)

    def finalize(t, carry):
        o0 = acc_ref[t, 0:tq, 0:LANES] / acc_ref[t, 0:tq, LANES:2 * LANES]
        o1 = acc_ref[t, tq:rows, 0:LANES] / acc_ref[t, tq:rows, LANES:2 * LANES]
        if mode == "diff":
            lam_init = consts_ref[0]
            lp = lam_ref[...]
            lam = (jnp.exp(jnp.sum(lp[0:1] * lp[1:2], axis=1, keepdims=True))
                   - jnp.exp(jnp.sum(lp[2:3] * lp[3:4], axis=1, keepdims=True)) + lam_init)
            y = _rms(o0 - lam * o1, sg_ref[...], SUBLN_EPS) * consts_ref[1]
        else:
            y = jnp.where(lane < HEAD_DIM, o0, o1)
        o_ref[pl.ds(pl.multiple_of(t * tq, tq), tq), :] = y.astype(o_ref.dtype)
        return carry

    lax.fori_loop(0, nq, finalize, 0)


def _flash_call(mode, proj, q_blk, k_blk, v_blk, extra=(), *, tq=512, unroll=4):
    B, S, _ = proj.shape
    nq = S // tq
    order = _chunk_order(nq)
    assert nq % 2 == 0 and len(order) % 2 == 0, "the chunk loop handles two chunks per trip"
    table = jnp.asarray(np.array(order + [order[-1]], np.int32).T)
    kern = functools.partial(_flash_kernel, mode=mode, tq=tq, nq=nq, unroll=unroll)
    seq = lambda blk: pl.BlockSpec((None, S, LANES), lambda b, p: (b, 0, blk + p))
    smem = pl.BlockSpec(memory_space=pltpu.SMEM)
    in_specs = [smem, seq(q_blk), seq(k_blk), seq(v_blk)]
    if mode == "diff":
        consts, alibi_aug, lam_rows, subln = extra
        in_specs = [smem] + in_specs + [
            pl.BlockSpec((None, S, LANES), lambda b, p: (p, 0, 0)),
            pl.BlockSpec((8, LANES), lambda b, p: (0, 0)),
            pl.BlockSpec((1, LANES), lambda b, p: (0, 0)),
        ]
        args = (consts, table, proj, proj, proj, alibi_aug, lam_rows, subln)
    else:
        in_specs = in_specs + [seq(P_AUG)]
        args = (table, proj, proj, proj, proj)
    return pl.pallas_call(
        kern,
        grid=(B, N_PAIRS),
        in_specs=in_specs,
        out_specs=pl.BlockSpec((None, S, LANES), lambda b, p: (b, 0, p)),
        out_shape=jax.ShapeDtypeStruct((B, S, BRANCH_WIDTH), BF16),
        scratch_shapes=[
            pltpu.VMEM((nq, 2 * tq, 2 * LANES), BF16),
            pltpu.VMEM((nq, 2 * tq, LANES), F32),
            pltpu.VMEM((nq, 2 * tq, 2 * LANES), F32),
            pltpu.VMEM((2 * tq, tq), F32),
            pltpu.VMEM((2 * tq, tq), F32),
        ],
        compiler_params=_cparams(("parallel", "parallel")),
        name="flash_" + mode,
    )(*args)


def _swa_kernel(slopes_ref, sinks_ref, q_ref, k_ref, v_ref, pos_ref, o_ref, e_ref, *, tq):
    kv = pl.program_id(1)
    qi = pl.program_id(2)
    nb = tq // SWA_BLOCK
    lane = lax.broadcasted_iota(jnp.int32, (SWA_BLOCK, LANES), 1)
    row = lax.broadcasted_iota(jnp.int32, (SWA_BLOCK, LANES), 0)
    r = lax.broadcasted_iota(jnp.int32, (SWA_BLOCK, 2 * SWA_BLOCK), 0)
    c = lax.broadcasted_iota(jnp.int32, (SWA_BLOCK, 2 * SWA_BLOCK), 1)
    for g in range(B_GROUP):
        first = N_PIECES * (kv * B_GROUP + g)
        ones_at_pieces = jnp.where(lane >= first, 1.0, 0.0) * jnp.where(lane < first + N_PIECES, 1.0, 0.0)
        e_ref[g * SWA_BLOCK:(g + 1) * SWA_BLOCK, :] = ones_at_pieces.astype(BF16)
    ones = jnp.ones((2 * SWA_BLOCK, LANES), BF16)
    for n in range(nb):
        q_start = qi * tq + n * SWA_BLOCK
        k_start = pl.multiple_of(jnp.maximum(q_start - SWA_BLOCK, 0), SWA_BLOCK)
        kw = jnp.concatenate([k_ref[pl.ds(k_start, 2 * SWA_BLOCK), :],
                              pos_ref[pl.ds(k_start, 2 * SWA_BLOCK), :]], axis=1)
        vw = jnp.concatenate([v_ref[pl.ds(k_start, 2 * SWA_BLOCK), :], ones], axis=1)
        qs = jnp.concatenate(
            [q_ref[n * SWA_BLOCK:(n + 1) * SWA_BLOCK, g * LANES:(g + 1) * LANES] for g in range(B_GROUP)],
            axis=0)
        s_all = _dot_nt(jnp.concatenate([qs, e_ref[...]], axis=1), kw)
        dist = (q_start - k_start) + r - c
        valid = (dist >= 0) & (dist < WINDOW)
        qpos = (q_start + row).astype(F32)
        ps, tails = [], []
        for g in range(B_GROUP):
            h = kv * B_GROUP + g
            sink = sinks_ref[h] + slopes_ref[h] * qpos
            s = jnp.where(valid, s_all[g * SWA_BLOCK:(g + 1) * SWA_BLOCK], -jnp.inf)
            s0, s1 = s[:, 0:LANES], s[:, LANES:2 * LANES]
            mx = jnp.maximum(jnp.max(jnp.maximum(s0, s1), axis=1, keepdims=True), sink)
            ps.append(jnp.concatenate([jnp.exp2(s0 - mx).astype(BF16), jnp.exp2(s1 - mx).astype(BF16)], axis=1))
            tails.append(jnp.exp2(sink - mx))
        pv = _dot(jnp.concatenate(ps, axis=0), vw)
        outs = []
        for g in range(B_GROUP):
            blk = pv[g * SWA_BLOCK:(g + 1) * SWA_BLOCK]
            outs.append(blk[:, 0:LANES] / (blk[:, LANES:2 * LANES] + tails[g]))
        for h2 in range(B_GROUP // 2):
            o_ref[n * SWA_BLOCK:(n + 1) * SWA_BLOCK, h2 * LANES:(h2 + 1) * LANES] = (
                jnp.where(lane < HEAD_DIM, outs[2 * h2], outs[2 * h2 + 1]).astype(o_ref.dtype))


def _swa_call(proj, slopes2, sinks2, pos_aug, *, tq=512):
    B, S, _ = proj.shape
    kern = functools.partial(_swa_kernel, tq=tq)
    qw = B_GROUP * LANES
    return pl.pallas_call(
        kern,
        grid=(B, B_KV_HEADS, S // tq),
        in_specs=[
            pl.BlockSpec(memory_space=pltpu.SMEM),
            pl.BlockSpec(memory_space=pltpu.SMEM),
            pl.BlockSpec((None, tq, qw), lambda b, kv, i: (b, i, P_QB // B_GROUP + kv)),
            pl.BlockSpec((None, S, LANES), lambda b, kv, i: (b, 0, P_KB)),
            pl.BlockSpec((None, S, LANES), lambda b, kv, i: (b, 0, P_VB + kv)),
            pl.BlockSpec((S, LANES), lambda b, kv, i: (0, 0)),
        ],
        out_specs=pl.BlockSpec((None, tq, 2 * LANES), lambda b, kv, i: (b, i, kv)),
        out_shape=jax.ShapeDtypeStruct((B, S, BRANCH_WIDTH), BF16),
        scratch_shapes=[pltpu.VMEM((B_GROUP * SWA_BLOCK, LANES), BF16)],
        compiler_params=_cparams(("parallel", "parallel", "arbitrary")),
        name="swa",
    )(slopes2, sinks2, proj, proj, proj, pos_aug)


def _post_kernel(x_ref, g_ref, fg_ref, ya_ref, yb_ref, yc_ref, wg_ref, wgm_ref, wua_ref, wub_ref, wuc_ref,
                 wo_ref, o_ref, *, final):
    x = x_ref[...]
    h = _rms(x, g_ref[...], RMS_EPS).astype(BF16)
    merged = None
    for br, (y_ref, wu_ref) in enumerate(((ya_ref, wua_ref), (yb_ref, wub_ref), (yc_ref, wuc_ref))):
        g = _dot(h, wg_ref[:, br * BRANCH_WIDTH:(br + 1) * BRANCH_WIDTH])
        t = (y_ref[...].astype(F32) * (g * jax.nn.sigmoid(g))).astype(BF16)
        u = _dot(t, wu_ref[...])
        gate = jax.nn.sigmoid(_dot(h, wgm_ref[:, br * D_MODEL:(br + 1) * D_MODEL]))
        merged = gate * u if merged is None else merged + gate * u
    out = x + _dot(merged.astype(BF16), wo_ref[...])
    if final:
        out = _rms(out, fg_ref[...], RMS_EPS)
    o_ref[...] = out


def _post_call(x, norm_gain, final_gain, ya, yb, yc, w_g, w_gm, w_ua, w_ub, w_uc, w_o, layer, final, *, tm=512):
    B, S, D = x.shape
    kern = functools.partial(_post_kernel, final=final)
    tok = lambda w: pl.BlockSpec((None, tm, w), lambda b, i: (b, i, 0))
    lay = lambda r, c: pl.BlockSpec((None, r, c), lambda b, i: (layer, 0, 0))
    return pl.pallas_call(
        kern,
        grid=(B, S // tm),
        in_specs=[
            tok(D), lay(1, D), pl.BlockSpec((1, D), lambda b, i: (0, 0)),
            tok(BRANCH_WIDTH), tok(BRANCH_WIDTH), tok(BRANCH_WIDTH),
            lay(D, N_BRANCH * BRANCH_WIDTH), lay(D, N_BRANCH * D),
            lay(BRANCH_WIDTH, D), lay(BRANCH_WIDTH, D), lay(BRANCH_WIDTH, D), lay(D, D),
        ],
        out_specs=tok(D),
        out_shape=jax.ShapeDtypeStruct((B, S, D), F32),
        compiler_params=_cparams(("parallel", "parallel")),
        name="post",
    )(x, norm_gain, final_gain, ya, yb, yc, w_g, w_gm, w_ua, w_ub, w_uc, w_o)


def _alibi_slopes(n_heads):
    return 2.0 ** (-8.0 * jnp.arange(1, n_heads + 1, dtype=F32) / n_heads)


def _prepare_weights(w_in):
    scale = HEAD_DIM ** -0.5
    scale2 = scale * LOG2E
    sl = lambda o, n: w_in[:, :, o:o + n]
    zeros64 = jnp.zeros(w_in.shape[:2] + (HEAD_DIM,), w_in.dtype)
    qb = sl(O_QB, 512) * scale2
    qb_blocks = []
    for n in range(B_HEADS):
        qn = qb[:, :, n * HEAD_DIM:(n + 1) * HEAD_DIM]
        qb_blocks += [qn, zeros64] if n // B_GROUP == 0 else [zeros64, qn]
    vb = sl(O_VB, 128)
    v0, v1 = vb[:, :, :HEAD_DIM], vb[:, :, HEAD_DIM:]
    w_pre = jnp.concatenate(
        [sl(O_QA, 512) * scale2, sl(O_KA, 512), sl(O_VA, 512),
         sl(O_QC, 512) * scale2, sl(O_KC, 512), sl(O_VC, 512)]
        + qb_blocks + [sl(O_KB, 128), v0, v0, v1, v1], axis=-1).astype(BF16)
    wf = jnp.pad(jnp.tile(sl(O_FC, C_HEADS), (1, 1, N_PIECES)),
                 ((0, 0), (0, 0), (0, LANES - N_PIECES * C_HEADS))).astype(BF16)
    w_g = jnp.concatenate([sl(O_GA, 512), sl(O_GB, 512), sl(O_GC, 512)], axis=-1).astype(BF16)
    w_gm = sl(O_GM, N_BRANCH * D_MODEL).astype(BF16)
    return w_pre, wf, w_g, w_gm


def _placement():
    place = np.zeros((LANES, N_PAIRS * LANES), np.float32)
    for x in range(N_PIECES):
        for h in range(C_HEADS):
            place[C_HEADS * x + h, (h // 2) * LANES + N_PIECES * (h % 2) + x] = 1.0
    return jnp.asarray(place, BF16)


def _alibi_pieces(n_heads, seq):
    slopes = 2.0 ** (-8.0 * np.arange(1, n_heads + 1, dtype=np.float64) / n_heads)
    rest = (slopes[:, None] * (LOG2E * np.arange(seq, dtype=np.float64))[None, :]).astype(np.float32)
    pieces = np.zeros((n_heads, seq, N_PIECES), np.float32)
    for x in range(N_PIECES):
        pieces[:, :, x] = rest.astype(BF16).astype(np.float32)
        rest = rest - pieces[:, :, x]
    return pieces


def _alibi_aug(seq):
    table = np.zeros((A_HEADS, seq, LANES), np.float32)
    table[:, :, 0:N_PIECES] = _alibi_pieces(A_HEADS, seq)
    return jnp.asarray(table, BF16)


def _swa_pos_aug(seq):
    table = np.zeros((seq, LANES), np.float32)
    pieces = _alibi_pieces(B_HEADS, seq)
    for h in range(B_HEADS):
        table[:, N_PIECES * h:N_PIECES * (h + 1)] = pieces[h]
    return jnp.asarray(table, BF16)


def kernel(x, norm_gain, w_in, b_forget, lambda_q1, lambda_k1, lambda_q2, lambda_k2, subln_gain, sinks,
           w_up_a, w_up_b, w_up_c, w_o, final_gain):
    B, S, D = x.shape
    depth = w_in.shape[0]
    w_pre, wf, w_g, w_gm = _prepare_weights(w_in)
    w_ua, w_ub, w_uc, w_ob = (w.astype(BF16) for w in (w_up_a, w_up_b, w_up_c, w_o))
    gain3 = norm_gain.reshape(depth, 1, D)
    fgain = final_gain.reshape(1, D)
    bf3 = jnp.pad(jnp.tile(b_forget.astype(F32), (1, N_PIECES)),
                  ((0, 0), (0, LANES - N_PIECES * C_HEADS))).reshape(depth, 1, LANES)
    place = _placement()
    alibi_aug = _alibi_aug(S)
    slopes2_b = _alibi_slopes(B_HEADS) * LOG2E
    pos_aug = _swa_pos_aug(S)
    pad = lambda v: jnp.pad(v.astype(F32), (0, LANES - v.shape[0]))

    for l in range(depth):
        lam_init = 0.8 - 0.6 * math.exp(-0.3 * l)
        proj = _pre_call(x, gain3, w_pre, wf, bf3, place, l)
        consts = jnp.array([lam_init, 1.0 - lam_init], F32)
        lam_rows = jnp.zeros((8, LANES), F32).at[0:4].set(
            jnp.stack([pad(lambda_q1[l]), pad(lambda_k1[l]), pad(lambda_q2[l]), pad(lambda_k2[l])]))
        ya = _flash_call("diff", proj, P_QA, P_KA, P_VA,
                         (consts, alibi_aug, lam_rows, subln_gain[l].astype(F32).reshape(1, LANES)))
        yc = _flash_call("fox", proj, P_QC, P_KC, P_VC)
        yb = _swa_call(proj, slopes2_b, sinks[l].astype(F32) * LOG2E, pos_aug)
        x = _post_call(x, gain3, fgain, ya, yb, yc, w_g, w_gm, w_ua, w_ub, w_uc, w_ob, l, l == depth - 1)
    return x
```

```python
import functools
import math

import numpy as np
import jax
import jax.numpy as jnp
from jax import lax
from jax.experimental import pallas as pl
from jax.experimental.pallas import tpu as pltpu

F32 = jnp.float32
BF16 = jnp.bfloat16

D_MODEL = 1024
HEAD_DIM = 64
LANES = 128
RMS_EPS = 1e-6
SUBLN_EPS = 1e-5
A_HEADS = 4
B_HEADS = 8
B_KV_HEADS = 2
B_GROUP = B_HEADS // B_KV_HEADS
C_HEADS = 8
WINDOW = 128
SWA_BLOCK = 128
N_BRANCH = 3
BRANCH_WIDTH = 512
N_PAIRS = 4
N_PIECES = 3
LOG2E = math.log2(math.e)

_SPLITS = (512, 512, 512, 512, 512, 128, 128, 512, 512, 512, 512, 8, 512, 3 * D_MODEL)
_OFF = [0]
for _s in _SPLITS:
    _OFF.append(_OFF[-1] + _s)
(O_QA, O_KA, O_VA, O_GA, O_QB, O_KB, O_VB, O_GB, O_QC, O_KC, O_VC, O_FC, O_GC, O_GM, _D_IN) = _OFF

P_QA, P_KA, P_VA = 0, 4, 8
P_QC, P_KC, P_VC = 12, 16, 20
P_QB = 24
P_KB = 32
P_VB = 33
P_MM = 35
P_AUG = 35
P_BLOCKS = 39
P_WIDTH = P_BLOCKS * LANES
MM_WIDTH = P_MM * LANES

VMEM_LIMIT = 56 * 1024 * 1024


def _cparams(sem):
    return pltpu.CompilerParams(dimension_semantics=sem, vmem_limit_bytes=VMEM_LIMIT)


def _rms(x, gain, eps):
    ms = jnp.mean(x * x, axis=-1, keepdims=True)
    return x * lax.rsqrt(ms + eps) * gain


def _dot(a, b):
    return jnp.dot(a, b, preferred_element_type=F32)


def _dot_nt(a, b):
    return lax.dot_general(a, b, (((1,), (1,)), ((), ())), preferred_element_type=F32)


def _split3(v):
    hi = v.astype(BF16)
    r = v - hi.astype(F32)
    mid = r.astype(BF16)
    lo = (r - mid.astype(F32)).astype(BF16)
    return hi, mid, lo


def _pre_kernel(x_ref, g_ref, w_ref, wf_ref, bf_ref, place_ref, proj_ref, carry_ref, *, tm, chunk):
    i = pl.program_id(1)

    @pl.when(i == 0)
    def _():
        carry_ref[...] = jnp.zeros_like(carry_ref)

    h = _rms(x_ref[...], g_ref[...], RMS_EPS).astype(BF16)

    lane = lax.broadcasted_iota(jnp.int32, (tm, LANES), 1)

    def pieces_by_group(v):
        p0, p1, p2 = (p.astype(F32) for p in _split3(v))
        by_group = jnp.where(lane < C_HEADS, p0, jnp.where(lane < 2 * C_HEADS, p1, p2))
        return jnp.where(lane < N_PIECES * C_HEADS, by_group, 0.0).astype(BF16)

    z = _dot(h, wf_ref[...]) + bf_ref[...]
    logf = jnp.minimum(z, 0.0) - jnp.log1p(jnp.exp(-jnp.abs(z)))
    row = lax.broadcasted_iota(jnp.int32, (tm, tm), 0)
    col = lax.broadcasted_iota(jnp.int32, (tm, tm), 1)
    lower = (col <= row).astype(BF16)
    part = _dot(lower, pieces_by_group(logf))
    total = part + pltpu.roll(part, LANES - C_HEADS, 1) + pltpu.roll(part, LANES - 2 * C_HEADS, 1)
    total = jnp.where(lane < C_HEADS, total, 0.0)
    c = total + pltpu.roll(total, C_HEADS, 1) + pltpu.roll(total, 2 * C_HEADS, 1) + carry_ref[0:1, :]
    carry_ref[...] = jnp.broadcast_to(c[tm - 1:tm, :], carry_ref.shape)
    proj_ref[:, MM_WIDTH:P_WIDTH] = _dot(pieces_by_group(c * (-LOG2E)), place_ref[...]).astype(BF16)

    for c0 in range(0, MM_WIDTH, chunk):
        c1 = min(c0 + chunk, MM_WIDTH)
        proj_ref[:, c0:c1] = _dot(h, w_ref[:, c0:c1]).astype(BF16)


def _pre_call(x, norm_gain, w_pre, wf, b_f, place, layer, *, tm=512, chunk=512):
    B, S, D = x.shape
    kern = functools.partial(_pre_kernel, tm=tm, chunk=chunk)
    return pl.pallas_call(
        kern,
        grid=(B, S // tm),
        in_specs=[
            pl.BlockSpec((None, tm, D), lambda b, i: (b, i, 0)),
            pl.BlockSpec((None, 1, D), lambda b, i: (layer, 0, 0)),
            pl.BlockSpec((None, D, MM_WIDTH), lambda b, i: (layer, 0, 0)),
            pl.BlockSpec((None, D, LANES), lambda b, i: (layer, 0, 0)),
            pl.BlockSpec((None, 1, LANES), lambda b, i: (layer, 0, 0)),
            pl.BlockSpec((LANES, N_PAIRS * LANES), lambda b, i: (0, 0)),
        ],
        out_specs=pl.BlockSpec((None, tm, P_WIDTH), lambda b, i: (b, i, 0)),
        out_shape=jax.ShapeDtypeStruct((B, S, P_WIDTH), BF16),
        scratch_shapes=[pltpu.VMEM((8, LANES), F32)],
        compiler_params=_cparams(("parallel", "arbitrary")),
        name="pre",
    )(x, norm_gain, w_pre, wf, b_f, place)


def _chunk_order(nq):
    order = [(t, t) for t in range(nq)]
    order += [(qi, c) for qi in range(1, nq) for c in range(qi)]
    return order


def _flash_kernel(*refs, mode, tq, nq, unroll):
    if mode == "diff":
        (consts_ref, tab_ref, q_ref, k_ref, v_ref, aug_ref, lam_ref, sg_ref, o_ref,
         qe_ref, m_ref, acc_ref, sa_ref, sb_ref) = refs
    else:
        (tab_ref, q_ref, k_ref, v_ref, aug_ref, o_ref, qe_ref, m_ref, acc_ref, sa_ref, sb_ref) = refs
    tk = tq
    rows = 2 * tq
    n_chunks = nq * (nq + 1) // 2
    lane = lax.broadcasted_iota(jnp.int32, (tq, LANES), 1)

    def build_qe(t, carry):
        q = q_ref[pl.ds(pl.multiple_of(t * tq, tq), tq), :]
        zero = jnp.zeros_like(q)
        for m in range(2):
            in_map = (lane >= HEAD_DIM) if m else (lane < HEAD_DIM)
            first = N_PIECES * m if mode == "fox" else 0
            ones_at_pieces = jnp.where(lane >= first, 1.0, 0.0) * jnp.where(lane < first + N_PIECES, 1.0, 0.0)
            qe_ref[t, m * tq:(m + 1) * tq, 0:LANES] = jnp.where(in_map, q, zero)
            qe_ref[t, m * tq:(m + 1) * tq, LANES:2 * LANES] = ones_at_pieces.astype(BF16)
        return carry

    lax.fori_loop(0, nq, build_qe, 0)

    def qk(t):
        qi = tab_ref[0, t]
        start = pl.multiple_of(tab_ref[1, t] * tk, tk)
        kc = jnp.concatenate([k_ref[pl.ds(start, tk), :], aug_ref[pl.ds(start, tk), :]], axis=1)
        return _dot_nt(qe_ref[qi], kc)

    def softmax_pv(s_ref, t, diagonal):
        qi = tab_ref[0, t]
        start = pl.multiple_of(tab_ref[1, t] * tk, tk)
        vc = jnp.concatenate([v_ref[pl.ds(start, tk), :], jnp.ones((tk, LANES), BF16)], axis=1)
        s = s_ref[...]
        if diagonal:
            r = lax.broadcasted_iota(jnp.int32, (rows, tk), 0)
            c = lax.broadcasted_iota(jnp.int32, (rows, tk), 1)
            s = jnp.where(c <= jnp.where(r >= tq, r - tq, r), s, -jnp.inf)
        cols = [s[:, c0:c0 + LANES] for c0 in range(0, tk, LANES)]
        mx = cols[0]
        for sc in cols[1:]:
            mx = jnp.maximum(mx, sc)
        m_cur = jnp.max(mx, axis=1, keepdims=True)
        if diagonal:
            m_new = jnp.broadcast_to(m_cur, (rows, LANES))
        else:
            m_prev = m_ref[qi]
            m_new = jnp.maximum(m_prev, m_cur)
            alpha = jnp.exp2(m_prev - m_new)
        p = jnp.concatenate([jnp.exp2(sc - m_new).astype(BF16) for sc in cols], axis=1)
        pv = _dot(p, vc)
        if diagonal:
            acc_ref[qi] = pv
        else:
            acc_ref[qi, :, 0:LANES] = alpha * acc_ref[qi, :, 0:LANES] + pv[:, 0:LANES]
            acc_ref[qi, :, LANES:2 * LANES] = alpha * acc_ref[qi, :, LANES:2 * LANES] + pv[:, LANES:2 * LANES]
        m_ref[qi] = m_new

    s_refs = (sa_ref, sb_ref)

    def run(first, last, diagonal, unroll):
        def body(i, carry):
            for u in range(unroll):
                t = first + i * unroll + u
                s_refs[(u + 1) % 2][...] = qk(t + 1)
                softmax_pv(s_refs[u % 2], t, diagonal)
            return carry
        lax.fori_loop(0, (last - first) // unroll, body, 0)

    def run_all(first, last, diagonal):
        main = first + (last - first) // unroll * unroll
        if main > first:
            run(first, main, diagonal, unroll)
        if last > main:
            run(main, last, diagonal, last - main)

    sa_ref[...] = qk(0)
    run_all(0, nq, True)
    run_all(nq, n_chunks, False)

    def finalize(t, carry):
        o0 = acc_ref[t, 0:tq, 0:LANES] / acc_ref[t, 0:tq, LANES:2 * LANES]
        o1 = acc_ref[t, tq:rows, 0:LANES] / acc_ref[t, tq:rows, LANES:2 * LANES]
        if mode == "diff":
            lam_init = consts_ref[0]
            lp = lam_ref[...]
            lam = (jnp.exp(jnp.sum(lp[0:1] * lp[1:2], axis=1, keepdims=True))
                   - jnp.exp(jnp.sum(lp[2:3] * lp[3:4], axis=1, keepdims=True)) + lam_init)
            y = _rms(o0 - lam * o1, sg_ref[...], SUBLN_EPS) * consts_ref[1]
        else:
            y = jnp.where(lane < HEAD_DIM, o0, o1)
        o_ref[pl.ds(pl.multiple_of(t * tq, tq), tq), :] = y.astype(o_ref.dtype)
        return carry

    lax.fori_loop(0, nq, finalize, 0)


def _flash_call(mode, proj, q_blk, k_blk, v_blk, extra=(), *, tq=512, unroll=14):
    B, S, _ = proj.shape
    nq = S // tq
    order = _chunk_order(nq)
    assert nq % 2 == 0 and len(order) % 2 == 0, "the chunk loop handles two chunks per trip"
    table = jnp.asarray(np.array(order + [order[-1]], np.int32).T)
    kern = functools.partial(_flash_kernel, mode=mode, tq=tq, nq=nq, unroll=unroll)
    seq = lambda blk: pl.BlockSpec((None, S, LANES), lambda b, p: (b, 0, blk + p))
    smem = pl.BlockSpec(memory_space=pltpu.SMEM)
    in_specs = [smem, seq(q_blk), seq(k_blk), seq(v_blk)]
    if mode == "diff":
        consts, alibi_aug, lam_rows, subln = extra
        in_specs = [smem] + in_specs + [
            pl.BlockSpec((None, S, LANES), lambda b, p: (p, 0, 0)),
            pl.BlockSpec((8, LANES), lambda b, p: (0, 0)),
            pl.BlockSpec((1, LANES), lambda b, p: (0, 0)),
        ]
        args = (consts, table, proj, proj, proj, alibi_aug, lam_rows, subln)
    else:
        in_specs = in_specs + [seq(P_AUG)]
        args = (table, proj, proj, proj, proj)
    return pl.pallas_call(
        kern,
        grid=(B, N_PAIRS),
        in_specs=in_specs,
        out_specs=pl.BlockSpec((None, S, LANES), lambda b, p: (b, 0, p)),
        out_shape=jax.ShapeDtypeStruct((B, S, BRANCH_WIDTH), BF16),
        scratch_shapes=[
            pltpu.VMEM((nq, 2 * tq, 2 * LANES), BF16),
            pltpu.VMEM((nq, 2 * tq, LANES), F32),
            pltpu.VMEM((nq, 2 * tq, 2 * LANES), F32),
            pltpu.VMEM((2 * tq, tq), F32),
            pltpu.VMEM((2 * tq, tq), F32),
        ],
        compiler_params=_cparams(("parallel", "parallel")),
        name="flash_" + mode,
    )(*args)


def _swa_kernel(slopes_ref, sinks_ref, q_ref, k_ref, v_ref, pos_ref, o_ref, e_ref, *, tq):
    kv = pl.program_id(1)
    qi = pl.program_id(2)
    nb = tq // SWA_BLOCK
    lane = lax.broadcasted_iota(jnp.int32, (SWA_BLOCK, LANES), 1)
    row = lax.broadcasted_iota(jnp.int32, (SWA_BLOCK, LANES), 0)
    r = lax.broadcasted_iota(jnp.int32, (SWA_BLOCK, 2 * SWA_BLOCK), 0)
    c = lax.broadcasted_iota(jnp.int32, (SWA_BLOCK, 2 * SWA_BLOCK), 1)
    for g in range(B_GROUP):
        first = N_PIECES * (kv * B_GROUP + g)
        ones_at_pieces = jnp.where(lane >= first, 1.0, 0.0) * jnp.where(lane < first + N_PIECES, 1.0, 0.0)
        e_ref[g * SWA_BLOCK:(g + 1) * SWA_BLOCK, :] = ones_at_pieces.astype(BF16)
    ones = jnp.ones((2 * SWA_BLOCK, LANES), BF16)
    for n in range(nb):
        q_start = qi * tq + n * SWA_BLOCK
        k_start = pl.multiple_of(jnp.maximum(q_start - SWA_BLOCK, 0), SWA_BLOCK)
        kw = jnp.concatenate([k_ref[pl.ds(k_start, 2 * SWA_BLOCK), :],
                              pos_ref[pl.ds(k_start, 2 * SWA_BLOCK), :]], axis=1)
        vw = jnp.concatenate([v_ref[pl.ds(k_start, 2 * SWA_BLOCK), :], ones], axis=1)
        qs = jnp.concatenate(
            [q_ref[n * SWA_BLOCK:(n + 1) * SWA_BLOCK, g * LANES:(g + 1) * LANES] for g in range(B_GROUP)],
            axis=0)
        s_all = _dot_nt(jnp.concatenate([qs, e_ref[...]], axis=1), kw)
        dist = (q_start - k_start) + r - c
        valid = (dist >= 0) & (dist < WINDOW)
        qpos = (q_start + row).astype(F32)
        ps, tails = [], []
        for g in range(B_GROUP):
            h = kv * B_GROUP + g
            sink = sinks_ref[h] + slopes_ref[h] * qpos
            s = jnp.where(valid, s_all[g * SWA_BLOCK:(g + 1) * SWA_BLOCK], -jnp.inf)
            s0, s1 = s[:, 0:LANES], s[:, LANES:2 * LANES]
            mx = jnp.maximum(jnp.max(jnp.maximum(s0, s1), axis=1, keepdims=True), sink)
            ps.append(jnp.concatenate([jnp.exp2(s0 - mx).astype(BF16), jnp.exp2(s1 - mx).astype(BF16)], axis=1))
            tails.append(jnp.exp2(sink - mx))
        pv = _dot(jnp.concatenate(ps, axis=0), vw)
        outs = []
        for g in range(B_GROUP):
            blk = pv[g * SWA_BLOCK:(g + 1) * SWA_BLOCK]
            outs.append(blk[:, 0:LANES] / (blk[:, LANES:2 * LANES] + tails[g]))
        for h2 in range(B_GROUP // 2):
            o_ref[n * SWA_BLOCK:(n + 1) * SWA_BLOCK, h2 * LANES:(h2 + 1) * LANES] = (
                jnp.where(lane < HEAD_DIM, outs[2 * h2], outs[2 * h2 + 1]).astype(o_ref.dtype))


def _swa_call(proj, slopes2, sinks2, pos_aug, *, tq=512):
    B, S, _ = proj.shape
    kern = functools.partial(_swa_kernel, tq=tq)
    qw = B_GROUP * LANES
    return pl.pallas_call(
        kern,
        grid=(B, B_KV_HEADS, S // tq),
        in_specs=[
            pl.BlockSpec(memory_space=pltpu.SMEM),
            pl.BlockSpec(memory_space=pltpu.SMEM),
            pl.BlockSpec((None, tq, qw), lambda b, kv, i: (b, i, P_QB // B_GROUP + kv)),
            pl.BlockSpec((None, S, LANES), lambda b, kv, i: (b, 0, P_KB)),
            pl.BlockSpec((None, S, LANES), lambda b, kv, i: (b, 0, P_VB + kv)),
            pl.BlockSpec((S, LANES), lambda b, kv, i: (0, 0)),
        ],
        out_specs=pl.BlockSpec((None, tq, 2 * LANES), lambda b, kv, i: (b, i, kv)),
        out_shape=jax.ShapeDtypeStruct((B, S, BRANCH_WIDTH), BF16),
        scratch_shapes=[pltpu.VMEM((B_GROUP * SWA_BLOCK, LANES), BF16)],
        compiler_params=_cparams(("parallel", "parallel", "arbitrary")),
        name="swa",
    )(slopes2, sinks2, proj, proj, proj, pos_aug)


def _post_kernel(x_ref, g_ref, fg_ref, ya_ref, yb_ref, yc_ref, wg_ref, wgm_ref, wua_ref, wub_ref, wuc_ref,
                 wo_ref, o_ref, *, final):
    x = x_ref[...]
    h = _rms(x, g_ref[...], RMS_EPS).astype(BF16)
    merged = None
    for br, (y_ref, wu_ref) in enumerate(((ya_ref, wua_ref), (yb_ref, wub_ref), (yc_ref, wuc_ref))):
        g = _dot(h, wg_ref[:, br * BRANCH_WIDTH:(br + 1) * BRANCH_WIDTH])
        t = (y_ref[...].astype(F32) * (g * jax.nn.sigmoid(g))).astype(BF16)
        u = _dot(t, wu_ref[...])
        gate = jax.nn.sigmoid(_dot(h, wgm_ref[:, br * D_MODEL:(br + 1) * D_MODEL]))
        merged = gate * u if merged is None else merged + gate * u
    out = x + _dot(merged.astype(BF16), wo_ref[...])
    if final:
        out = _rms(out, fg_ref[...], RMS_EPS)
    o_ref[...] = out


def _post_call(x, norm_gain, final_gain, ya, yb, yc, w_g, w_gm, w_ua, w_ub, w_uc, w_o, layer, final, *, tm=512):
    B, S, D = x.shape
    kern = functools.partial(_post_kernel, final=final)
    tok = lambda w: pl.BlockSpec((None, tm, w), lambda b, i: (b, i, 0))
    lay = lambda r, c: pl.BlockSpec((None, r, c), lambda b, i: (layer, 0, 0))
    return pl.pallas_call(
        kern,
        grid=(B, S // tm),
        in_specs=[
            tok(D), lay(1, D), pl.BlockSpec((1, D), lambda b, i: (0, 0)),
            tok(BRANCH_WIDTH), tok(BRANCH_WIDTH), tok(BRANCH_WIDTH),
            lay(D, N_BRANCH * BRANCH_WIDTH), lay(D, N_BRANCH * D),
            lay(BRANCH_WIDTH, D), lay(BRANCH_WIDTH, D), lay(BRANCH_WIDTH, D), lay(D, D),
        ],
        out_specs=tok(D),
        out_shape=jax.ShapeDtypeStruct((B, S, D), F32),
        compiler_params=_cparams(("parallel", "parallel")),
        name="post",
    )(x, norm_gain, final_gain, ya, yb, yc, w_g, w_gm, w_ua, w_ub, w_uc, w_o)


def _alibi_slopes(n_heads):
    return 2.0 ** (-8.0 * jnp.arange(1, n_heads + 1, dtype=F32) / n_heads)


def _prepare_weights(w_in):
    scale = HEAD_DIM ** -0.5
    scale2 = scale * LOG2E
    sl = lambda o, n: w_in[:, :, o:o + n]
    zeros64 = jnp.zeros(w_in.shape[:2] + (HEAD_DIM,), w_in.dtype)
    qb = sl(O_QB, 512) * scale2
    qb_blocks = []
    for n in range(B_HEADS):
        qn = qb[:, :, n * HEAD_DIM:(n + 1) * HEAD_DIM]
        qb_blocks += [qn, zeros64] if n // B_GROUP == 0 else [zeros64, qn]
    vb = sl(O_VB, 128)
    v0, v1 = vb[:, :, :HEAD_DIM], vb[:, :, HEAD_DIM:]
    w_pre = jnp.concatenate(
        [sl(O_QA, 512) * scale2, sl(O_KA, 512), sl(O_VA, 512),
         sl(O_QC, 512) * scale2, sl(O_KC, 512), sl(O_VC, 512)]
        + qb_blocks + [sl(O_KB, 128), v0, v0, v1, v1], axis=-1).astype(BF16)
    wf = jnp.pad(jnp.tile(sl(O_FC, C_HEADS), (1, 1, N_PIECES)),
                 ((0, 0), (0, 0), (0, LANES - N_PIECES * C_HEADS))).astype(BF16)
    w_g = jnp.concatenate([sl(O_GA, 512), sl(O_GB, 512), sl(O_GC, 512)], axis=-1).astype(BF16)
    w_gm = sl(O_GM, N_BRANCH * D_MODEL).astype(BF16)
    return w_pre, wf, w_g, w_gm


def _placement():
    place = np.zeros((LANES, N_PAIRS * LANES), np.float32)
    for x in range(N_PIECES):
        for h in range(C_HEADS):
            place[C_HEADS * x + h, (h // 2) * LANES + N_PIECES * (h % 2) + x] = 1.0
    return jnp.asarray(place, BF16)


def _alibi_pieces(n_heads, seq):
    slopes = 2.0 ** (-8.0 * np.arange(1, n_heads + 1, dtype=np.float64) / n_heads)
    rest = (slopes[:, None] * (LOG2E * np.arange(seq, dtype=np.float64))[None, :]).astype(np.float32)
    pieces = np.zeros((n_heads, seq, N_PIECES), np.float32)
    for x in range(N_PIECES):
        pieces[:, :, x] = rest.astype(BF16).astype(np.float32)
        rest = rest - pieces[:, :, x]
    return pieces


def _alibi_aug(seq):
    table = np.zeros((A_HEADS, seq, LANES), np.float32)
    table[:, :, 0:N_PIECES] = _alibi_pieces(A_HEADS, seq)
    return jnp.asarray(table, BF16)


def _swa_pos_aug(seq):
    table = np.zeros((seq, LANES), np.float32)
    pieces = _alibi_pieces(B_HEADS, seq)
    for h in range(B_HEADS):
        table[:, N_PIECES * h:N_PIECES * (h + 1)] = pieces[h]
    return jnp.asarray(table, BF16)


def kernel(x, norm_gain, w_in, b_forget, lambda_q1, lambda_k1, lambda_q2, lambda_k2, subln_gain, sinks,
           w_up_a, w_up_b, w_up_c, w_o, final_gain):
    B, S, D = x.shape
    depth = w_in.shape[0]
    w_pre, wf, w_g, w_gm = _prepare_weights(w_in)
    w_ua, w_ub, w_uc, w_ob = (w.astype(BF16) for w in (w_up_a, w_up_b, w_up_c, w_o))
    gain3 = norm_gain.reshape(depth, 1, D)
    fgain = final_gain.reshape(1, D)
    bf3 = jnp.pad(jnp.tile(b_forget.astype(F32), (1, N_PIECES)),
                  ((0, 0), (0, LANES - N_PIECES * C_HEADS))).reshape(depth, 1, LANES)
    place = _placement()
    alibi_aug = _alibi_aug(S)
    slopes2_b = _alibi_slopes(B_HEADS) * LOG2E
    pos_aug = _swa_pos_aug(S)
    pad = lambda v: jnp.pad(v.astype(F32), (0, LANES - v.shape[0]))

    for l in range(depth):
        lam_init = 0.8 - 0.6 * math.exp(-0.3 * l)
        proj = _pre_call(x, gain3, w_pre, wf, bf3, place, l)
        consts = jnp.array([lam_init, 1.0 - lam_init], F32)
        lam_rows = jnp.zeros((8, LANES), F32).at[0:4].set(
            jnp.stack([pad(lambda_q1[l]), pad(lambda_k1[l]), pad(lambda_q2[l]), pad(lambda_k2[l])]))
        ya = _flash_call("diff", proj, P_QA, P_KA, P_VA,
                         (consts, alibi_aug, lam_rows, subln_gain[l].astype(F32).reshape(1, LANES)))
        yc = _flash_call("fox", proj, P_QC, P_KC, P_VC)
        yb = _swa_call(proj, slopes2_b, sinks[l].astype(F32) * LOG2E, pos_aug)
        x = _post_call(x, gain3, fgain, ya, yb, yc, w_g, w_gm, w_ua, w_ub, w_uc, w_ob, l, l == depth - 1)
    return x
```

```python
import functools
import math

import numpy as np
import jax
import jax.numpy as jnp
from jax import lax
from jax.experimental import pallas as pl
from jax.experimental.pallas import tpu as pltpu

F32 = jnp.float32
BF16 = jnp.bfloat16

D_MODEL = 1024
HEAD_DIM = 64
LANES = 128
RMS_EPS = 1e-6
SUBLN_EPS = 1e-5
A_HEADS = 4
B_HEADS = 8
B_KV_HEADS = 2
B_GROUP = B_HEADS // B_KV_HEADS
C_HEADS = 8
WINDOW = 128
SWA_BLOCK = 128
N_BRANCH = 3
BRANCH_WIDTH = 512
N_PAIRS = 4
N_PIECES = 3
LOG2E = math.log2(math.e)

_SPLITS = (512, 512, 512, 512, 512, 128, 128, 512, 512, 512, 512, 8, 512, 3 * D_MODEL)
_OFF = [0]
for _s in _SPLITS:
    _OFF.append(_OFF[-1] + _s)
(O_QA, O_KA, O_VA, O_GA, O_QB, O_KB, O_VB, O_GB, O_QC, O_KC, O_VC, O_FC, O_GC, O_GM, _D_IN) = _OFF

P_QA, P_KA, P_VA = 0, 4, 8
P_QC, P_KC, P_VC = 12, 16, 20
P_QB = 24
P_KB = 32
P_VB = 33
P_MM = 35
P_AUG = 35
P_BLOCKS = 39
P_WIDTH = P_BLOCKS * LANES
MM_WIDTH = P_MM * LANES

VMEM_LIMIT = 56 * 1024 * 1024


def _cparams(sem):
    return pltpu.CompilerParams(dimension_semantics=sem, vmem_limit_bytes=VMEM_LIMIT)


def _rms(x, gain, eps):
    ms = jnp.mean(x * x, axis=-1, keepdims=True)
    return x * lax.rsqrt(ms + eps) * gain


def _dot(a, b):
    return jnp.dot(a, b, preferred_element_type=F32)


def _dot_nt(a, b):
    return lax.dot_general(a, b, (((1,), (1,)), ((), ())), preferred_element_type=F32)


def _split3(v):
    hi = v.astype(BF16)
    r = v - hi.astype(F32)
    mid = r.astype(BF16)
    lo = (r - mid.astype(F32)).astype(BF16)
    return hi, mid, lo


def _pre_kernel(x_ref, g_ref, w_ref, wf_ref, bf_ref, place_ref, proj_ref, carry_ref, *, tm, chunk):
    i = pl.program_id(1)

    @pl.when(i == 0)
    def _():
        carry_ref[...] = jnp.zeros_like(carry_ref)

    h = _rms(x_ref[...], g_ref[...], RMS_EPS).astype(BF16)

    lane = lax.broadcasted_iota(jnp.int32, (tm, LANES), 1)

    def pieces_by_group(v):
        p0, p1, p2 = (p.astype(F32) for p in _split3(v))
        by_group = jnp.where(lane < C_HEADS, p0, jnp.where(lane < 2 * C_HEADS, p1, p2))
        return jnp.where(lane < N_PIECES * C_HEADS, by_group, 0.0).astype(BF16)

    z = _dot(h, wf_ref[...]) + bf_ref[...]
    logf = jnp.minimum(z, 0.0) - jnp.log1p(jnp.exp(-jnp.abs(z)))
    row = lax.broadcasted_iota(jnp.int32, (tm, tm), 0)
    col = lax.broadcasted_iota(jnp.int32, (tm, tm), 1)
    lower = (col <= row).astype(BF16)
    part = _dot(lower, pieces_by_group(logf))
    total = part + pltpu.roll(part, LANES - C_HEADS, 1) + pltpu.roll(part, LANES - 2 * C_HEADS, 1)
    total = jnp.where(lane < C_HEADS, total, 0.0)
    c = total + pltpu.roll(total, C_HEADS, 1) + pltpu.roll(total, 2 * C_HEADS, 1) + carry_ref[0:1, :]
    carry_ref[...] = jnp.broadcast_to(c[tm - 1:tm, :], carry_ref.shape)
    proj_ref[:, MM_WIDTH:P_WIDTH] = _dot(pieces_by_group(c * (-LOG2E)), place_ref[...]).astype(BF16)

    for c0 in range(0, MM_WIDTH, chunk):
        c1 = min(c0 + chunk, MM_WIDTH)
        proj_ref[:, c0:c1] = _dot(h, w_ref[:, c0:c1]).astype(BF16)


def _pre_call(x, norm_gain, w_pre, wf, b_f, place, layer, *, tm=512, chunk=512):
    B, S, D = x.shape
    kern = functools.partial(_pre_kernel, tm=tm, chunk=chunk)
    return pl.pallas_call(
        kern,
        grid=(B, S // tm),
        in_specs=[
            pl.BlockSpec((None, tm, D), lambda b, i: (b, i, 0)),
            pl.BlockSpec((None, 1, D), lambda b, i: (layer, 0, 0)),
            pl.BlockSpec((None, D, MM_WIDTH), lambda b, i: (layer, 0, 0)),
            pl.BlockSpec((None, D, LANES), lambda b, i: (layer, 0, 0)),
            pl.BlockSpec((None, 1, LANES), lambda b, i: (layer, 0, 0)),
            pl.BlockSpec((LANES, N_PAIRS * LANES), lambda b, i: (0, 0)),
        ],
        out_specs=pl.BlockSpec((None, tm, P_WIDTH), lambda b, i: (b, i, 0)),
        out_shape=jax.ShapeDtypeStruct((B, S, P_WIDTH), BF16),
        scratch_shapes=[pltpu.VMEM((8, LANES), F32)],
        compiler_params=_cparams(("parallel", "arbitrary")),
        name="pre",
    )(x, norm_gain, w_pre, wf, b_f, place)


def _chunk_order(nq):
    order = [(t, t) for t in range(nq)]
    order += [(qi, c) for qi in range(1, nq) for c in range(qi)]
    return order


def _flash_kernel(*refs, mode, tq, nq, unroll):
    if mode == "diff":
        (consts_ref, tab_ref, q_ref, k_ref, v_ref, aug_ref, lam_ref, sg_ref, o_ref,
         qe_ref, m_ref, acc_ref, sa_ref, sb_ref) = refs
    else:
        (tab_ref, q_ref, k_ref, v_ref, aug_ref, o_ref, qe_ref, m_ref, acc_ref, sa_ref, sb_ref) = refs
    tk = tq
    rows = 2 * tq
    n_chunks = nq * (nq + 1) // 2
    lane = lax.broadcasted_iota(jnp.int32, (tq, LANES), 1)

    def build_qe(t, carry):
        q = q_ref[pl.ds(pl.multiple_of(t * tq, tq), tq), :]
        zero = jnp.zeros_like(q)
        for m in range(2):
            in_map = (lane >= HEAD_DIM) if m else (lane < HEAD_DIM)
            first = N_PIECES * m if mode == "fox" else 0
            ones_at_pieces = jnp.where(lane >= first, 1.0, 0.0) * jnp.where(lane < first + N_PIECES, 1.0, 0.0)
            qe_ref[t, m * tq:(m + 1) * tq, 0:LANES] = jnp.where(in_map, q, zero)
            qe_ref[t, m * tq:(m + 1) * tq, LANES:2 * LANES] = ones_at_pieces.astype(BF16)
        return carry

    lax.fori_loop(0, nq, build_qe, 0)

    def qk(t):
        qi = tab_ref[0, t]
        start = pl.multiple_of(tab_ref[1, t] * tk, tk)
        kc = jnp.concatenate([k_ref[pl.ds(start, tk), :], aug_ref[pl.ds(start, tk), :]], axis=1)
        return _dot_nt(qe_ref[qi], kc)

    def softmax_pv(s_ref, t, diagonal):
        qi = tab_ref[0, t]
        start = pl.multiple_of(tab_ref[1, t] * tk, tk)
        vc = jnp.concatenate([v_ref[pl.ds(start, tk), :], jnp.ones((tk, LANES), BF16)], axis=1)
        s = s_ref[...]
        if diagonal:
            r = lax.broadcasted_iota(jnp.int32, (rows, tk), 0)
            c = lax.broadcasted_iota(jnp.int32, (rows, tk), 1)
            s = jnp.where(c <= jnp.where(r >= tq, r - tq, r), s, -jnp.inf)
        cols = [s[:, c0:c0 + LANES] for c0 in range(0, tk, LANES)]
        mx = cols[0]
        for sc in cols[1:]:
            mx = jnp.maximum(mx, sc)
        m_cur = jnp.max(mx, axis=1, keepdims=True)
        if diagonal:
            m_new = jnp.broadcast_to(m_cur, (rows, LANES))
        else:
            m_prev = m_ref[qi]
            m_new = jnp.maximum(m_prev, m_cur)
            alpha = jnp.exp2(m_prev - m_new)
        p = jnp.concatenate([jnp.exp2(sc - m_new).astype(BF16) for sc in cols], axis=1)
        pv = _dot(p, vc)
        if diagonal:
            acc_ref[qi] = pv
        else:
            acc_ref[qi, :, 0:LANES] = alpha * acc_ref[qi, :, 0:LANES] + pv[:, 0:LANES]
            acc_ref[qi, :, LANES:2 * LANES] = alpha * acc_ref[qi, :, LANES:2 * LANES] + pv[:, LANES:2 * LANES]
        m_ref[qi] = m_new

    s_refs = (sa_ref, sb_ref)

    def run(first, last, diagonal, unroll):
        def body(i, carry):
            for u in range(unroll):
                t = first + i * unroll + u
                s_refs[(u + 1) % 2][...] = qk(t + 1)
                softmax_pv(s_refs[u % 2], t, diagonal)
            return carry
        lax.fori_loop(0, (last - first) // unroll, body, 0)

    def run_all(first, last, diagonal):
        main = first + (last - first) // unroll * unroll
        if main > first:
            run(first, main, diagonal, unroll)
        if last > main:
            run(main, last, diagonal, last - main)

    sa_ref[...] = qk(0)
    run_all(0, nq, True)
    run_all(nq, n_chunks, False)

    def finalize(t, carry):
        o0 = acc_ref[t, 0:tq, 0:LANES] / acc_ref[t, 0:tq, LANES:2 * LANES]
        o1 = acc_ref[t, tq:rows, 0:LANES] / acc_ref[t, tq:rows, LANES:2 * LANES]
        if mode == "diff":
            lam_init = consts_ref[0]
            lp = lam_ref[...]
            lam = (jnp.exp(jnp.sum(lp[0:1] * lp[1:2], axis=1, keepdims=True))
                   - jnp.exp(jnp.sum(lp[2:3] * lp[3:4], axis=1, keepdims=True)) + lam_init)
            y = _rms(o0 - lam * o1, sg_ref[...], SUBLN_EPS) * consts_ref[1]
        else:
            y = jnp.where(lane < HEAD_DIM, o0, o1)
        o_ref[t * tq:(t + 1) * tq, :] = y.astype(o_ref.dtype)
        return carry

    for t in range(nq):
        finalize(t, 0)


def _flash_call(mode, proj, q_blk, k_blk, v_blk, extra=(), *, tq=512, unroll=14):
    B, S, _ = proj.shape
    nq = S // tq
    order = _chunk_order(nq)
    assert nq % 2 == 0 and len(order) % 2 == 0, "the chunk loop handles two chunks per trip"
    table = jnp.asarray(np.array(order + [order[-1]], np.int32).T)
    kern = functools.partial(_flash_kernel, mode=mode, tq=tq, nq=nq, unroll=unroll)
    seq = lambda blk: pl.BlockSpec((None, S, LANES), lambda b, p: (b, 0, blk + p))
    smem = pl.BlockSpec(memory_space=pltpu.SMEM)
    in_specs = [smem, seq(q_blk), seq(k_blk), seq(v_blk)]
    if mode == "diff":
        consts, alibi_aug, lam_rows, subln = extra
        in_specs = [smem] + in_specs + [
            pl.BlockSpec((None, S, LANES), lambda b, p: (p, 0, 0)),
            pl.BlockSpec((8, LANES), lambda b, p: (0, 0)),
            pl.BlockSpec((1, LANES), lambda b, p: (0, 0)),
        ]
        args = (consts, table, proj, proj, proj, alibi_aug, lam_rows, subln)
    else:
        in_specs = in_specs + [seq(P_AUG)]
        args = (table, proj, proj, proj, proj)
    return pl.pallas_call(
        kern,
        grid=(B, N_PAIRS),
        in_specs=in_specs,
        out_specs=pl.BlockSpec((None, S, LANES), lambda b, p: (b, 0, p)),
        out_shape=jax.ShapeDtypeStruct((B, S, BRANCH_WIDTH), BF16),
        scratch_shapes=[
            pltpu.VMEM((nq, 2 * tq, 2 * LANES), BF16),
            pltpu.VMEM((nq, 2 * tq, LANES), F32),
            pltpu.VMEM((nq, 2 * tq, 2 * LANES), F32),
            pltpu.VMEM((2 * tq, tq), F32),
            pltpu.VMEM((2 * tq, tq), F32),
        ],
        compiler_params=_cparams(("parallel", "parallel")),
        name="flash_" + mode,
    )(*args)


def _swa_kernel(slopes_ref, sinks_ref, q_ref, k_ref, v_ref, pos_ref, o_ref, e_ref, *, tq):
    kv = pl.program_id(1)
    qi = pl.program_id(2)
    nb = tq // SWA_BLOCK
    lane = lax.broadcasted_iota(jnp.int32, (SWA_BLOCK, LANES), 1)
    row = lax.broadcasted_iota(jnp.int32, (SWA_BLOCK, LANES), 0)
    r = lax.broadcasted_iota(jnp.int32, (SWA_BLOCK, 2 * SWA_BLOCK), 0)
    c = lax.broadcasted_iota(jnp.int32, (SWA_BLOCK, 2 * SWA_BLOCK), 1)
    for g in range(B_GROUP):
        first = N_PIECES * (kv * B_GROUP + g)
        ones_at_pieces = jnp.where(lane >= first, 1.0, 0.0) * jnp.where(lane < first + N_PIECES, 1.0, 0.0)
        e_ref[g * SWA_BLOCK:(g + 1) * SWA_BLOCK, :] = ones_at_pieces.astype(BF16)
    ones = jnp.ones((2 * SWA_BLOCK, LANES), BF16)
    for n in range(nb):
        q_start = qi * tq + n * SWA_BLOCK
        k_start = pl.multiple_of(jnp.maximum(q_start - SWA_BLOCK, 0), SWA_BLOCK)
        kw = jnp.concatenate([k_ref[pl.ds(k_start, 2 * SWA_BLOCK), :],
                              pos_ref[pl.ds(k_start, 2 * SWA_BLOCK), :]], axis=1)
        vw = jnp.concatenate([v_ref[pl.ds(k_start, 2 * SWA_BLOCK), :], ones], axis=1)
        qs = jnp.concatenate(
            [q_ref[n * SWA_BLOCK:(n + 1) * SWA_BLOCK, g * LANES:(g + 1) * LANES] for g in range(B_GROUP)],
            axis=0)
        s_all = _dot_nt(jnp.concatenate([qs, e_ref[...]], axis=1), kw)
        dist = (q_start - k_start) + r - c
        valid = (dist >= 0) & (dist < WINDOW)
        qpos = (q_start + row).astype(F32)
        ps, tails = [], []
        for g in range(B_GROUP):
            h = kv * B_GROUP + g
            sink = sinks_ref[h] + slopes_ref[h] * qpos
            s = jnp.where(valid, s_all[g * SWA_BLOCK:(g + 1) * SWA_BLOCK], -jnp.inf)
            s0, s1 = s[:, 0:LANES], s[:, LANES:2 * LANES]
            mx = jnp.maximum(jnp.max(jnp.maximum(s0, s1), axis=1, keepdims=True), sink)
            ps.append(jnp.concatenate([jnp.exp2(s0 - mx).astype(BF16), jnp.exp2(s1 - mx).astype(BF16)], axis=1))
            tails.append(jnp.exp2(sink - mx))
        pv = _dot(jnp.concatenate(ps, axis=0), vw)
        outs = []
        for g in range(B_GROUP):
            blk = pv[g * SWA_BLOCK:(g + 1) * SWA_BLOCK]
            outs.append(blk[:, 0:LANES] / (blk[:, LANES:2 * LANES] + tails[g]))
        for h2 in range(B_GROUP // 2):
            o_ref[n * SWA_BLOCK:(n + 1) * SWA_BLOCK, h2 * LANES:(h2 + 1) * LANES] = (
                jnp.where(lane < HEAD_DIM, outs[2 * h2], outs[2 * h2 + 1]).astype(o_ref.dtype))


def _swa_call(proj, slopes2, sinks2, pos_aug, *, tq=512):
    B, S, _ = proj.shape
    kern = functools.partial(_swa_kernel, tq=tq)
    qw = B_GROUP * LANES
    return pl.pallas_call(
        kern,
        grid=(B, B_KV_HEADS, S // tq),
        in_specs=[
            pl.BlockSpec(memory_space=pltpu.SMEM),
            pl.BlockSpec(memory_space=pltpu.SMEM),
            pl.BlockSpec((None, tq, qw), lambda b, kv, i: (b, i, P_QB // B_GROUP + kv)),
            pl.BlockSpec((None, S, LANES), lambda b, kv, i: (b, 0, P_KB)),
            pl.BlockSpec((None, S, LANES), lambda b, kv, i: (b, 0, P_VB + kv)),
            pl.BlockSpec((S, LANES), lambda b, kv, i: (0, 0)),
        ],
        out_specs=pl.BlockSpec((None, tq, 2 * LANES), lambda b, kv, i: (b, i, kv)),
        out_shape=jax.ShapeDtypeStruct((B, S, BRANCH_WIDTH), BF16),
        scratch_shapes=[pltpu.VMEM((B_GROUP * SWA_BLOCK, LANES), BF16)],
        compiler_params=_cparams(("parallel", "parallel", "arbitrary")),
        name="swa",
    )(slopes2, sinks2, proj, proj, proj, pos_aug)


def _post_kernel(x_ref, g_ref, fg_ref, ya_ref, yb_ref, yc_ref, wg_ref, wgm_ref, wua_ref, wub_ref, wuc_ref,
                 wo_ref, o_ref, *, final):
    x = x_ref[...]
    h = _rms(x, g_ref[...], RMS_EPS).astype(BF16)
    merged = None
    for br, (y_ref, wu_ref) in enumerate(((ya_ref, wua_ref), (yb_ref, wub_ref), (yc_ref, wuc_ref))):
        g = _dot(h, wg_ref[:, br * BRANCH_WIDTH:(br + 1) * BRANCH_WIDTH])
        t = (y_ref[...].astype(F32) * (g * jax.nn.sigmoid(g))).astype(BF16)
        u = _dot(t, wu_ref[...])
        gate = jax.nn.sigmoid(_dot(h, wgm_ref[:, br * D_MODEL:(br + 1) * D_MODEL]))
        merged = gate * u if merged is None else merged + gate * u
    out = x + _dot(merged.astype(BF16), wo_ref[...])
    if final:
        out = _rms(out, fg_ref[...], RMS_EPS)
    o_ref[...] = out


def _post_call(x, norm_gain, final_gain, ya, yb, yc, w_g, w_gm, w_ua, w_ub, w_uc, w_o, layer, final, *, tm=1024):
    B, S, D = x.shape
    kern = functools.partial(_post_kernel, final=final)
    tok = lambda w: pl.BlockSpec((None, tm, w), lambda b, i: (b, i, 0))
    lay = lambda r, c: pl.BlockSpec((None, r, c), lambda b, i: (layer, 0, 0), pipeline_mode=pl.Buffered(1))
    return pl.pallas_call(
        kern,
        grid=(B, S // tm),
        in_specs=[
            tok(D), lay(1, D), pl.BlockSpec((1, D), lambda b, i: (0, 0)),
            tok(BRANCH_WIDTH), tok(BRANCH_WIDTH), tok(BRANCH_WIDTH),
            lay(D, N_BRANCH * BRANCH_WIDTH), lay(D, N_BRANCH * D),
            lay(BRANCH_WIDTH, D), lay(BRANCH_WIDTH, D), lay(BRANCH_WIDTH, D), lay(D, D),
        ],
        out_specs=tok(D),
        out_shape=jax.ShapeDtypeStruct((B, S, D), F32),
        compiler_params=_cparams(("parallel", "parallel")),
        name="post",
    )(x, norm_gain, final_gain, ya, yb, yc, w_g, w_gm, w_ua, w_ub, w_uc, w_o)


def _alibi_slopes(n_heads):
    return 2.0 ** (-8.0 * jnp.arange(1, n_heads + 1, dtype=F32) / n_heads)


def _prepare_weights(w_in):
    col_scale = np.ones((_D_IN,), np.float32)
    for o in (O_QA, O_QB, O_QC):
        col_scale[o:o + 512] = HEAD_DIM ** -0.5 * LOG2E
    wb = (w_in * col_scale).astype(BF16)
    sl = lambda o, n: wb[:, :, o:o + n]
    zeros64 = jnp.zeros(wb.shape[:2] + (HEAD_DIM,), BF16)
    qb_blocks = []
    for n in range(B_HEADS):
        qn = sl(O_QB + n * HEAD_DIM, HEAD_DIM)
        qb_blocks += [qn, zeros64] if n // B_GROUP == 0 else [zeros64, qn]
    v0, v1 = sl(O_VB, HEAD_DIM), sl(O_VB + HEAD_DIM, HEAD_DIM)
    w_pre = jnp.concatenate(
        [sl(O_QA, 512), sl(O_KA, 512), sl(O_VA, 512), sl(O_QC, 512), sl(O_KC, 512), sl(O_VC, 512)]
        + qb_blocks + [sl(O_KB, 128), v0, v0, v1, v1], axis=-1)
    wf = jnp.pad(jnp.tile(sl(O_FC, C_HEADS), (1, 1, N_PIECES)),
                 ((0, 0), (0, 0), (0, LANES - N_PIECES * C_HEADS)))
    w_g = jnp.concatenate([sl(O_GA, 512), sl(O_GB, 512), sl(O_GC, 512)], axis=-1)
    w_gm = sl(O_GM, N_BRANCH * D_MODEL)
    return w_pre, wf, w_g, w_gm


def _placement():
    place = np.zeros((LANES, N_PAIRS * LANES), np.float32)
    for x in range(N_PIECES):
        for h in range(C_HEADS):
            place[C_HEADS * x + h, (h // 2) * LANES + N_PIECES * (h % 2) + x] = 1.0
    return jnp.asarray(place, BF16)


def _alibi_pieces(n_heads, seq):
    slopes = 2.0 ** (-8.0 * np.arange(1, n_heads + 1, dtype=np.float64) / n_heads)
    rest = (slopes[:, None] * (LOG2E * np.arange(seq, dtype=np.float64))[None, :]).astype(np.float32)
    pieces = np.zeros((n_heads, seq, N_PIECES), np.float32)
    for x in range(N_PIECES):
        pieces[:, :, x] = rest.astype(BF16).astype(np.float32)
        rest = rest - pieces[:, :, x]
    return pieces


def _alibi_aug(seq):
    table = np.zeros((A_HEADS, seq, LANES), np.float32)
    table[:, :, 0:N_PIECES] = _alibi_pieces(A_HEADS, seq)
    return jnp.asarray(table, BF16)


def _swa_pos_aug(seq):
    table = np.zeros((seq, LANES), np.float32)
    pieces = _alibi_pieces(B_HEADS, seq)
    for h in range(B_HEADS):
        table[:, N_PIECES * h:N_PIECES * (h + 1)] = pieces[h]
    return jnp.asarray(table, BF16)


def kernel(x, norm_gain, w_in, b_forget, lambda_q1, lambda_k1, lambda_q2, lambda_k2, subln_gain, sinks,
           w_up_a, w_up_b, w_up_c, w_o, final_gain):
    B, S, D = x.shape
    depth = w_in.shape[0]
    w_pre, wf, w_g, w_gm = _prepare_weights(w_in)
    w_ua, w_ub, w_uc, w_ob = (w.astype(BF16) for w in (w_up_a, w_up_b, w_up_c, w_o))
    gain3 = norm_gain.reshape(depth, 1, D)
    fgain = final_gain.reshape(1, D)
    bf3 = jnp.pad(jnp.tile(b_forget.astype(F32), (1, N_PIECES)),
                  ((0, 0), (0, LANES - N_PIECES * C_HEADS))).reshape(depth, 1, LANES)
    place = _placement()
    alibi_aug = _alibi_aug(S)
    slopes2_b = _alibi_slopes(B_HEADS) * LOG2E
    pos_aug = _swa_pos_aug(S)
    pad = lambda v: jnp.pad(v.astype(F32), (0, LANES - v.shape[0]))

    for l in range(depth):
        lam_init = 0.8 - 0.6 * math.exp(-0.3 * l)
        proj = _pre_call(x, gain3, w_pre, wf, bf3, place, l)
        consts = jnp.array([lam_init, 1.0 - lam_init], F32)
        lam_rows = jnp.zeros((8, LANES), F32).at[0:4].set(
            jnp.stack([pad(lambda_q1[l]), pad(lambda_k1[l]), pad(lambda_q2[l]), pad(lambda_k2[l])]))
        ya = _flash_call("diff", proj, P_QA, P_KA, P_VA,
                         (consts, alibi_aug, lam_rows, subln_gain[l].astype(F32).reshape(1, LANES)))
        yc = _flash_call("fox", proj, P_QC, P_KC, P_VC)
        yb = _swa_call(proj, slopes2_b, sinks[l].astype(F32) * LOG2E, pos_aug)
        x = _post_call(x, gain3, fgain, ya, yb, yc, w_g, w_gm, w_ua, w_ub, w_uc, w_ob, l, l == depth - 1)
    return x
```

```python
import functools
import math

import numpy as np
import jax
import jax.numpy as jnp
from jax import lax
from jax.experimental import pallas as pl
from jax.experimental.pallas import tpu as pltpu

F32 = jnp.float32
BF16 = jnp.bfloat16

D_MODEL = 1024
HEAD_DIM = 64
LANES = 128
RMS_EPS = 1e-6
SUBLN_EPS = 1e-5
A_HEADS = 4
B_HEADS = 8
B_KV_HEADS = 2
B_GROUP = B_HEADS // B_KV_HEADS
C_HEADS = 8
WINDOW = 128
SWA_BLOCK = 128
N_BRANCH = 3
BRANCH_WIDTH = 512
N_PAIRS = 4
N_PIECES = 3
LOG2E = math.log2(math.e)

_SPLITS = (512, 512, 512, 512, 512, 128, 128, 512, 512, 512, 512, 8, 512, 3 * D_MODEL)
_OFF = [0]
for _s in _SPLITS:
    _OFF.append(_OFF[-1] + _s)
(O_QA, O_KA, O_VA, O_GA, O_QB, O_KB, O_VB, O_GB, O_QC, O_KC, O_VC, O_FC, O_GC, O_GM, _D_IN) = _OFF

P_QA, P_KA, P_VA = 0, 4, 8
P_QC, P_KC, P_VC = 12, 16, 20
P_QB = 24
P_KB = 32
P_VB = 33
P_MM = 35
P_AUG = 35
P_BLOCKS = 39
P_WIDTH = P_BLOCKS * LANES
MM_WIDTH = P_MM * LANES

VMEM_LIMIT = 56 * 1024 * 1024


def _cparams(sem):
    return pltpu.CompilerParams(dimension_semantics=sem, vmem_limit_bytes=VMEM_LIMIT)


def _rms(x, gain, eps):
    ms = jnp.mean(x * x, axis=-1, keepdims=True)
    return x * lax.rsqrt(ms + eps) * gain


def _dot(a, b):
    return jnp.dot(a, b, preferred_element_type=F32)


def _dot_nt(a, b):
    return lax.dot_general(a, b, (((1,), (1,)), ((), ())), preferred_element_type=F32)


def _split3(v):
    hi = v.astype(BF16)
    r = v - hi.astype(F32)
    mid = r.astype(BF16)
    lo = (r - mid.astype(F32)).astype(BF16)
    return hi, mid, lo


def _pre_kernel(x_ref, g_ref, w_ref, wf_ref, bf_ref, place_ref, proj_ref, carry_ref, *, tm, chunk):
    i = pl.program_id(1)

    @pl.when(i == 0)
    def _():
        carry_ref[...] = jnp.zeros_like(carry_ref)

    h = _rms(x_ref[...], g_ref[...], RMS_EPS).astype(BF16)

    lane = lax.broadcasted_iota(jnp.int32, (tm, LANES), 1)

    def pieces_by_group(v):
        p0, p1, p2 = (p.astype(F32) for p in _split3(v))
        by_group = jnp.where(lane < C_HEADS, p0, jnp.where(lane < 2 * C_HEADS, p1, p2))
        return jnp.where(lane < N_PIECES * C_HEADS, by_group, 0.0).astype(BF16)

    z = _dot(h, wf_ref[...]) + bf_ref[...]
    logf = jnp.minimum(z, 0.0) - jnp.log1p(jnp.exp(-jnp.abs(z)))
    row = lax.broadcasted_iota(jnp.int32, (tm, tm), 0)
    col = lax.broadcasted_iota(jnp.int32, (tm, tm), 1)
    lower = (col <= row).astype(BF16)
    part = _dot(lower, pieces_by_group(logf))
    total = part + pltpu.roll(part, LANES - C_HEADS, 1) + pltpu.roll(part, LANES - 2 * C_HEADS, 1)
    total = jnp.where(lane < C_HEADS, total, 0.0)
    c = total + pltpu.roll(total, C_HEADS, 1) + pltpu.roll(total, 2 * C_HEADS, 1) + carry_ref[0:1, :]
    carry_ref[...] = jnp.broadcast_to(c[tm - 1:tm, :], carry_ref.shape)
    proj_ref[:, MM_WIDTH:P_WIDTH] = _dot(pieces_by_group(c * (-LOG2E)), place_ref[...]).astype(BF16)

    for c0 in range(0, MM_WIDTH, chunk):
        c1 = min(c0 + chunk, MM_WIDTH)
        proj_ref[:, c0:c1] = _dot(h, w_ref[:, c0:c1]).astype(BF16)


def _pre_call(x, norm_gain, w_pre, wf, b_f, place, layer, *, tm=512, chunk=512):
    B, S, D = x.shape
    kern = functools.partial(_pre_kernel, tm=tm, chunk=chunk)
    return pl.pallas_call(
        kern,
        grid=(B, S // tm),
        in_specs=[
            pl.BlockSpec((None, tm, D), lambda b, i: (b, i, 0)),
            pl.BlockSpec((None, 1, D), lambda b, i: (layer, 0, 0)),
            pl.BlockSpec((None, D, MM_WIDTH), lambda b, i: (layer, 0, 0)),
            pl.BlockSpec((None, D, LANES), lambda b, i: (layer, 0, 0)),
            pl.BlockSpec((None, 1, LANES), lambda b, i: (layer, 0, 0)),
            pl.BlockSpec((LANES, N_PAIRS * LANES), lambda b, i: (0, 0)),
        ],
        out_specs=pl.BlockSpec((None, tm, P_WIDTH), lambda b, i: (b, i, 0)),
        out_shape=jax.ShapeDtypeStruct((B, S, P_WIDTH), BF16),
        scratch_shapes=[pltpu.VMEM((8, LANES), F32)],
        compiler_params=_cparams(("parallel", "arbitrary")),
        name="pre",
    )(x, norm_gain, w_pre, wf, b_f, place)


def _chunk_order(nq):
    order = [(t, t) for t in range(nq)]
    order += [(qi, c) for qi in range(1, nq) for c in range(qi)]
    return order


def _flash_kernel(*refs, mode, tq, nq, unroll):
    if mode == "diff":
        (consts_ref, tab_ref, q_ref, k_ref, v_ref, aug_ref, lam_ref, sg_ref, o_ref,
         qe_ref, m_ref, acc_ref, sa_ref, sb_ref) = refs
    else:
        (tab_ref, q_ref, k_ref, v_ref, aug_ref, o_ref, qe_ref, m_ref, acc_ref, sa_ref, sb_ref) = refs
    tk = tq
    rows = 2 * tq
    half = tq // 2
    n_chunks = nq * (nq + 1) // 2
    lane = lax.broadcasted_iota(jnp.int32, (half, LANES), 1)

    def build_qe(t, carry):
        for hf in range(2):
            q = q_ref[pl.ds(pl.multiple_of(t * tq + hf * half, half), half), :]
            zero = jnp.zeros_like(q)
            for m in range(2):
                in_map = (lane >= HEAD_DIM) if m else (lane < HEAD_DIM)
                first = N_PIECES * m if mode == "fox" else 0
                ones_at_pieces = jnp.where(lane >= first, 1.0, 0.0) * jnp.where(lane < first + N_PIECES, 1.0, 0.0)
                g0 = (2 * hf + m) * half
                qe_ref[t, g0:g0 + half, 0:LANES] = jnp.where(in_map, q, zero)
                qe_ref[t, g0:g0 + half, LANES:2 * LANES] = ones_at_pieces.astype(BF16)
        return carry

    lax.fori_loop(0, nq, build_qe, 0)

    def keys(start, n):
        return jnp.concatenate([k_ref[pl.ds(start, n), :], aug_ref[pl.ds(start, n), :]], axis=1)

    def values(start, n):
        return jnp.concatenate([v_ref[pl.ds(start, n), :], jnp.ones((n, LANES), BF16)], axis=1)

    def qk(t, s_ref):
        qi = tab_ref[0, t]
        start = pl.multiple_of(tab_ref[1, t] * tk, tk)
        s_ref[...] = _dot_nt(qe_ref[qi], keys(start, tk))

    def qk_diagonal(t, s_ref):
        s_ref[0:tq, 0:half] = _dot_nt(qe_ref[t, 0:tq], keys(t * tk, half))
        s_ref[tq:rows, :] = _dot_nt(qe_ref[t, tq:rows], keys(t * tk, tk))

    def probabilities(s, m_prev):
        cols = [s[:, c0:c0 + LANES] for c0 in range(0, s.shape[1], LANES)]
        mx = cols[0]
        for sc in cols[1:]:
            mx = jnp.maximum(mx, sc)
        m_new = jnp.broadcast_to(jnp.max(mx, axis=1, keepdims=True), (s.shape[0], LANES))
        if m_prev is not None:
            m_new = jnp.maximum(m_prev, m_new)
        return m_new, jnp.concatenate([jnp.exp2(sc - m_new).astype(BF16) for sc in cols], axis=1)

    def softmax_pv(s_ref, t):
        qi = tab_ref[0, t]
        start = pl.multiple_of(tab_ref[1, t] * tk, tk)
        m_prev = m_ref[qi]
        m_new, p = probabilities(s_ref[...], m_prev)
        alpha = jnp.exp2(m_prev - m_new)
        pv = _dot(p, values(start, tk))
        acc_ref[qi, :, 0:LANES] = alpha * acc_ref[qi, :, 0:LANES] + pv[:, 0:LANES]
        acc_ref[qi, :, LANES:2 * LANES] = alpha * acc_ref[qi, :, LANES:2 * LANES] + pv[:, LANES:2 * LANES]
        m_ref[qi] = m_new

    def softmax_pv_diagonal(s_ref, t):
        for r0, n_keys in ((0, half), (tq, tk)):
            r = lax.broadcasted_iota(jnp.int32, (tq, n_keys), 0)
            c = lax.broadcasted_iota(jnp.int32, (tq, n_keys), 1)
            q_pos = jnp.where(r >= half, r - half, r) + (half if r0 else 0)
            s = jnp.where(c <= q_pos, s_ref[r0:r0 + tq, 0:n_keys], -jnp.inf)
            m_new, p = probabilities(s, None)
            acc_ref[t, r0:r0 + tq, :] = _dot(p, values(t * tk, n_keys))
            m_ref[t, r0:r0 + tq, :] = m_new

    s_refs = (sa_ref, sb_ref)

    qk_diagonal(0, sa_ref)
    for t in range(nq):
        if t + 1 < nq:
            qk_diagonal(t + 1, s_refs[(t + 1) % 2])
        else:
            qk(t + 1, s_refs[(t + 1) % 2])
        softmax_pv_diagonal(s_refs[t % 2], t)

    def run(first, last, unroll):
        def body(i, carry):
            for u in range(unroll):
                t = first + i * unroll + u
                qk(t + 1, s_refs[(u + 1) % 2])
                softmax_pv(s_refs[u % 2], t)
            return carry
        lax.fori_loop(0, (last - first) // unroll, body, 0)

    main = nq + (n_chunks - nq) // unroll * unroll
    if main > nq:
        run(nq, main, unroll)
    if n_chunks > main:
        run(main, n_chunks, n_chunks - main)

    def finalize(t):
        for hf in range(2):
            a0 = acc_ref[t, (2 * hf) * half:(2 * hf + 1) * half, :]
            a1 = acc_ref[t, (2 * hf + 1) * half:(2 * hf + 2) * half, :]
            o0 = a0[:, 0:LANES] / a0[:, LANES:2 * LANES]
            o1 = a1[:, 0:LANES] / a1[:, LANES:2 * LANES]
            if mode == "diff":
                lam_init = consts_ref[0]
                lp = lam_ref[...]
                lam = (jnp.exp(jnp.sum(lp[0:1] * lp[1:2], axis=1, keepdims=True))
                       - jnp.exp(jnp.sum(lp[2:3] * lp[3:4], axis=1, keepdims=True)) + lam_init)
                y = _rms(o0 - lam * o1, sg_ref[...], SUBLN_EPS) * consts_ref[1]
            else:
                y = jnp.where(lane < HEAD_DIM, o0, o1)
            o_ref[t * tq + hf * half:t * tq + (hf + 1) * half, :] = y.astype(o_ref.dtype)

    for t in range(nq):
        finalize(t)


def _flash_call(mode, proj, q_blk, k_blk, v_blk, extra=(), *, tq=512, unroll=14):
    B, S, _ = proj.shape
    nq = S // tq
    order = _chunk_order(nq)
    assert nq % 2 == 0 and len(order) % 2 == 0, "the chunk loop handles two chunks per trip"
    table = jnp.asarray(np.array(order + [order[-1]], np.int32).T)
    kern = functools.partial(_flash_kernel, mode=mode, tq=tq, nq=nq, unroll=unroll)
    seq = lambda blk: pl.BlockSpec((None, S, LANES), lambda b, p: (b, 0, blk + p))
    smem = pl.BlockSpec(memory_space=pltpu.SMEM)
    in_specs = [smem, seq(q_blk), seq(k_blk), seq(v_blk)]
    if mode == "diff":
        consts, alibi_aug, lam_rows, subln = extra
        in_specs = [smem] + in_specs + [
            pl.BlockSpec((None, S, LANES), lambda b, p: (p, 0, 0)),
            pl.BlockSpec((8, LANES), lambda b, p: (0, 0)),
            pl.BlockSpec((1, LANES), lambda b, p: (0, 0)),
        ]
        args = (consts, table, proj, proj, proj, alibi_aug, lam_rows, subln)
    else:
        in_specs = in_specs + [seq(P_AUG)]
        args = (table, proj, proj, proj, proj)
    return pl.pallas_call(
        kern,
        grid=(B, N_PAIRS),
        in_specs=in_specs,
        out_specs=pl.BlockSpec((None, S, LANES), lambda b, p: (b, 0, p)),
        out_shape=jax.ShapeDtypeStruct((B, S, BRANCH_WIDTH), BF16),
        scratch_shapes=[
            pltpu.VMEM((nq, 2 * tq, 2 * LANES), BF16),
            pltpu.VMEM((nq, 2 * tq, LANES), F32),
            pltpu.VMEM((nq, 2 * tq, 2 * LANES), F32),
            pltpu.VMEM((2 * tq, tq), F32),
            pltpu.VMEM((2 * tq, tq), F32),
        ],
        compiler_params=_cparams(("parallel", "parallel")),
        name="flash_" + mode,
    )(*args)


def _swa_kernel(slopes_ref, sinks_ref, q_ref, k_ref, v_ref, pos_ref, o_ref, e_ref, *, tq):
    kv = pl.program_id(1)
    qi = pl.program_id(2)
    nb = tq // SWA_BLOCK
    lane = lax.broadcasted_iota(jnp.int32, (SWA_BLOCK, LANES), 1)
    row = lax.broadcasted_iota(jnp.int32, (SWA_BLOCK, LANES), 0)
    r = lax.broadcasted_iota(jnp.int32, (SWA_BLOCK, 2 * SWA_BLOCK), 0)
    c = lax.broadcasted_iota(jnp.int32, (SWA_BLOCK, 2 * SWA_BLOCK), 1)
    for g in range(B_GROUP):
        first = N_PIECES * (kv * B_GROUP + g)
        ones_at_pieces = jnp.where(lane >= first, 1.0, 0.0) * jnp.where(lane < first + N_PIECES, 1.0, 0.0)
        e_ref[g * SWA_BLOCK:(g + 1) * SWA_BLOCK, :] = ones_at_pieces.astype(BF16)
    ones = jnp.ones((2 * SWA_BLOCK, LANES), BF16)
    for n in range(nb):
        q_start = qi * tq + n * SWA_BLOCK
        k_start = pl.multiple_of(jnp.maximum(q_start - SWA_BLOCK, 0), SWA_BLOCK)
        kw = jnp.concatenate([k_ref[pl.ds(k_start, 2 * SWA_BLOCK), :],
                              pos_ref[pl.ds(k_start, 2 * SWA_BLOCK), :]], axis=1)
        vw = jnp.concatenate([v_ref[pl.ds(k_start, 2 * SWA_BLOCK), :], ones], axis=1)
        qs = jnp.concatenate(
            [q_ref[n * SWA_BLOCK:(n + 1) * SWA_BLOCK, g * LANES:(g + 1) * LANES] for g in range(B_GROUP)],
            axis=0)
        s_all = _dot_nt(jnp.concatenate([qs, e_ref[...]], axis=1), kw)
        dist = (q_start - k_start) + r - c
        valid = (dist >= 0) & (dist < WINDOW)
        qpos = (q_start + row).astype(F32)
        ps, tails = [], []
        for g in range(B_GROUP):
            h = kv * B_GROUP + g
            sink = sinks_ref[h] + slopes_ref[h] * qpos
            s = jnp.where(valid, s_all[g * SWA_BLOCK:(g + 1) * SWA_BLOCK], -jnp.inf)
            s0, s1 = s[:, 0:LANES], s[:, LANES:2 * LANES]
            mx = jnp.maximum(jnp.max(jnp.maximum(s0, s1), axis=1, keepdims=True), sink)
            ps.append(jnp.concatenate([jnp.exp2(s0 - mx).astype(BF16), jnp.exp2(s1 - mx).astype(BF16)], axis=1))
            tails.append(jnp.exp2(sink - mx))
        pv = _dot(jnp.concatenate(ps, axis=0), vw)
        outs = []
        for g in range(B_GROUP):
            blk = pv[g * SWA_BLOCK:(g + 1) * SWA_BLOCK]
            outs.append(blk[:, 0:LANES] / (blk[:, LANES:2 * LANES] + tails[g]))
        for h2 in range(B_GROUP // 2):
            o_ref[n * SWA_BLOCK:(n + 1) * SWA_BLOCK, h2 * LANES:(h2 + 1) * LANES] = (
                jnp.where(lane < HEAD_DIM, outs[2 * h2], outs[2 * h2 + 1]).astype(o_ref.dtype))


def _swa_call(proj, slopes2, sinks2, pos_aug, *, tq=512):
    B, S, _ = proj.shape
    kern = functools.partial(_swa_kernel, tq=tq)
    qw = B_GROUP * LANES
    return pl.pallas_call(
        kern,
        grid=(B, B_KV_HEADS, S // tq),
        in_specs=[
            pl.BlockSpec(memory_space=pltpu.SMEM),
            pl.BlockSpec(memory_space=pltpu.SMEM),
            pl.BlockSpec((None, tq, qw), lambda b, kv, i: (b, i, P_QB // B_GROUP + kv)),
            pl.BlockSpec((None, S, LANES), lambda b, kv, i: (b, 0, P_KB)),
            pl.BlockSpec((None, S, LANES), lambda b, kv, i: (b, 0, P_VB + kv)),
            pl.BlockSpec((S, LANES), lambda b, kv, i: (0, 0)),
        ],
        out_specs=pl.BlockSpec((None, tq, 2 * LANES), lambda b, kv, i: (b, i, kv)),
        out_shape=jax.ShapeDtypeStruct((B, S, BRANCH_WIDTH), BF16),
        scratch_shapes=[pltpu.VMEM((B_GROUP * SWA_BLOCK, LANES), BF16)],
        compiler_params=_cparams(("parallel", "parallel", "arbitrary")),
        name="swa",
    )(slopes2, sinks2, proj, proj, proj, pos_aug)


def _post_kernel(x_ref, g_ref, fg_ref, ya_ref, yb_ref, yc_ref, wg_ref, wgm_ref, wua_ref, wub_ref, wuc_ref,
                 wo_ref, o_ref, *, final):
    x = x_ref[...]
    h = _rms(x, g_ref[...], RMS_EPS).astype(BF16)
    merged = None
    for br, (y_ref, wu_ref) in enumerate(((ya_ref, wua_ref), (yb_ref, wub_ref), (yc_ref, wuc_ref))):
        g = _dot(h, wg_ref[:, br * BRANCH_WIDTH:(br + 1) * BRANCH_WIDTH])
        t = (y_ref[...].astype(F32) * (g * jax.nn.sigmoid(g))).astype(BF16)
        u = _dot(t, wu_ref[...])
        gate = jax.nn.sigmoid(_dot(h, wgm_ref[:, br * D_MODEL:(br + 1) * D_MODEL]))
        merged = gate * u if merged is None else merged + gate * u
    out = x + _dot(merged.astype(BF16), wo_ref[...])
    if final:
        out = _rms(out, fg_ref[...], RMS_EPS)
    o_ref[...] = out


def _post_call(x, norm_gain, final_gain, ya, yb, yc, w_g, w_gm, w_ua, w_ub, w_uc, w_o, layer, final, *, tm=1024):
    B, S, D = x.shape
    kern = functools.partial(_post_kernel, final=final)
    tok = lambda w: pl.BlockSpec((None, tm, w), lambda b, i: (b, i, 0))
    lay = lambda r, c: pl.BlockSpec((None, r, c), lambda b, i: (layer, 0, 0), pipeline_mode=pl.Buffered(1))
    return pl.pallas_call(
        kern,
        grid=(B, S // tm),
        in_specs=[
            tok(D), lay(1, D), pl.BlockSpec((1, D), lambda b, i: (0, 0)),
            tok(BRANCH_WIDTH), tok(BRANCH_WIDTH), tok(BRANCH_WIDTH),
            lay(D, N_BRANCH * BRANCH_WIDTH), lay(D, N_BRANCH * D),
            lay(BRANCH_WIDTH, D), lay(BRANCH_WIDTH, D), lay(BRANCH_WIDTH, D), lay(D, D),
        ],
        out_specs=tok(D),
        out_shape=jax.ShapeDtypeStruct((B, S, D), F32),
        compiler_params=_cparams(("parallel", "parallel")),
        name="post",
    )(x, norm_gain, final_gain, ya, yb, yc, w_g, w_gm, w_ua, w_ub, w_uc, w_o)


def _alibi_slopes(n_heads):
    return 2.0 ** (-8.0 * jnp.arange(1, n_heads + 1, dtype=F32) / n_heads)


def _prepare_weights(w_in):
    col_scale = np.ones((_D_IN,), np.float32)
    for o in (O_QA, O_QB, O_QC):
        col_scale[o:o + 512] = HEAD_DIM ** -0.5 * LOG2E
    wb = (w_in * col_scale).astype(BF16)
    sl = lambda o, n: wb[:, :, o:o + n]
    zeros64 = jnp.zeros(wb.shape[:2] + (HEAD_DIM,), BF16)
    qb_blocks = []
    for n in range(B_HEADS):
        qn = sl(O_QB + n * HEAD_DIM, HEAD_DIM)
        qb_blocks += [qn, zeros64] if n // B_GROUP == 0 else [zeros64, qn]
    v0, v1 = sl(O_VB, HEAD_DIM), sl(O_VB + HEAD_DIM, HEAD_DIM)
    w_pre = jnp.concatenate(
        [sl(O_QA, 512), sl(O_KA, 512), sl(O_VA, 512), sl(O_QC, 512), sl(O_KC, 512), sl(O_VC, 512)]
        + qb_blocks + [sl(O_KB, 128), v0, v0, v1, v1], axis=-1)
    wf = jnp.pad(jnp.tile(sl(O_FC, C_HEADS), (1, 1, N_PIECES)),
                 ((0, 0), (0, 0), (0, LANES - N_PIECES * C_HEADS)))
    w_g = jnp.concatenate([sl(O_GA, 512), sl(O_GB, 512), sl(O_GC, 512)], axis=-1)
    w_gm = sl(O_GM, N_BRANCH * D_MODEL)
    return w_pre, wf, w_g, w_gm


def _placement():
    place = np.zeros((LANES, N_PAIRS * LANES), np.float32)
    for x in range(N_PIECES):
        for h in range(C_HEADS):
            place[C_HEADS * x + h, (h // 2) * LANES + N_PIECES * (h % 2) + x] = 1.0
    return jnp.asarray(place, BF16)


def _alibi_pieces(n_heads, seq):
    slopes = 2.0 ** (-8.0 * np.arange(1, n_heads + 1, dtype=np.float64) / n_heads)
    rest = (slopes[:, None] * (LOG2E * np.arange(seq, dtype=np.float64))[None, :]).astype(np.float32)
    pieces = np.zeros((n_heads, seq, N_PIECES), np.float32)
    for x in range(N_PIECES):
        pieces[:, :, x] = rest.astype(BF16).astype(np.float32)
        rest = rest - pieces[:, :, x]
    return pieces


def _alibi_aug(seq):
    table = np.zeros((A_HEADS, seq, LANES), np.float32)
    table[:, :, 0:N_PIECES] = _alibi_pieces(A_HEADS, seq)
    return jnp.asarray(table, BF16)


def _swa_pos_aug(seq):
    table = np.zeros((seq, LANES), np.float32)
    pieces = _alibi_pieces(B_HEADS, seq)
    for h in range(B_HEADS):
        table[:, N_PIECES * h:N_PIECES * (h + 1)] = pieces[h]
    return jnp.asarray(table, BF16)


def kernel(x, norm_gain, w_in, b_forget, lambda_q1, lambda_k1, lambda_q2, lambda_k2, subln_gain, sinks,
           w_up_a, w_up_b, w_up_c, w_o, final_gain):
    B, S, D = x.shape
    depth = w_in.shape[0]
    w_pre, wf, w_g, w_gm = _prepare_weights(w_in)
    w_ua, w_ub, w_uc, w_ob = (w.astype(BF16) for w in (w_up_a, w_up_b, w_up_c, w_o))
    gain3 = norm_gain.reshape(depth, 1, D)
    fgain = final_gain.reshape(1, D)
    bf3 = jnp.pad(jnp.tile(b_forget.astype(F32), (1, N_PIECES)),
                  ((0, 0), (0, LANES - N_PIECES * C_HEADS))).reshape(depth, 1, LANES)
    place = _placement()
    alibi_aug = _alibi_aug(S)
    slopes2_b = _alibi_slopes(B_HEADS) * LOG2E
    pos_aug = _swa_pos_aug(S)
    pad = lambda v: jnp.pad(v.astype(F32), (0, LANES - v.shape[0]))

    for l in range(depth):
        lam_init = 0.8 - 0.6 * math.exp(-0.3 * l)
        proj = _pre_call(x, gain3, w_pre, wf, bf3, place, l)
        consts = jnp.array([lam_init, 1.0 - lam_init], F32)
        lam_rows = jnp.zeros((8, LANES), F32).at[0:4].set(
            jnp.stack([pad(lambda_q1[l]), pad(lambda_k1[l]), pad(lambda_q2[l]), pad(lambda_k2[l])]))
        ya = _flash_call("diff", proj, P_QA, P_KA, P_VA,
                         (consts, alibi_aug, lam_rows, subln_gain[l].astype(F32).reshape(1, LANES)))
        yc = _flash_call("fox", proj, P_QC, P_KC, P_VC)
        yb = _swa_call(proj, slopes2_b, sinks[l].astype(F32) * LOG2E, pos_aug)
        x = _post_call(x, gain3, fgain, ya, yb, yc, w_g, w_gm, w_ua, w_ub, w_uc, w_ob, l, l == depth - 1)
    return x
```

```python
import functools
import math

import numpy as np
import jax
import jax.numpy as jnp
from jax import lax
from jax.experimental import pallas as pl
from jax.experimental.pallas import tpu as pltpu

F32 = jnp.float32
BF16 = jnp.bfloat16

D_MODEL = 1024
HEAD_DIM = 64
LANES = 128
RMS_EPS = 1e-6
SUBLN_EPS = 1e-5
A_HEADS = 4
B_HEADS = 8
B_KV_HEADS = 2
B_GROUP = B_HEADS // B_KV_HEADS
C_HEADS = 8
WINDOW = 128
SWA_BLOCK = 128
N_BRANCH = 3
BRANCH_WIDTH = 512
N_PAIRS = 4
N_PIECES = 3
LOG2E = math.log2(math.e)

_SPLITS = (512, 512, 512, 512, 512, 128, 128, 512, 512, 512, 512, 8, 512, 3 * D_MODEL)
_OFF = [0]
for _s in _SPLITS:
    _OFF.append(_OFF[-1] + _s)
(O_QA, O_KA, O_VA, O_GA, O_QB, O_KB, O_VB, O_GB, O_QC, O_KC, O_VC, O_FC, O_GC, O_GM, _D_IN) = _OFF

P_QA, P_KA, P_VA = 0, 4, 8
P_QC, P_KC, P_VC = 12, 16, 20
P_QB = 24
P_KB = 32
P_VB = 33
P_MM = 35
P_AUG = 35
P_BLOCKS = 39
P_WIDTH = P_BLOCKS * LANES
MM_WIDTH = P_MM * LANES

VMEM_LIMIT = 56 * 1024 * 1024


def _cparams(sem):
    return pltpu.CompilerParams(dimension_semantics=sem, vmem_limit_bytes=VMEM_LIMIT)


def _rms(x, gain, eps):
    ms = jnp.mean(x * x, axis=-1, keepdims=True)
    return x * lax.rsqrt(ms + eps) * gain


def _dot(a, b):
    return jnp.dot(a, b, preferred_element_type=F32)


def _dot_nt(a, b):
    return lax.dot_general(a, b, (((1,), (1,)), ((), ())), preferred_element_type=F32)


def _split3(v):
    hi = v.astype(BF16)
    r = v - hi.astype(F32)
    mid = r.astype(BF16)
    lo = (r - mid.astype(F32)).astype(BF16)
    return hi, mid, lo


def _pre_kernel(x_ref, g_ref, w_ref, wf_ref, bf_ref, place_ref, proj_ref, carry_ref, *, tm, chunk):
    i = pl.program_id(1)

    @pl.when(i == 0)
    def _():
        carry_ref[...] = jnp.zeros_like(carry_ref)

    h = _rms(x_ref[...], g_ref[...], RMS_EPS).astype(BF16)

    lane = lax.broadcasted_iota(jnp.int32, (tm, LANES), 1)

    def pieces_by_group(v):
        p0, p1, p2 = (p.astype(F32) for p in _split3(v))
        by_group = jnp.where(lane < C_HEADS, p0, jnp.where(lane < 2 * C_HEADS, p1, p2))
        return jnp.where(lane < N_PIECES * C_HEADS, by_group, 0.0).astype(BF16)

    starts = list(range(0, MM_WIDTH, chunk))

    def project(first, last):
        for c0 in starts[first:last]:
            c1 = min(c0 + chunk, MM_WIDTH)
            proj_ref[:, c0:c1] = _dot(h, w_ref[:, c0:c1]).astype(BF16)

    n1, n2, n3 = len(starts) // 4, len(starts) // 2, 3 * len(starts) // 4
    project(0, n1)
    z = _dot(h, wf_ref[...]) + bf_ref[...]
    logf = jnp.minimum(z, 0.0) - jnp.log1p(jnp.exp(-jnp.abs(z)))
    project(n1, n2)
    row = lax.broadcasted_iota(jnp.int32, (tm, tm), 0)
    col = lax.broadcasted_iota(jnp.int32, (tm, tm), 1)
    lower = (col <= row).astype(BF16)
    part = _dot(lower, pieces_by_group(logf))
    total = part + pltpu.roll(part, LANES - C_HEADS, 1) + pltpu.roll(part, LANES - 2 * C_HEADS, 1)
    total = jnp.where(lane < C_HEADS, total, 0.0)
    c = total + pltpu.roll(total, C_HEADS, 1) + pltpu.roll(total, 2 * C_HEADS, 1) + carry_ref[0:1, :]
    carry_ref[...] = jnp.broadcast_to(c[tm - 1:tm, :], carry_ref.shape)
    project(n2, n3)
    proj_ref[:, MM_WIDTH:P_WIDTH] = _dot(pieces_by_group(c * (-LOG2E)), place_ref[...]).astype(BF16)
    project(n3, len(starts))


def _pre_call(x, norm_gain, w_pre, wf, b_f, place, layer, *, tm=512, chunk=512):
    B, S, D = x.shape
    kern = functools.partial(_pre_kernel, tm=tm, chunk=chunk)
    return pl.pallas_call(
        kern,
        grid=(B, S // tm),
        in_specs=[
            pl.BlockSpec((None, tm, D), lambda b, i: (b, i, 0)),
            pl.BlockSpec((None, 1, D), lambda b, i: (layer, 0, 0)),
            pl.BlockSpec((None, D, MM_WIDTH), lambda b, i: (layer, 0, 0)),
            pl.BlockSpec((None, D, LANES), lambda b, i: (layer, 0, 0)),
            pl.BlockSpec((None, 1, LANES), lambda b, i: (layer, 0, 0)),
            pl.BlockSpec((LANES, N_PAIRS * LANES), lambda b, i: (0, 0)),
        ],
        out_specs=pl.BlockSpec((None, tm, P_WIDTH), lambda b, i: (b, i, 0)),
        out_shape=jax.ShapeDtypeStruct((B, S, P_WIDTH), BF16),
        scratch_shapes=[pltpu.VMEM((8, LANES), F32)],
        compiler_params=_cparams(("parallel", "arbitrary")),
        name="pre",
    )(x, norm_gain, w_pre, wf, b_f, place)


def _chunk_order(nq):
    order = [(t, t) for t in range(nq)]
    order += [(qi, c) for qi in range(1, nq) for c in range(qi)]
    return order


def _flash_kernel(*refs, mode, tq, nq, unroll):
    if mode == "diff":
        (consts_ref, tab_ref, q_ref, k_ref, v_ref, aug_ref, lam_ref, sg_ref, o_ref,
         qe_ref, m_ref, acc_ref, sa_ref, sb_ref) = refs
    else:
        (tab_ref, q_ref, k_ref, v_ref, aug_ref, o_ref, qe_ref, m_ref, acc_ref, sa_ref, sb_ref) = refs
    tk = tq
    rows = 2 * tq
    half = tq // 2
    n_chunks = nq * (nq + 1) // 2
    lane = lax.broadcasted_iota(jnp.int32, (half, LANES), 1)

    def build_qe(t, carry):
        for hf in range(2):
            q = q_ref[pl.ds(pl.multiple_of(t * tq + hf * half, half), half), :]
            zero = jnp.zeros_like(q)
            for m in range(2):
                in_map = (lane >= HEAD_DIM) if m else (lane < HEAD_DIM)
                first = N_PIECES * m if mode == "fox" else 0
                ones_at_pieces = jnp.where(lane >= first, 1.0, 0.0) * jnp.where(lane < first + N_PIECES, 1.0, 0.0)
                g0 = (2 * hf + m) * half
                qe_ref[t, g0:g0 + half, 0:LANES] = jnp.where(in_map, q, zero)
                qe_ref[t, g0:g0 + half, LANES:2 * LANES] = ones_at_pieces.astype(BF16)
        return carry

    lax.fori_loop(0, nq, build_qe, 0)

    def keys(start, n):
        return jnp.concatenate([k_ref[pl.ds(start, n), :], aug_ref[pl.ds(start, n), :]], axis=1)

    def values(start, n):
        return jnp.concatenate([v_ref[pl.ds(start, n), :], jnp.ones((n, LANES), BF16)], axis=1)

    def qk(t, s_ref):
        qi = tab_ref[0, t]
        start = pl.multiple_of(tab_ref[1, t] * tk, tk)
        s_ref[...] = _dot_nt(qe_ref[qi], keys(start, tk))

    def qk_diagonal(t, s_ref):
        s_ref[0:tq, 0:half] = _dot_nt(qe_ref[t, 0:tq], keys(t * tk, half))
        s_ref[tq:rows, :] = _dot_nt(qe_ref[t, tq:rows], keys(t * tk, tk))

    def probabilities(s, m_prev):
        cols = [s[:, c0:c0 + LANES] for c0 in range(0, s.shape[1], LANES)]
        mx = cols[0]
        for sc in cols[1:]:
            mx = jnp.maximum(mx, sc)
        m_new = jnp.broadcast_to(jnp.max(mx, axis=1, keepdims=True), (s.shape[0], LANES))
        if m_prev is not None:
            m_new = jnp.maximum(m_prev, m_new)
        return m_new, jnp.concatenate([jnp.exp2(sc - m_new).astype(BF16) for sc in cols], axis=1)

    def softmax_pv(s_ref, t):
        qi = tab_ref[0, t]
        start = pl.multiple_of(tab_ref[1, t] * tk, tk)
        m_prev = m_ref[qi]
        m_new, p = probabilities(s_ref[...], m_prev)
        alpha = jnp.exp2(m_prev - m_new)
        pv = _dot(p, values(start, tk))
        acc_ref[qi, :, 0:LANES] = alpha * acc_ref[qi, :, 0:LANES] + pv[:, 0:LANES]
        acc_ref[qi, :, LANES:2 * LANES] = alpha * acc_ref[qi, :, LANES:2 * LANES] + pv[:, LANES:2 * LANES]
        m_ref[qi] = m_new

    def softmax_pv_diagonal(s_ref, t):
        for r0, n_keys in ((0, half), (tq, tk)):
            r = lax.broadcasted_iota(jnp.int32, (tq, n_keys), 0)
            c = lax.broadcasted_iota(jnp.int32, (tq, n_keys), 1)
            q_pos = jnp.where(r >= half, r - half, r) + (half if r0 else 0)
            s = jnp.where(c <= q_pos, s_ref[r0:r0 + tq, 0:n_keys], -jnp.inf)
            m_new, p = probabilities(s, None)
            acc_ref[t, r0:r0 + tq, :] = _dot(p, values(t * tk, n_keys))
            m_ref[t, r0:r0 + tq, :] = m_new

    s_refs = (sa_ref, sb_ref)

    qk_diagonal(0, sa_ref)
    for t in range(nq):
        if t + 1 < nq:
            qk_diagonal(t + 1, s_refs[(t + 1) % 2])
        else:
            qk(t + 1, s_refs[(t + 1) % 2])
        softmax_pv_diagonal(s_refs[t % 2], t)

    def run(first, last, unroll):
        def body(i, carry):
            for u in range(unroll):
                t = first + i * unroll + u
                qk(t + 1, s_refs[(u + 1) % 2])
                softmax_pv(s_refs[u % 2], t)
            return carry
        lax.fori_loop(0, (last - first) // unroll, body, 0)

    main = nq + (n_chunks - nq) // unroll * unroll
    if main > nq:
        run(nq, main, unroll)
    if n_chunks > main:
        run(main, n_chunks, n_chunks - main)

    def finalize(t):
        for hf in range(2):
            a0 = acc_ref[t, (2 * hf) * half:(2 * hf + 1) * half, :]
            a1 = acc_ref[t, (2 * hf + 1) * half:(2 * hf + 2) * half, :]
            o0 = a0[:, 0:LANES] / a0[:, LANES:2 * LANES]
            o1 = a1[:, 0:LANES] / a1[:, LANES:2 * LANES]
            if mode == "diff":
                lam_init = consts_ref[0]
                lp = lam_ref[...]
                lam = (jnp.exp(jnp.sum(lp[0:1] * lp[1:2], axis=1, keepdims=True))
                       - jnp.exp(jnp.sum(lp[2:3] * lp[3:4], axis=1, keepdims=True)) + lam_init)
                y = _rms(o0 - lam * o1, sg_ref[...], SUBLN_EPS) * consts_ref[1]
            else:
                y = jnp.where(lane < HEAD_DIM, o0, o1)
            o_ref[t * tq + hf * half:t * tq + (hf + 1) * half, :] = y.astype(o_ref.dtype)

    for t in range(nq):
        finalize(t)


def _flash_call(mode, proj, q_blk, k_blk, v_blk, extra=(), *, tq=512, unroll=14):
    B, S, _ = proj.shape
    nq = S // tq
    order = _chunk_order(nq)
    assert nq % 2 == 0 and len(order) % 2 == 0, "the chunk loop handles two chunks per trip"
    table = jnp.asarray(np.array(order + [order[-1]], np.int32).T)
    kern = functools.partial(_flash_kernel, mode=mode, tq=tq, nq=nq, unroll=unroll)
    seq = lambda blk: pl.BlockSpec((None, S, LANES), lambda b, p: (b, 0, blk + p))
    smem = pl.BlockSpec(memory_space=pltpu.SMEM)
    in_specs = [smem, seq(q_blk), seq(k_blk), seq(v_blk)]
    if mode == "diff":
        consts, alibi_aug, lam_rows, subln = extra
        in_specs = [smem] + in_specs + [
            pl.BlockSpec((None, S, LANES), lambda b, p: (p, 0, 0)),
            pl.BlockSpec((8, LANES), lambda b, p: (0, 0)),
            pl.BlockSpec((1, LANES), lambda b, p: (0, 0)),
        ]
        args = (consts, table, proj, proj, proj, alibi_aug, lam_rows, subln)
    else:
        in_specs = in_specs + [seq(P_AUG)]
        args = (table, proj, proj, proj, proj)
    return pl.pallas_call(
        kern,
        grid=(B, N_PAIRS),
        in_specs=in_specs,
        out_specs=pl.BlockSpec((None, S, LANES), lambda b, p: (b, 0, p)),
        out_shape=jax.ShapeDtypeStruct((B, S, BRANCH_WIDTH), BF16),
        scratch_shapes=[
            pltpu.VMEM((nq, 2 * tq, 2 * LANES), BF16),
            pltpu.VMEM((nq, 2 * tq, LANES), F32),
            pltpu.VMEM((nq, 2 * tq, 2 * LANES), F32),
            pltpu.VMEM((2 * tq, tq), F32),
            pltpu.VMEM((2 * tq, tq), F32),
        ],
        compiler_params=_cparams(("parallel", "parallel")),
        name="flash_" + mode,
    )(*args)


def _swa_kernel(slopes_ref, sinks_ref, q_ref, k_ref, v_ref, pos_ref, o_ref, e_ref, *, tq):
    kv = pl.program_id(1)
    qi = pl.program_id(2)
    nb = tq // SWA_BLOCK
    lane = lax.broadcasted_iota(jnp.int32, (SWA_BLOCK, LANES), 1)
    row = lax.broadcasted_iota(jnp.int32, (SWA_BLOCK, LANES), 0)
    r = lax.broadcasted_iota(jnp.int32, (SWA_BLOCK, 2 * SWA_BLOCK), 0)
    c = lax.broadcasted_iota(jnp.int32, (SWA_BLOCK, 2 * SWA_BLOCK), 1)
    for g in range(B_GROUP):
        first = N_PIECES * (kv * B_GROUP + g)
        ones_at_pieces = jnp.where(lane >= first, 1.0, 0.0) * jnp.where(lane < first + N_PIECES, 1.0, 0.0)
        e_ref[g * SWA_BLOCK:(g + 1) * SWA_BLOCK, :] = ones_at_pieces.astype(BF16)
    ones = jnp.ones((2 * SWA_BLOCK, LANES), BF16)
    for n in range(nb):
        q_start = qi * tq + n * SWA_BLOCK
        k_start = pl.multiple_of(jnp.maximum(q_start - SWA_BLOCK, 0), SWA_BLOCK)
        kw = jnp.concatenate([k_ref[pl.ds(k_start, 2 * SWA_BLOCK), :],
                              pos_ref[pl.ds(k_start, 2 * SWA_BLOCK), :]], axis=1)
        vw = jnp.concatenate([v_ref[pl.ds(k_start, 2 * SWA_BLOCK), :], ones], axis=1)
        qs = jnp.concatenate(
            [q_ref[n * SWA_BLOCK:(n + 1) * SWA_BLOCK, g * LANES:(g + 1) * LANES] for g in range(B_GROUP)],
            axis=0)
        s_all = _dot_nt(jnp.concatenate([qs, e_ref[...]], axis=1), kw)
        dist = (q_start - k_start) + r - c
        valid = (dist >= 0) & (dist < WINDOW)
        qpos = (q_start + row).astype(F32)
        ps, tails = [], []
        for g in range(B_GROUP):
            h = kv * B_GROUP + g
            sink = sinks_ref[h] + slopes_ref[h] * qpos
            s = jnp.where(valid, s_all[g * SWA_BLOCK:(g + 1) * SWA_BLOCK], -jnp.inf)
            s0, s1 = s[:, 0:LANES], s[:, LANES:2 * LANES]
            mx = jnp.maximum(jnp.max(jnp.maximum(s0, s1), axis=1, keepdims=True), sink)
            ps.append(jnp.concatenate([jnp.exp2(s0 - mx).astype(BF16), jnp.exp2(s1 - mx).astype(BF16)], axis=1))
            tails.append(jnp.exp2(sink - mx))
        pv = _dot(jnp.concatenate(ps, axis=0), vw)
        outs = []
        for g in range(B_GROUP):
            blk = pv[g * SWA_BLOCK:(g + 1) * SWA_BLOCK]
            outs.append(blk[:, 0:LANES] / (blk[:, LANES:2 * LANES] + tails[g]))
        for h2 in range(B_GROUP // 2):
            o_ref[n * SWA_BLOCK:(n + 1) * SWA_BLOCK, h2 * LANES:(h2 + 1) * LANES] = (
                jnp.where(lane < HEAD_DIM, outs[2 * h2], outs[2 * h2 + 1]).astype(o_ref.dtype))


def _swa_call(proj, slopes2, sinks2, pos_aug, *, tq=512):
    B, S, _ = proj.shape
    kern = functools.partial(_swa_kernel, tq=tq)
    qw = B_GROUP * LANES
    return pl.pallas_call(
        kern,
        grid=(B, B_KV_HEADS, S // tq),
        in_specs=[
            pl.BlockSpec(memory_space=pltpu.SMEM),
            pl.BlockSpec(memory_space=pltpu.SMEM),
            pl.BlockSpec((None, tq, qw), lambda b, kv, i: (b, i, P_QB // B_GROUP + kv)),
            pl.BlockSpec((None, S, LANES), lambda b, kv, i: (b, 0, P_KB)),
            pl.BlockSpec((None, S, LANES), lambda b, kv, i: (b, 0, P_VB + kv)),
            pl.BlockSpec((S, LANES), lambda b, kv, i: (0, 0)),
        ],
        out_specs=pl.BlockSpec((None, tq, 2 * LANES), lambda b, kv, i: (b, i, kv)),
        out_shape=jax.ShapeDtypeStruct((B, S, BRANCH_WIDTH), BF16),
        scratch_shapes=[pltpu.VMEM((B_GROUP * SWA_BLOCK, LANES), BF16)],
        compiler_params=_cparams(("parallel", "parallel", "arbitrary")),
        name="swa",
    )(slopes2, sinks2, proj, proj, proj, pos_aug)


def _post_kernel(x_ref, g_ref, fg_ref, ya_ref, yb_ref, yc_ref, wg_ref, wgm_ref, wua_ref, wub_ref, wuc_ref,
                 wo_ref, o_ref, *, final):
    x = x_ref[...]
    h = _rms(x, g_ref[...], RMS_EPS).astype(BF16)
    hw = D_MODEL // 2
    ts = []
    for br, y_ref in enumerate((ya_ref, yb_ref, yc_ref)):
        g = _dot(h, wg_ref[:, br * BRANCH_WIDTH:(br + 1) * BRANCH_WIDTH])
        ts.append((y_ref[...].astype(F32) * (g * jax.nn.sigmoid(g))).astype(BF16))
    merged = [None, None]
    for br, wu_ref in enumerate((wua_ref, wub_ref, wuc_ref)):
        for j in range(2):
            gate = jax.nn.sigmoid(_dot(h, wgm_ref[:, br * D_MODEL + j * hw:br * D_MODEL + (j + 1) * hw]))
            u = _dot(ts[br], wu_ref[:, j * hw:(j + 1) * hw])
            merged[j] = gate * u if merged[j] is None else merged[j] + gate * u
    mb = jnp.concatenate(merged, axis=1).astype(BF16)
    for j in range(2):
        o_ref[:, j * hw:(j + 1) * hw] = x[:, j * hw:(j + 1) * hw] + _dot(mb, wo_ref[:, j * hw:(j + 1) * hw])
    if final:
        o_ref[...] = _rms(o_ref[...], fg_ref[...], RMS_EPS)


def _post_call(x, norm_gain, final_gain, ya, yb, yc, w_g, w_gm, w_ua, w_ub, w_uc, w_o, layer, final, *, tm=1024):
    B, S, D = x.shape
    kern = functools.partial(_post_kernel, final=final)
    tok = lambda w: pl.BlockSpec((None, tm, w), lambda b, i: (b, i, 0))
    lay = lambda r, c: pl.BlockSpec((None, r, c), lambda b, i: (layer, 0, 0), pipeline_mode=pl.Buffered(1))
    return pl.pallas_call(
        kern,
        grid=(B, S // tm),
        in_specs=[
            tok(D), lay(1, D), pl.BlockSpec((1, D), lambda b, i: (0, 0)),
            tok(BRANCH_WIDTH), tok(BRANCH_WIDTH), tok(BRANCH_WIDTH),
            lay(D, N_BRANCH * BRANCH_WIDTH), lay(D, N_BRANCH * D),
            lay(BRANCH_WIDTH, D), lay(BRANCH_WIDTH, D), lay(BRANCH_WIDTH, D), lay(D, D),
        ],
        out_specs=tok(D),
        out_shape=jax.ShapeDtypeStruct((B, S, D), F32),
        compiler_params=_cparams(("parallel", "parallel")),
        name="post",
    )(x, norm_gain, final_gain, ya, yb, yc, w_g, w_gm, w_ua, w_ub, w_uc, w_o)


def _alibi_slopes(n_heads):
    return 2.0 ** (-8.0 * jnp.arange(1, n_heads + 1, dtype=F32) / n_heads)


def _prepare_weights(w_in):
    col_scale = np.ones((_D_IN,), np.float32)
    for o in (O_QA, O_QB, O_QC):
        col_scale[o:o + 512] = HEAD_DIM ** -0.5 * LOG2E
    wb = (w_in * col_scale).astype(BF16)
    sl = lambda o, n: wb[:, :, o:o + n]
    zeros64 = jnp.zeros(wb.shape[:2] + (HEAD_DIM,), BF16)
    qb_blocks = []
    for n in range(B_HEADS):
        qn = sl(O_QB + n * HEAD_DIM, HEAD_DIM)
        qb_blocks += [qn, zeros64] if n // B_GROUP == 0 else [zeros64, qn]
    v0, v1 = sl(O_VB, HEAD_DIM), sl(O_VB + HEAD_DIM, HEAD_DIM)
    w_pre = jnp.concatenate(
        [sl(O_QA, 512), sl(O_KA, 512), sl(O_VA, 512), sl(O_QC, 512), sl(O_KC, 512), sl(O_VC, 512)]
        + qb_blocks + [sl(O_KB, 128), v0, v0, v1, v1], axis=-1)
    wf = jnp.pad(jnp.tile(sl(O_FC, C_HEADS), (1, 1, N_PIECES)),
                 ((0, 0), (0, 0), (0, LANES - N_PIECES * C_HEADS)))
    w_g = jnp.concatenate([sl(O_GA, 512), sl(O_GB, 512), sl(O_GC, 512)], axis=-1)
    w_gm = sl(O_GM, N_BRANCH * D_MODEL)
    return w_pre, wf, w_g, w_gm


def _placement():
    place = np.zeros((LANES, N_PAIRS * LANES), np.float32)
    for x in range(N_PIECES):
        for h in range(C_HEADS):
            place[C_HEADS * x + h, (h // 2) * LANES + N_PIECES * (h % 2) + x] = 1.0
    return jnp.asarray(place, BF16)


def _alibi_pieces(n_heads, seq):
    slopes = 2.0 ** (-8.0 * np.arange(1, n_heads + 1, dtype=np.float64) / n_heads)
    rest = (slopes[:, None] * (LOG2E * np.arange(seq, dtype=np.float64))[None, :]).astype(np.float32)
    pieces = np.zeros((n_heads, seq, N_PIECES), np.float32)
    for x in range(N_PIECES):
        pieces[:, :, x] = rest.astype(BF16).astype(np.float32)
        rest = rest - pieces[:, :, x]
    return pieces


def _alibi_aug(seq):
    table = np.zeros((A_HEADS, seq, LANES), np.float32)
    table[:, :, 0:N_PIECES] = _alibi_pieces(A_HEADS, seq)
    return jnp.asarray(table, BF16)


def _swa_pos_aug(seq):
    table = np.zeros((seq, LANES), np.float32)
    pieces = _alibi_pieces(B_HEADS, seq)
    for h in range(B_HEADS):
        table[:, N_PIECES * h:N_PIECES * (h + 1)] = pieces[h]
    return jnp.asarray(table, BF16)


def kernel(x, norm_gain, w_in, b_forget, lambda_q1, lambda_k1, lambda_q2, lambda_k2, subln_gain, sinks,
           w_up_a, w_up_b, w_up_c, w_o, final_gain):
    B, S, D = x.shape
    depth = w_in.shape[0]
    w_pre, wf, w_g, w_gm = _prepare_weights(w_in)
    w_ua, w_ub, w_uc, w_ob = (w.astype(BF16) for w in (w_up_a, w_up_b, w_up_c, w_o))
    gain3 = norm_gain.reshape(depth, 1, D)
    fgain = final_gain.reshape(1, D)
    bf3 = jnp.pad(jnp.tile(b_forget.astype(F32), (1, N_PIECES)),
                  ((0, 0), (0, LANES - N_PIECES * C_HEADS))).reshape(depth, 1, LANES)
    place = _placement()
    alibi_aug = _alibi_aug(S)
    slopes2_b = _alibi_slopes(B_HEADS) * LOG2E
    pos_aug = _swa_pos_aug(S)
    pad = lambda v: jnp.pad(v.astype(F32), (0, LANES - v.shape[0]))

    for l in range(depth):
        lam_init = 0.8 - 0.6 * math.exp(-0.3 * l)
        proj = _pre_call(x, gain3, w_pre, wf, bf3, place, l)
        consts = jnp.array([lam_init, 1.0 - lam_init], F32)
        lam_rows = jnp.zeros((8, LANES), F32).at[0:4].set(
            jnp.stack([pad(lambda_q1[l]), pad(lambda_k1[l]), pad(lambda_q2[l]), pad(lambda_k2[l])]))
        ya = _flash_call("diff", proj, P_QA, P_KA, P_VA,
                         (consts, alibi_aug, lam_rows, subln_gain[l].astype(F32).reshape(1, LANES)))
        yc = _flash_call("fox", proj, P_QC, P_KC, P_VC)
        yb = _swa_call(proj, slopes2_b, sinks[l].astype(F32) * LOG2E, pos_aug)
        x = _post_call(x, gain3, fgain, ya, yb, yc, w_g, w_gm, w_ua, w_ub, w_uc, w_ob, l, l == depth - 1)
    return x
```

```python
import functools
import math

import numpy as np
import jax
import jax.numpy as jnp
from jax import lax
from jax.experimental import pallas as pl
from jax.experimental.pallas import tpu as pltpu

F32 = jnp.float32
BF16 = jnp.bfloat16

D_MODEL = 1024
HEAD_DIM = 64
LANES = 128
RMS_EPS = 1e-6
SUBLN_EPS = 1e-5
A_HEADS = 4
B_HEADS = 8
B_KV_HEADS = 2
B_GROUP = B_HEADS // B_KV_HEADS
C_HEADS = 8
WINDOW = 128
SWA_BLOCK = 128
N_BRANCH = 3
BRANCH_WIDTH = 512
N_PAIRS = 4
N_PIECES = 3
LOG2E = math.log2(math.e)

_SPLITS = (512, 512, 512, 512, 512, 128, 128, 512, 512, 512, 512, 8, 512, 3 * D_MODEL)
_OFF = [0]
for _s in _SPLITS:
    _OFF.append(_OFF[-1] + _s)
(O_QA, O_KA, O_VA, O_GA, O_QB, O_KB, O_VB, O_GB, O_QC, O_KC, O_VC, O_FC, O_GC, O_GM, _D_IN) = _OFF

P_QA, P_KA, P_VA = 0, 4, 8
P_QC, P_KC, P_VC = 12, 16, 20
P_QB = 24
P_KB = 32
P_VB = 33
P_MM = 35
P_AUG = 35
P_BLOCKS = 39
P_WIDTH = P_BLOCKS * LANES
MM_WIDTH = P_MM * LANES

VMEM_LIMIT = 56 * 1024 * 1024


def _cparams(sem):
    return pltpu.CompilerParams(dimension_semantics=sem, vmem_limit_bytes=VMEM_LIMIT)


def _rms(x, gain, eps):
    ms = jnp.mean(x * x, axis=-1, keepdims=True)
    return x * lax.rsqrt(ms + eps) * gain


def _dot(a, b):
    return jnp.dot(a, b, preferred_element_type=F32)


def _dot_nt(a, b):
    return lax.dot_general(a, b, (((1,), (1,)), ((), ())), preferred_element_type=F32)


def _split3(v):
    hi = v.astype(BF16)
    r = v - hi.astype(F32)
    mid = r.astype(BF16)
    lo = (r - mid.astype(F32)).astype(BF16)
    return hi, mid, lo


def _pre_kernel(x_ref, g_ref, w_ref, wf_ref, bf_ref, place_ref, proj_ref, carry_ref, *, tm, chunk):
    i = pl.program_id(1)

    @pl.when(i == 0)
    def _():
        carry_ref[...] = jnp.zeros_like(carry_ref)

    h = _rms(x_ref[...], g_ref[...], RMS_EPS).astype(BF16)

    lane = lax.broadcasted_iota(jnp.int32, (tm, LANES), 1)

    def pieces_by_group(v):
        p0, p1, p2 = (p.astype(F32) for p in _split3(v))
        by_group = jnp.where(lane < C_HEADS, p0, jnp.where(lane < 2 * C_HEADS, p1, p2))
        return jnp.where(lane < N_PIECES * C_HEADS, by_group, 0.0).astype(BF16)

    starts = list(range(0, MM_WIDTH, chunk))

    def project(first, last):
        for c0 in starts[first:last]:
            c1 = min(c0 + chunk, MM_WIDTH)
            proj_ref[:, c0:c1] = _dot(h, w_ref[:, c0:c1]).astype(BF16)

    n1, n2, n3 = len(starts) // 4, len(starts) // 2, 3 * len(starts) // 4
    project(0, n1)
    z = _dot(h, wf_ref[...]) + bf_ref[...]
    logf = jnp.minimum(z, 0.0) - jnp.log1p(jnp.exp(-jnp.abs(z)))
    project(n1, n2)
    row = lax.broadcasted_iota(jnp.int32, (tm, tm), 0)
    col = lax.broadcasted_iota(jnp.int32, (tm, tm), 1)
    lower = (col <= row).astype(BF16)
    part = _dot(lower, pieces_by_group(logf))
    total = part + pltpu.roll(part, LANES - C_HEADS, 1) + pltpu.roll(part, LANES - 2 * C_HEADS, 1)
    total = jnp.where(lane < C_HEADS, total, 0.0)
    c = total + pltpu.roll(total, C_HEADS, 1) + pltpu.roll(total, 2 * C_HEADS, 1) + carry_ref[0:1, :]
    carry_ref[...] = jnp.broadcast_to(c[tm - 1:tm, :], carry_ref.shape)
    project(n2, n3)
    proj_ref[:, MM_WIDTH:P_WIDTH] = _dot(pieces_by_group(c * (-LOG2E)), place_ref[...]).astype(BF16)
    project(n3, len(starts))


def _pre_call(x, norm_gain, w_pre, wf, b_f, place, layer, *, tm=512, chunk=512):
    B, S, D = x.shape
    kern = functools.partial(_pre_kernel, tm=tm, chunk=chunk)
    return pl.pallas_call(
        kern,
        grid=(B, S // tm),
        in_specs=[
            pl.BlockSpec((None, tm, D), lambda b, i: (b, i, 0)),
            pl.BlockSpec((None, 1, D), lambda b, i: (layer, 0, 0)),
            pl.BlockSpec((None, D, MM_WIDTH), lambda b, i: (layer, 0, 0)),
            pl.BlockSpec((None, D, LANES), lambda b, i: (layer, 0, 0)),
            pl.BlockSpec((None, 1, LANES), lambda b, i: (layer, 0, 0)),
            pl.BlockSpec((LANES, N_PAIRS * LANES), lambda b, i: (0, 0)),
        ],
        out_specs=pl.BlockSpec((None, tm, P_WIDTH), lambda b, i: (b, i, 0)),
        out_shape=jax.ShapeDtypeStruct((B, S, P_WIDTH), BF16),
        scratch_shapes=[pltpu.VMEM((8, LANES), F32)],
        compiler_params=_cparams(("parallel", "arbitrary")),
        name="pre",
    )(x, norm_gain, w_pre, wf, b_f, place)


def _chunk_order(nq):
    order = [(t, t) for t in range(nq)]
    order += [(qi, c) for qi in range(1, nq) for c in range(qi)]
    return order


def _flash_kernel(*refs, mode, tq, nq, unroll):
    if mode == "diff":
        (consts_ref, tab_ref, q_ref, k_ref, v_ref, aug_ref, lam_ref, sg_ref, o_ref,
         qe_ref, m_ref, acc_ref, sa_ref, sb_ref) = refs
    else:
        (tab_ref, q_ref, k_ref, v_ref, aug_ref, o_ref, qe_ref, m_ref, acc_ref, sa_ref, sb_ref) = refs
    tk = tq
    rows = 2 * tq
    half = tq // 2
    n_chunks = nq * (nq + 1) // 2
    lane = lax.broadcasted_iota(jnp.int32, (half, LANES), 1)

    def build_qe(t, carry):
        for hf in range(2):
            q = q_ref[pl.ds(pl.multiple_of(t * tq + hf * half, half), half), :]
            zero = jnp.zeros_like(q)
            for m in range(2):
                in_map = (lane >= HEAD_DIM) if m else (lane < HEAD_DIM)
                first = N_PIECES * m if mode == "fox" else 0
                ones_at_pieces = jnp.where(lane >= first, 1.0, 0.0) * jnp.where(lane < first + N_PIECES, 1.0, 0.0)
                g0 = (2 * hf + m) * half
                qe_ref[t, g0:g0 + half, 0:LANES] = jnp.where(in_map, q, zero)
                qe_ref[t, g0:g0 + half, LANES:2 * LANES] = ones_at_pieces.astype(BF16)
        return carry

    lax.fori_loop(0, nq, build_qe, 0)

    def keys(start, n):
        return jnp.concatenate([k_ref[pl.ds(start, n), :], aug_ref[pl.ds(start, n), :]], axis=1)

    def values(start, n):
        return jnp.concatenate([v_ref[pl.ds(start, n), :], jnp.ones((n, LANES), BF16)], axis=1)

    def qk(t, s_ref):
        qi = tab_ref[0, t]
        start = pl.multiple_of(tab_ref[1, t] * tk, tk)
        s_ref[...] = _dot_nt(qe_ref[qi], keys(start, tk))

    def qk_diagonal(t, s_ref):
        s_ref[0:tq, 0:half] = _dot_nt(qe_ref[t, 0:tq], keys(t * tk, half))
        s_ref[tq:rows, :] = _dot_nt(qe_ref[t, tq:rows], keys(t * tk, tk))

    def probabilities(s, m_prev):
        cols = [s[:, c0:c0 + LANES] for c0 in range(0, s.shape[1], LANES)]
        mx = cols[0]
        for sc in cols[1:]:
            mx = jnp.maximum(mx, sc)
        m_new = jnp.broadcast_to(jnp.max(mx, axis=1, keepdims=True), (s.shape[0], LANES))
        if m_prev is not None:
            m_new = jnp.maximum(m_prev, m_new)
        return m_new, jnp.concatenate([jnp.exp2(sc - m_new).astype(BF16) for sc in cols], axis=1)

    def softmax_pv(s_ref, t):
        qi = tab_ref[0, t]
        start = pl.multiple_of(tab_ref[1, t] * tk, tk)
        m_prev = m_ref[qi]
        m_new, p = probabilities(s_ref[...], m_prev)
        alpha = jnp.exp2(m_prev - m_new)
        pv = _dot(p, values(start, tk))
        acc_ref[qi, :, 0:LANES] = alpha * acc_ref[qi, :, 0:LANES] + pv[:, 0:LANES]
        acc_ref[qi, :, LANES:2 * LANES] = alpha * acc_ref[qi, :, LANES:2 * LANES] + pv[:, LANES:2 * LANES]
        m_ref[qi] = m_new

    def softmax_pv_diagonal(s_ref, t):
        for r0, n_keys in ((0, half), (tq, tk)):
            r = lax.broadcasted_iota(jnp.int32, (tq, n_keys), 0)
            c = lax.broadcasted_iota(jnp.int32, (tq, n_keys), 1)
            q_pos = jnp.where(r >= half, r - half, r) + (half if r0 else 0)
            s = jnp.where(c <= q_pos, s_ref[r0:r0 + tq, 0:n_keys], -jnp.inf)
            m_new, p = probabilities(s, None)
            acc_ref[t, r0:r0 + tq, :] = _dot(p, values(t * tk, n_keys))
            m_ref[t, r0:r0 + tq, :] = m_new

    s_refs = (sa_ref, sb_ref)

    qk_diagonal(0, sa_ref)
    for t in range(nq):
        if t + 1 < nq:
            qk_diagonal(t + 1, s_refs[(t + 1) % 2])
        else:
            qk(t + 1, s_refs[(t + 1) % 2])
        softmax_pv_diagonal(s_refs[t % 2], t)

    def run(first, last, unroll):
        def body(i, carry):
            for u in range(unroll):
                t = first + i * unroll + u
                qk(t + 1, s_refs[(u + 1) % 2])
                softmax_pv(s_refs[u % 2], t)
            return carry
        lax.fori_loop(0, (last - first) // unroll, body, 0)

    main = nq + (n_chunks - nq) // unroll * unroll
    if main > nq:
        run(nq, main, unroll)
    if n_chunks > main:
        run(main, n_chunks, n_chunks - main)

    def finalize(t):
        for hf in range(2):
            a0 = acc_ref[t, (2 * hf) * half:(2 * hf + 1) * half, :]
            a1 = acc_ref[t, (2 * hf + 1) * half:(2 * hf + 2) * half, :]
            o0 = a0[:, 0:LANES] / a0[:, LANES:2 * LANES]
            o1 = a1[:, 0:LANES] / a1[:, LANES:2 * LANES]
            if mode == "diff":
                lam_init = consts_ref[0]
                lp = lam_ref[...]
                lam = (jnp.exp(jnp.sum(lp[0:1] * lp[1:2], axis=1, keepdims=True))
                       - jnp.exp(jnp.sum(lp[2:3] * lp[3:4], axis=1, keepdims=True)) + lam_init)
                y = _rms(o0 - lam * o1, sg_ref[...], SUBLN_EPS) * consts_ref[1]
            else:
                y = jnp.where(lane < HEAD_DIM, o0, o1)
            o_ref[t * tq + hf * half:t * tq + (hf + 1) * half, :] = y.astype(o_ref.dtype)

    for t in range(nq):
        finalize(t)


def _flash_call(mode, proj, q_blk, k_blk, v_blk, extra=(), *, tq=512, unroll=14):
    B, S, _ = proj.shape
    nq = S // tq
    order = _chunk_order(nq)
    assert nq % 2 == 0 and len(order) % 2 == 0, "the chunk loop handles two chunks per trip"
    table = jnp.asarray(np.array(order + [order[-1]], np.int32).T)
    kern = functools.partial(_flash_kernel, mode=mode, tq=tq, nq=nq, unroll=unroll)
    seq = lambda blk: pl.BlockSpec((None, S, LANES), lambda b, p: (b, 0, blk + p))
    smem = pl.BlockSpec(memory_space=pltpu.SMEM)
    in_specs = [smem, seq(q_blk), seq(k_blk), seq(v_blk)]
    if mode == "diff":
        consts, alibi_aug, lam_rows, subln = extra
        in_specs = [smem] + in_specs + [
            pl.BlockSpec((None, S, LANES), lambda b, p: (p, 0, 0)),
            pl.BlockSpec((8, LANES), lambda b, p: (0, 0)),
            pl.BlockSpec((1, LANES), lambda b, p: (0, 0)),
        ]
        args = (consts, table, proj, proj, proj, alibi_aug, lam_rows, subln)
    else:
        in_specs = in_specs + [seq(P_AUG)]
        args = (table, proj, proj, proj, proj)
    return pl.pallas_call(
        kern,
        grid=(B, N_PAIRS),
        in_specs=in_specs,
        out_specs=pl.BlockSpec((None, S, LANES), lambda b, p: (b, 0, p)),
        out_shape=jax.ShapeDtypeStruct((B, S, BRANCH_WIDTH), BF16),
        scratch_shapes=[
            pltpu.VMEM((nq, 2 * tq, 2 * LANES), BF16),
            pltpu.VMEM((nq, 2 * tq, LANES), F32),
            pltpu.VMEM((nq, 2 * tq, 2 * LANES), F32),
            pltpu.VMEM((2 * tq, tq), F32),
            pltpu.VMEM((2 * tq, tq), F32),
        ],
        compiler_params=_cparams(("parallel", "parallel")),
        name="flash_" + mode,
    )(*args)


def _swa_kernel(slopes_ref, sinks_ref, q_ref, k_ref, v_ref, pos_ref, o_ref, e_ref, *, seq, unroll):
    kv = pl.program_id(1)
    lane = lax.broadcasted_iota(jnp.int32, (SWA_BLOCK, LANES), 1)
    row = lax.broadcasted_iota(jnp.int32, (SWA_BLOCK, LANES), 0)
    r = lax.broadcasted_iota(jnp.int32, (SWA_BLOCK, 2 * SWA_BLOCK), 0)
    c = lax.broadcasted_iota(jnp.int32, (SWA_BLOCK, 2 * SWA_BLOCK), 1)
    for g in range(B_GROUP):
        first = N_PIECES * (kv * B_GROUP + g)
        ones_at_pieces = jnp.where(lane >= first, 1.0, 0.0) * jnp.where(lane < first + N_PIECES, 1.0, 0.0)
        e_ref[g * SWA_BLOCK:(g + 1) * SWA_BLOCK, :] = ones_at_pieces.astype(BF16)
    ones = jnp.ones((2 * SWA_BLOCK, LANES), BF16)

    def block(q_start):
        k_start = pl.multiple_of(jnp.maximum(q_start - SWA_BLOCK, 0), SWA_BLOCK)
        kw = jnp.concatenate([k_ref[pl.ds(k_start, 2 * SWA_BLOCK), :],
                              pos_ref[pl.ds(k_start, 2 * SWA_BLOCK), :]], axis=1)
        vw = jnp.concatenate([v_ref[pl.ds(k_start, 2 * SWA_BLOCK), :], ones], axis=1)
        qs = jnp.concatenate(
            [q_ref[pl.ds(q_start, SWA_BLOCK), g * LANES:(g + 1) * LANES] for g in range(B_GROUP)],
            axis=0)
        s_all = _dot_nt(jnp.concatenate([qs, e_ref[...]], axis=1), kw)
        dist = (q_start - k_start) + r - c
        valid = (dist >= 0) & (dist < WINDOW)
        qpos = (q_start + row).astype(F32)
        ps, tails = [], []
        for g in range(B_GROUP):
            h = kv * B_GROUP + g
            sink = sinks_ref[h] + slopes_ref[h] * qpos
            s = jnp.where(valid, s_all[g * SWA_BLOCK:(g + 1) * SWA_BLOCK], -jnp.inf)
            s0, s1 = s[:, 0:LANES], s[:, LANES:2 * LANES]
            mx = jnp.maximum(jnp.max(jnp.maximum(s0, s1), axis=1, keepdims=True), sink)
            ps.append(jnp.concatenate([jnp.exp2(s0 - mx).astype(BF16), jnp.exp2(s1 - mx).astype(BF16)], axis=1))
            tails.append(jnp.exp2(sink - mx))
        pv = _dot(jnp.concatenate(ps, axis=0), vw)
        outs = []
        for g in range(B_GROUP):
            blk = pv[g * SWA_BLOCK:(g + 1) * SWA_BLOCK]
            outs.append(blk[:, 0:LANES] / (blk[:, LANES:2 * LANES] + tails[g]))
        for h2 in range(B_GROUP // 2):
            o_ref[pl.ds(q_start, SWA_BLOCK), h2 * LANES:(h2 + 1) * LANES] = (
                jnp.where(lane < HEAD_DIM, outs[2 * h2], outs[2 * h2 + 1]).astype(o_ref.dtype))

    def body(i, carry):
        for n in range(unroll):
            block(pl.multiple_of((i * unroll + n) * SWA_BLOCK, SWA_BLOCK))
        return carry

    lax.fori_loop(0, seq // (SWA_BLOCK * unroll), body, 0)


def _swa_call(proj, slopes2, sinks2, pos_aug, *, unroll=4):
    B, S, _ = proj.shape
    kern = functools.partial(_swa_kernel, seq=S, unroll=unroll)
    qw = B_GROUP * LANES
    return pl.pallas_call(
        kern,
        grid=(B, B_KV_HEADS),
        in_specs=[
            pl.BlockSpec(memory_space=pltpu.SMEM),
            pl.BlockSpec(memory_space=pltpu.SMEM),
            pl.BlockSpec((None, S, qw), lambda b, kv: (b, 0, P_QB // B_GROUP + kv)),
            pl.BlockSpec((None, S, LANES), lambda b, kv: (b, 0, P_KB)),
            pl.BlockSpec((None, S, LANES), lambda b, kv: (b, 0, P_VB + kv)),
            pl.BlockSpec((S, LANES), lambda b, kv: (0, 0)),
        ],
        out_specs=pl.BlockSpec((None, S, 2 * LANES), lambda b, kv: (b, 0, kv)),
        out_shape=jax.ShapeDtypeStruct((B, S, BRANCH_WIDTH), BF16),
        scratch_shapes=[pltpu.VMEM((B_GROUP * SWA_BLOCK, LANES), BF16)],
        compiler_params=_cparams(("parallel", "parallel")),
        name="swa",
    )(slopes2, sinks2, proj, proj, proj, pos_aug)


def _post_kernel(x_ref, g_ref, fg_ref, ya_ref, yb_ref, yc_ref, wg_ref, wgm_ref, wua_ref, wub_ref, wuc_ref,
                 wo_ref, o_ref, *, final):
    x = x_ref[...]
    h = _rms(x, g_ref[...], RMS_EPS).astype(BF16)
    hw = D_MODEL // 2
    ts = []
    for br, y_ref in enumerate((ya_ref, yb_ref, yc_ref)):
        g = _dot(h, wg_ref[:, br * BRANCH_WIDTH:(br + 1) * BRANCH_WIDTH])
        ts.append((y_ref[...].astype(F32) * (g * jax.nn.sigmoid(g))).astype(BF16))
    merged = [None, None]
    for br, wu_ref in enumerate((wua_ref, wub_ref, wuc_ref)):
        for j in range(2):
            gate = jax.nn.sigmoid(_dot(h, wgm_ref[:, br * D_MODEL + j * hw:br * D_MODEL + (j + 1) * hw]))
            u = _dot(ts[br], wu_ref[:, j * hw:(j + 1) * hw])
            merged[j] = gate * u if merged[j] is None else merged[j] + gate * u
    mb = jnp.concatenate(merged, axis=1).astype(BF16)
    for j in range(2):
        o_ref[:, j * hw:(j + 1) * hw] = x[:, j * hw:(j + 1) * hw] + _dot(mb, wo_ref[:, j * hw:(j + 1) * hw])
    if final:
        o_ref[...] = _rms(o_ref[...], fg_ref[...], RMS_EPS)


def _post_call(x, norm_gain, final_gain, ya, yb, yc, w_g, w_gm, w_ua, w_ub, w_uc, w_o, layer, final, *, tm=1024):
    B, S, D = x.shape
    kern = functools.partial(_post_kernel, final=final)
    tok = lambda w: pl.BlockSpec((None, tm, w), lambda b, i: (b, i, 0))
    lay = lambda r, c: pl.BlockSpec((None, r, c), lambda b, i: (layer, 0, 0), pipeline_mode=pl.Buffered(1))
    return pl.pallas_call(
        kern,
        grid=(B, S // tm),
        in_specs=[
            tok(D), lay(1, D), pl.BlockSpec((1, D), lambda b, i: (0, 0)),
            tok(BRANCH_WIDTH), tok(BRANCH_WIDTH), tok(BRANCH_WIDTH),
            lay(D, N_BRANCH * BRANCH_WIDTH), lay(D, N_BRANCH * D),
            lay(BRANCH_WIDTH, D), lay(BRANCH_WIDTH, D), lay(BRANCH_WIDTH, D), lay(D, D),
        ],
        out_specs=tok(D),
        out_shape=jax.ShapeDtypeStruct((B, S, D), F32),
        compiler_params=_cparams(("parallel", "parallel")),
        name="post",
    )(x, norm_gain, final_gain, ya, yb, yc, w_g, w_gm, w_ua, w_ub, w_uc, w_o)


def _alibi_slopes(n_heads):
    return 2.0 ** (-8.0 * jnp.arange(1, n_heads + 1, dtype=F32) / n_heads)


def _prepare_weights(w_in):
    col_scale = np.ones((_D_IN,), np.float32)
    for o in (O_QA, O_QB, O_QC):
        col_scale[o:o + 512] = HEAD_DIM ** -0.5 * LOG2E
    wb = (w_in * col_scale).astype(BF16)
    sl = lambda o, n: wb[:, :, o:o + n]
    zeros64 = jnp.zeros(wb.shape[:2] + (HEAD_DIM,), BF16)
    qb_blocks = []
    for n in range(B_HEADS):
        qn = sl(O_QB + n * HEAD_DIM, HEAD_DIM)
        qb_blocks += [qn, zeros64] if n // B_GROUP == 0 else [zeros64, qn]
    v0, v1 = sl(O_VB, HEAD_DIM), sl(O_VB + HEAD_DIM, HEAD_DIM)
    w_pre = jnp.concatenate(
        [sl(O_QA, 512), sl(O_KA, 512), sl(O_VA, 512), sl(O_QC, 512), sl(O_KC, 512), sl(O_VC, 512)]
        + qb_blocks + [sl(O_KB, 128), v0, v0, v1, v1], axis=-1)
    wf = jnp.pad(jnp.tile(sl(O_FC, C_HEADS), (1, 1, N_PIECES)),
                 ((0, 0), (0, 0), (0, LANES - N_PIECES * C_HEADS)))
    w_g = jnp.concatenate([sl(O_GA, 512), sl(O_GB, 512), sl(O_GC, 512)], axis=-1)
    w_gm = sl(O_GM, N_BRANCH * D_MODEL)
    return w_pre, wf, w_g, w_gm


def _placement():
    place = np.zeros((LANES, N_PAIRS * LANES), np.float32)
    for x in range(N_PIECES):
        for h in range(C_HEADS):
            place[C_HEADS * x + h, (h // 2) * LANES + N_PIECES * (h % 2) + x] = 1.0
    return jnp.asarray(place, BF16)


def _alibi_pieces(n_heads, seq):
    slopes = 2.0 ** (-8.0 * np.arange(1, n_heads + 1, dtype=np.float64) / n_heads)
    rest = (slopes[:, None] * (LOG2E * np.arange(seq, dtype=np.float64))[None, :]).astype(np.float32)
    pieces = np.zeros((n_heads, seq, N_PIECES), np.float32)
    for x in range(N_PIECES):
        pieces[:, :, x] = rest.astype(BF16).astype(np.float32)
        rest = rest - pieces[:, :, x]
    return pieces


def _alibi_aug(seq):
    table = np.zeros((A_HEADS, seq, LANES), np.float32)
    table[:, :, 0:N_PIECES] = _alibi_pieces(A_HEADS, seq)
    return jnp.asarray(table, BF16)


def _swa_pos_aug(seq):
    table = np.zeros((seq, LANES), np.float32)
    pieces = _alibi_pieces(B_HEADS, seq)
    for h in range(B_HEADS):
        table[:, N_PIECES * h:N_PIECES * (h + 1)] = pieces[h]
    return jnp.asarray(table, BF16)


def kernel(x, norm_gain, w_in, b_forget, lambda_q1, lambda_k1, lambda_q2, lambda_k2, subln_gain, sinks,
           w_up_a, w_up_b, w_up_c, w_o, final_gain):
    B, S, D = x.shape
    depth = w_in.shape[0]
    w_pre, wf, w_g, w_gm = _prepare_weights(w_in)
    w_ua, w_ub, w_uc, w_ob = (w.astype(BF16) for w in (w_up_a, w_up_b, w_up_c, w_o))
    gain3 = norm_gain.reshape(depth, 1, D)
    fgain = final_gain.reshape(1, D)
    bf3 = jnp.pad(jnp.tile(b_forget.astype(F32), (1, N_PIECES)),
                  ((0, 0), (0, LANES - N_PIECES * C_HEADS))).reshape(depth, 1, LANES)
    place = _placement()
    alibi_aug = _alibi_aug(S)
    slopes2_b = _alibi_slopes(B_HEADS) * LOG2E
    pos_aug = _swa_pos_aug(S)
    pad = lambda v: jnp.pad(v.astype(F32), (0, LANES - v.shape[0]))

    for l in range(depth):
        lam_init = 0.8 - 0.6 * math.exp(-0.3 * l)
        proj = _pre_call(x, gain3, w_pre, wf, bf3, place, l)
        consts = jnp.array([lam_init, 1.0 - lam_init], F32)
        lam_rows = jnp.zeros((8, LANES), F32).at[0:4].set(
            jnp.stack([pad(lambda_q1[l]), pad(lambda_k1[l]), pad(lambda_q2[l]), pad(lambda_k2[l])]))
        ya = _flash_call("diff", proj, P_QA, P_KA, P_VA,
                         (consts, alibi_aug, lam_rows, subln_gain[l].astype(F32).reshape(1, LANES)))
        yc = _flash_call("fox", proj, P_QC, P_KC, P_VC)
        yb = _swa_call(proj, slopes2_b, sinks[l].astype(F32) * LOG2E, pos_aug)
        x = _post_call(x, gain3, fgain, ya, yb, yc, w_g, w_gm, w_ua, w_ub, w_uc, w_ob, l, l == depth - 1)
    return x
```

```python
import functools
import math

import numpy as np
import jax
import jax.numpy as jnp
from jax import lax
from jax.experimental import pallas as pl
from jax.experimental.pallas import tpu as pltpu

F32 = jnp.float32
BF16 = jnp.bfloat16

D_MODEL = 1024
HEAD_DIM = 64
LANES = 128
RMS_EPS = 1e-6
SUBLN_EPS = 1e-5
A_HEADS = 4
B_HEADS = 8
B_KV_HEADS = 2
B_GROUP = B_HEADS // B_KV_HEADS
C_HEADS = 8
WINDOW = 128
SWA_BLOCK = 128
N_BRANCH = 3
BRANCH_WIDTH = 512
N_PAIRS = 4
N_PIECES = 3
LOG2E = math.log2(math.e)

_SPLITS = (512, 512, 512, 512, 512, 128, 128, 512, 512, 512, 512, 8, 512, 3 * D_MODEL)
_OFF = [0]
for _s in _SPLITS:
    _OFF.append(_OFF[-1] + _s)
(O_QA, O_KA, O_VA, O_GA, O_QB, O_KB, O_VB, O_GB, O_QC, O_KC, O_VC, O_FC, O_GC, O_GM, _D_IN) = _OFF

P_QA, P_KA, P_VA = 0, 4, 8
P_QC, P_KC, P_VC = 12, 16, 20
P_QB = 24
P_KB = 32
P_VB = 33
P_MM = 35
P_AUG = 35
P_BLOCKS = 39
P_WIDTH = P_BLOCKS * LANES
MM_WIDTH = P_MM * LANES

VMEM_LIMIT = 56 * 1024 * 1024


def _cparams(sem):
    return pltpu.CompilerParams(dimension_semantics=sem, vmem_limit_bytes=VMEM_LIMIT)


def _rms(x, gain, eps):
    ms = jnp.mean(x * x, axis=-1, keepdims=True)
    return x * lax.rsqrt(ms + eps) * gain


def _dot(a, b):
    return jnp.dot(a, b, preferred_element_type=F32)


def _dot_nt(a, b):
    return lax.dot_general(a, b, (((1,), (1,)), ((), ())), preferred_element_type=F32)


def _split3(v):
    hi = v.astype(BF16)
    r = v - hi.astype(F32)
    mid = r.astype(BF16)
    lo = (r - mid.astype(F32)).astype(BF16)
    return hi, mid, lo


def _pre_kernel(x_ref, g_ref, w_ref, wf_ref, bf_ref, place_ref, proj_ref, carry_ref, *, tm, chunk):
    i = pl.program_id(1)

    @pl.when(i == 0)
    def _():
        carry_ref[...] = jnp.zeros_like(carry_ref)

    h = _rms(x_ref[...], g_ref[...], RMS_EPS).astype(BF16)

    lane = lax.broadcasted_iota(jnp.int32, (tm, LANES), 1)

    def pieces_by_group(v):
        p0, p1, p2 = (p.astype(F32) for p in _split3(v))
        by_group = jnp.where(lane < C_HEADS, p0, jnp.where(lane < 2 * C_HEADS, p1, p2))
        return jnp.where(lane < N_PIECES * C_HEADS, by_group, 0.0).astype(BF16)

    starts = list(range(0, MM_WIDTH, chunk))

    def project(first, last):
        for c0 in starts[first:last]:
            c1 = min(c0 + chunk, MM_WIDTH)
            proj_ref[:, c0:c1] = _dot(h, w_ref[:, c0:c1]).astype(BF16)

    n1, n2, n3 = len(starts) // 4, len(starts) // 2, 3 * len(starts) // 4
    project(0, n1)
    z = _dot(h, wf_ref[...]) + bf_ref[...]
    logf = jnp.minimum(z, 0.0) - jnp.log1p(jnp.exp(-jnp.abs(z)))
    project(n1, n2)
    row = lax.broadcasted_iota(jnp.int32, (tm, tm), 0)
    col = lax.broadcasted_iota(jnp.int32, (tm, tm), 1)
    lower = (col <= row).astype(BF16)
    part = _dot(lower, pieces_by_group(logf))
    total = part + pltpu.roll(part, LANES - C_HEADS, 1) + pltpu.roll(part, LANES - 2 * C_HEADS, 1)
    total = jnp.where(lane < C_HEADS, total, 0.0)
    c = total + pltpu.roll(total, C_HEADS, 1) + pltpu.roll(total, 2 * C_HEADS, 1) + carry_ref[0:1, :]
    carry_ref[...] = jnp.broadcast_to(c[tm - 1:tm, :], carry_ref.shape)
    project(n2, n3)
    proj_ref[:, MM_WIDTH:P_WIDTH] = _dot(pieces_by_group(c * (-LOG2E)), place_ref[...]).astype(BF16)
    project(n3, len(starts))


def _pre_call(x, norm_gain, w_pre, wf, b_f, place, layer, *, tm=1024, chunk=512):
    B, S, D = x.shape
    kern = functools.partial(_pre_kernel, tm=tm, chunk=chunk)
    return pl.pallas_call(
        kern,
        grid=(B, S // tm),
        in_specs=[
            pl.BlockSpec((None, tm, D), lambda b, i: (b, i, 0)),
            pl.BlockSpec((None, 1, D), lambda b, i: (layer, 0, 0)),
            pl.BlockSpec((None, D, MM_WIDTH), lambda b, i: (layer, 0, 0), pipeline_mode=pl.Buffered(1)),
            pl.BlockSpec((None, D, LANES), lambda b, i: (layer, 0, 0)),
            pl.BlockSpec((None, 1, LANES), lambda b, i: (layer, 0, 0)),
            pl.BlockSpec((LANES, N_PAIRS * LANES), lambda b, i: (0, 0)),
        ],
        out_specs=pl.BlockSpec((None, tm, P_WIDTH), lambda b, i: (b, i, 0)),
        out_shape=jax.ShapeDtypeStruct((B, S, P_WIDTH), BF16),
        scratch_shapes=[pltpu.VMEM((8, LANES), F32)],
        compiler_params=_cparams(("parallel", "arbitrary")),
        name="pre",
    )(x, norm_gain, w_pre, wf, b_f, place)


def _chunk_order(nq):
    order = [(t, t) for t in range(nq)]
    order += [(qi, c) for qi in range(1, nq) for c in range(qi)]
    return order


def _flash_kernel(*refs, mode, tq, nq, unroll):
    if mode == "diff":
        (consts_ref, tab_ref, q_ref, k_ref, v_ref, aug_ref, lam_ref, sg_ref, o_ref,
         qe_ref, m_ref, acc_ref, sa_ref, sb_ref) = refs
    else:
        (tab_ref, q_ref, k_ref, v_ref, aug_ref, o_ref, qe_ref, m_ref, acc_ref, sa_ref, sb_ref) = refs
    tk = tq
    rows = 2 * tq
    half = tq // 2
    n_chunks = nq * (nq + 1) // 2
    lane = lax.broadcasted_iota(jnp.int32, (half, LANES), 1)

    def build_qe(t, carry):
        for hf in range(2):
            q = q_ref[pl.ds(pl.multiple_of(t * tq + hf * half, half), half), :]
            zero = jnp.zeros_like(q)
            for m in range(2):
                in_map = (lane >= HEAD_DIM) if m else (lane < HEAD_DIM)
                first = N_PIECES * m if mode == "fox" else 0
                ones_at_pieces = jnp.where(lane >= first, 1.0, 0.0) * jnp.where(lane < first + N_PIECES, 1.0, 0.0)
                g0 = (2 * hf + m) * half
                qe_ref[t, g0:g0 + half, 0:LANES] = jnp.where(in_map, q, zero)
                qe_ref[t, g0:g0 + half, LANES:2 * LANES] = ones_at_pieces.astype(BF16)
        return carry

    lax.fori_loop(0, nq, build_qe, 0)

    def keys(start, n):
        return jnp.concatenate([k_ref[pl.ds(start, n), :], aug_ref[pl.ds(start, n), :]], axis=1)

    def values(start, n):
        return jnp.concatenate([v_ref[pl.ds(start, n), :], jnp.ones((n, LANES), BF16)], axis=1)

    def qk(t, s_ref):
        qi = tab_ref[0, t]
        start = pl.multiple_of(tab_ref[1, t] * tk, tk)
        s_ref[...] = _dot_nt(qe_ref[qi], keys(start, tk))

    def qk_diagonal(t, s_ref):
        s_ref[0:tq, 0:half] = _dot_nt(qe_ref[t, 0:tq], keys(t * tk, half))
        s_ref[tq:rows, :] = _dot_nt(qe_ref[t, tq:rows], keys(t * tk, tk))

    def probabilities(s, m_prev):
        cols = [s[:, c0:c0 + LANES] for c0 in range(0, s.shape[1], LANES)]
        mx = cols[0]
        for sc in cols[1:]:
            mx = jnp.maximum(mx, sc)
        m_new = jnp.broadcast_to(jnp.max(mx, axis=1, keepdims=True), (s.shape[0], LANES))
        if m_prev is not None:
            m_new = jnp.maximum(m_prev, m_new)
        return m_new, jnp.concatenate([jnp.exp2(sc - m_new).astype(BF16) for sc in cols], axis=1)

    def softmax_pv(s_ref, t):
        qi = tab_ref[0, t]
        start = pl.multiple_of(tab_ref[1, t] * tk, tk)
        m_prev = m_ref[qi]
        m_new, p = probabilities(s_ref[...], m_prev)
        alpha = jnp.exp2(m_prev - m_new)
        pv = _dot(p, values(start, tk))
        acc_ref[qi, :, 0:LANES] = alpha * acc_ref[qi, :, 0:LANES] + pv[:, 0:LANES]
        acc_ref[qi, :, LANES:2 * LANES] = alpha * acc_ref[qi, :, LANES:2 * LANES] + pv[:, LANES:2 * LANES]
        m_ref[qi] = m_new

    def softmax_pv_diagonal(s_ref, t):
        for r0, n_keys in ((0, half), (tq, tk)):
            r = lax.broadcasted_iota(jnp.int32, (tq, n_keys), 0)
            c = lax.broadcasted_iota(jnp.int32, (tq, n_keys), 1)
            q_pos = jnp.where(r >= half, r - half, r) + (half if r0 else 0)
            s = jnp.where(c <= q_pos, s_ref[r0:r0 + tq, 0:n_keys], -jnp.inf)
            m_new, p = probabilities(s, None)
            acc_ref[t, r0:r0 + tq, :] = _dot(p, values(t * tk, n_keys))
            m_ref[t, r0:r0 + tq, :] = m_new

    s_refs = (sa_ref, sb_ref)

    qk_diagonal(0, sa_ref)
    for t in range(nq):
        if t + 1 < nq:
            qk_diagonal(t + 1, s_refs[(t + 1) % 2])
        else:
            qk(t + 1, s_refs[(t + 1) % 2])
        softmax_pv_diagonal(s_refs[t % 2], t)

    def run(first, last, unroll):
        def body(i, carry):
            for u in range(unroll):
                t = first + i * unroll + u
                qk(t + 1, s_refs[(u + 1) % 2])
                softmax_pv(s_refs[u % 2], t)
            return carry
        lax.fori_loop(0, (last - first) // unroll, body, 0)

    main = nq + (n_chunks - nq) // unroll * unroll
    if main > nq:
        run(nq, main, unroll)
    if n_chunks > main:
        run(main, n_chunks, n_chunks - main)

    def finalize(t):
        for hf in range(2):
            a0 = acc_ref[t, (2 * hf) * half:(2 * hf + 1) * half, :]
            a1 = acc_ref[t, (2 * hf + 1) * half:(2 * hf + 2) * half, :]
            o0 = a0[:, 0:LANES] / a0[:, LANES:2 * LANES]
            o1 = a1[:, 0:LANES] / a1[:, LANES:2 * LANES]
            if mode == "diff":
                lam_init = consts_ref[0]
                lp = lam_ref[...]
                lam = (jnp.exp(jnp.sum(lp[0:1] * lp[1:2], axis=1, keepdims=True))
                       - jnp.exp(jnp.sum(lp[2:3] * lp[3:4], axis=1, keepdims=True)) + lam_init)
                y = _rms(o0 - lam * o1, sg_ref[...], SUBLN_EPS) * consts_ref[1]
            else:
                y = jnp.where(lane < HEAD_DIM, o0, o1)
            o_ref[t * tq + hf * half:t * tq + (hf + 1) * half, :] = y.astype(o_ref.dtype)

    for t in range(nq):
        finalize(t)


def _flash_call(mode, proj, q_blk, k_blk, v_blk, extra=(), *, tq=512, unroll=14):
    B, S, _ = proj.shape
    nq = S // tq
    order = _chunk_order(nq)
    assert nq % 2 == 0 and len(order) % 2 == 0, "the chunk loop handles two chunks per trip"
    table = jnp.asarray(np.array(order + [order[-1]], np.int32).T)
    kern = functools.partial(_flash_kernel, mode=mode, tq=tq, nq=nq, unroll=unroll)
    seq = lambda blk: pl.BlockSpec((None, S, LANES), lambda b, p: (b, 0, blk + p))
    smem = pl.BlockSpec(memory_space=pltpu.SMEM)
    in_specs = [smem, seq(q_blk), seq(k_blk), seq(v_blk)]
    if mode == "diff":
        consts, alibi_aug, lam_rows, subln = extra
        in_specs = [smem] + in_specs + [
            pl.BlockSpec((None, S, LANES), lambda b, p: (p, 0, 0)),
            pl.BlockSpec((8, LANES), lambda b, p: (0, 0)),
            pl.BlockSpec((1, LANES), lambda b, p: (0, 0)),
        ]
        args = (consts, table, proj, proj, proj, alibi_aug, lam_rows, subln)
    else:
        in_specs = in_specs + [seq(P_AUG)]
        args = (table, proj, proj, proj, proj)
    return pl.pallas_call(
        kern,
        grid=(B, N_PAIRS),
        in_specs=in_specs,
        out_specs=pl.BlockSpec((None, S, LANES), lambda b, p: (b, 0, p)),
        out_shape=jax.ShapeDtypeStruct((B, S, BRANCH_WIDTH), BF16),
        scratch_shapes=[
            pltpu.VMEM((nq, 2 * tq, 2 * LANES), BF16),
            pltpu.VMEM((nq, 2 * tq, LANES), F32),
            pltpu.VMEM((nq, 2 * tq, 2 * LANES), F32),
            pltpu.VMEM((2 * tq, tq), F32),
            pltpu.VMEM((2 * tq, tq), F32),
        ],
        compiler_params=_cparams(("parallel", "parallel")),
        name="flash_" + mode,
    )(*args)


def _swa_kernel(slopes_ref, sinks_ref, q_ref, k_ref, v_ref, pos_ref, o_ref, e_ref, *, seq, unroll):
    kv = pl.program_id(1)
    lane = lax.broadcasted_iota(jnp.int32, (SWA_BLOCK, LANES), 1)
    row = lax.broadcasted_iota(jnp.int32, (SWA_BLOCK, LANES), 0)
    r = lax.broadcasted_iota(jnp.int32, (SWA_BLOCK, 2 * SWA_BLOCK), 0)
    c = lax.broadcasted_iota(jnp.int32, (SWA_BLOCK, 2 * SWA_BLOCK), 1)
    for g in range(B_GROUP):
        first = N_PIECES * (kv * B_GROUP + g)
        ones_at_pieces = jnp.where(lane >= first, 1.0, 0.0) * jnp.where(lane < first + N_PIECES, 1.0, 0.0)
        e_ref[g * SWA_BLOCK:(g + 1) * SWA_BLOCK, :] = ones_at_pieces.astype(BF16)
    ones = jnp.ones((2 * SWA_BLOCK, LANES), BF16)

    def block(q_start):
        k_start = pl.multiple_of(jnp.maximum(q_start - SWA_BLOCK, 0), SWA_BLOCK)
        kw = jnp.concatenate([k_ref[pl.ds(k_start, 2 * SWA_BLOCK), :],
                              pos_ref[pl.ds(k_start, 2 * SWA_BLOCK), :]], axis=1)
        vw = jnp.concatenate([v_ref[pl.ds(k_start, 2 * SWA_BLOCK), :], ones], axis=1)
        qs = jnp.concatenate(
            [q_ref[pl.ds(q_start, SWA_BLOCK), g * LANES:(g + 1) * LANES] for g in range(B_GROUP)],
            axis=0)
        s_all = _dot_nt(jnp.concatenate([qs, e_ref[...]], axis=1), kw)
        dist = (q_start - k_start) + r - c
        valid = (dist >= 0) & (dist < WINDOW)
        qpos = (q_start + row).astype(F32)
        ps, tails = [], []
        for g in range(B_GROUP):
            h = kv * B_GROUP + g
            sink = sinks_ref[h] + slopes_ref[h] * qpos
            s = jnp.where(valid, s_all[g * SWA_BLOCK:(g + 1) * SWA_BLOCK], -jnp.inf)
            s0, s1 = s[:, 0:LANES], s[:, LANES:2 * LANES]
            mx = jnp.maximum(jnp.max(jnp.maximum(s0, s1), axis=1, keepdims=True), sink)
            ps.append(jnp.concatenate([jnp.exp2(s0 - mx).astype(BF16), jnp.exp2(s1 - mx).astype(BF16)], axis=1))
            tails.append(jnp.exp2(sink - mx))
        pv = _dot(jnp.concatenate(ps, axis=0), vw)
        outs = []
        for g in range(B_GROUP):
            blk = pv[g * SWA_BLOCK:(g + 1) * SWA_BLOCK]
            outs.append(blk[:, 0:LANES] / (blk[:, LANES:2 * LANES] + tails[g]))
        for h2 in range(B_GROUP // 2):
            o_ref[pl.ds(q_start, SWA_BLOCK), h2 * LANES:(h2 + 1) * LANES] = (
                jnp.where(lane < HEAD_DIM, outs[2 * h2], outs[2 * h2 + 1]).astype(o_ref.dtype))

    def body(i, carry):
        for n in range(unroll):
            block(pl.multiple_of((i * unroll + n) * SWA_BLOCK, SWA_BLOCK))
        return carry

    lax.fori_loop(0, seq // (SWA_BLOCK * unroll), body, 0)


def _swa_call(proj, slopes2, sinks2, pos_aug, *, unroll=8):
    B, S, _ = proj.shape
    kern = functools.partial(_swa_kernel, seq=S, unroll=unroll)
    qw = B_GROUP * LANES
    return pl.pallas_call(
        kern,
        grid=(B, B_KV_HEADS),
        in_specs=[
            pl.BlockSpec(memory_space=pltpu.SMEM),
            pl.BlockSpec(memory_space=pltpu.SMEM),
            pl.BlockSpec((None, S, qw), lambda b, kv: (b, 0, P_QB // B_GROUP + kv)),
            pl.BlockSpec((None, S, LANES), lambda b, kv: (b, 0, P_KB)),
            pl.BlockSpec((None, S, LANES), lambda b, kv: (b, 0, P_VB + kv)),
            pl.BlockSpec((S, LANES), lambda b, kv: (0, 0)),
        ],
        out_specs=pl.BlockSpec((None, S, 2 * LANES), lambda b, kv: (b, 0, kv)),
        out_shape=jax.ShapeDtypeStruct((B, S, BRANCH_WIDTH), BF16),
        scratch_shapes=[pltpu.VMEM((B_GROUP * SWA_BLOCK, LANES), BF16)],
        compiler_params=_cparams(("parallel", "parallel")),
        name="swa",
    )(slopes2, sinks2, proj, proj, proj, pos_aug)


def _post_kernel(x_ref, g_ref, fg_ref, ya_ref, yb_ref, yc_ref, wg_ref, wgm_ref, wua_ref, wub_ref, wuc_ref,
                 wo_ref, o_ref, *, final):
    x = x_ref[...]
    h = _rms(x, g_ref[...], RMS_EPS).astype(BF16)
    hw = D_MODEL // 2
    ts = []
    for br, y_ref in enumerate((ya_ref, yb_ref, yc_ref)):
        g = _dot(h, wg_ref[:, br * BRANCH_WIDTH:(br + 1) * BRANCH_WIDTH])
        ts.append((y_ref[...].astype(F32) * (g * jax.nn.sigmoid(g))).astype(BF16))
    merged = [None, None]
    for br, wu_ref in enumerate((wua_ref, wub_ref, wuc_ref)):
        for j in range(2):
            gate = jax.nn.sigmoid(_dot(h, wgm_ref[:, br * D_MODEL + j * hw:br * D_MODEL + (j + 1) * hw]))
            u = _dot(ts[br], wu_ref[:, j * hw:(j + 1) * hw])
            merged[j] = gate * u if merged[j] is None else merged[j] + gate * u
    mb = jnp.concatenate(merged, axis=1).astype(BF16)
    for j in range(2):
        o_ref[:, j * hw:(j + 1) * hw] = x[:, j * hw:(j + 1) * hw] + _dot(mb, wo_ref[:, j * hw:(j + 1) * hw])
    if final:
        o_ref[...] = _rms(o_ref[...], fg_ref[...], RMS_EPS)


def _post_call(x, norm_gain, final_gain, ya, yb, yc, w_g, w_gm, w_ua, w_ub, w_uc, w_o, layer, final, *, tm=1024):
    B, S, D = x.shape
    kern = functools.partial(_post_kernel, final=final)
    tok = lambda w: pl.BlockSpec((None, tm, w), lambda b, i: (b, i, 0))
    lay = lambda r, c: pl.BlockSpec((None, r, c), lambda b, i: (layer, 0, 0), pipeline_mode=pl.Buffered(1))
    return pl.pallas_call(
        kern,
        grid=(B, S // tm),
        in_specs=[
            tok(D), lay(1, D), pl.BlockSpec((1, D), lambda b, i: (0, 0)),
            tok(BRANCH_WIDTH), tok(BRANCH_WIDTH), tok(BRANCH_WIDTH),
            lay(D, N_BRANCH * BRANCH_WIDTH), lay(D, N_BRANCH * D),
            lay(BRANCH_WIDTH, D), lay(BRANCH_WIDTH, D), lay(BRANCH_WIDTH, D), lay(D, D),
        ],
        out_specs=tok(D),
        out_shape=jax.ShapeDtypeStruct((B, S, D), F32),
        compiler_params=_cparams(("parallel", "parallel")),
        name="post",
    )(x, norm_gain, final_gain, ya, yb, yc, w_g, w_gm, w_ua, w_ub, w_uc, w_o)


def _alibi_slopes(n_heads):
    return 2.0 ** (-8.0 * jnp.arange(1, n_heads + 1, dtype=F32) / n_heads)


def _prepare_weights(w_in):
    col_scale = np.ones((_D_IN,), np.float32)
    for o in (O_QA, O_QB, O_QC):
        col_scale[o:o + 512] = HEAD_DIM ** -0.5 * LOG2E
    wb = (w_in * col_scale).astype(BF16)
    sl = lambda o, n: wb[:, :, o:o + n]
    zeros64 = jnp.zeros(wb.shape[:2] + (HEAD_DIM,), BF16)
    qb_blocks = []
    for n in range(B_HEADS):
        qn = sl(O_QB + n * HEAD_DIM, HEAD_DIM)
        qb_blocks += [qn, zeros64] if n // B_GROUP == 0 else [zeros64, qn]
    v0, v1 = sl(O_VB, HEAD_DIM), sl(O_VB + HEAD_DIM, HEAD_DIM)
    w_pre = jnp.concatenate(
        [sl(O_QA, 512), sl(O_KA, 512), sl(O_VA, 512), sl(O_QC, 512), sl(O_KC, 512), sl(O_VC, 512)]
        + qb_blocks + [sl(O_KB, 128), v0, v0, v1, v1], axis=-1)
    wf = jnp.pad(jnp.tile(sl(O_FC, C_HEADS), (1, 1, N_PIECES)),
                 ((0, 0), (0, 0), (0, LANES - N_PIECES * C_HEADS)))
    w_g = jnp.concatenate([sl(O_GA, 512), sl(O_GB, 512), sl(O_GC, 512)], axis=-1)
    w_gm = sl(O_GM, N_BRANCH * D_MODEL)
    return w_pre, wf, w_g, w_gm


def _placement():
    place = np.zeros((LANES, N_PAIRS * LANES), np.float32)
    for x in range(N_PIECES):
        for h in range(C_HEADS):
            place[C_HEADS * x + h, (h // 2) * LANES + N_PIECES * (h % 2) + x] = 1.0
    return jnp.asarray(place, BF16)


def _alibi_pieces(n_heads, seq):
    slopes = 2.0 ** (-8.0 * np.arange(1, n_heads + 1, dtype=np.float64) / n_heads)
    rest = (slopes[:, None] * (LOG2E * np.arange(seq, dtype=np.float64))[None, :]).astype(np.float32)
    pieces = np.zeros((n_heads, seq, N_PIECES), np.float32)
    for x in range(N_PIECES):
        pieces[:, :, x] = rest.astype(BF16).astype(np.float32)
        rest = rest - pieces[:, :, x]
    return pieces


def _alibi_aug(seq):
    table = np.zeros((A_HEADS, seq, LANES), np.float32)
    table[:, :, 0:N_PIECES] = _alibi_pieces(A_HEADS, seq)
    return jnp.asarray(table, BF16)


def _swa_pos_aug(seq):
    table = np.zeros((seq, LANES), np.float32)
    pieces = _alibi_pieces(B_HEADS, seq)
    for h in range(B_HEADS):
        table[:, N_PIECES * h:N_PIECES * (h + 1)] = pieces[h]
    return jnp.asarray(table, BF16)


def kernel(x, norm_gain, w_in, b_forget, lambda_q1, lambda_k1, lambda_q2, lambda_k2, subln_gain, sinks,
           w_up_a, w_up_b, w_up_c, w_o, final_gain):
    B, S, D = x.shape
    depth = w_in.shape[0]
    w_pre, wf, w_g, w_gm = _prepare_weights(w_in)
    w_ua, w_ub, w_uc, w_ob = (w.astype(BF16) for w in (w_up_a, w_up_b, w_up_c, w_o))
    gain3 = norm_gain.reshape(depth, 1, D)
    fgain = final_gain.reshape(1, D)
    bf3 = jnp.pad(jnp.tile(b_forget.astype(F32), (1, N_PIECES)),
                  ((0, 0), (0, LANES - N_PIECES * C_HEADS))).reshape(depth, 1, LANES)
    place = _placement()
    alibi_aug = _alibi_aug(S)
    slopes2_b = _alibi_slopes(B_HEADS) * LOG2E
    pos_aug = _swa_pos_aug(S)
    pad = lambda v: jnp.pad(v.astype(F32), (0, LANES - v.shape[0]))

    for l in range(depth):
        lam_init = 0.8 - 0.6 * math.exp(-0.3 * l)
        proj = _pre_call(x, gain3, w_pre, wf, bf3, place, l)
        consts = jnp.array([lam_init, 1.0 - lam_init], F32)
        lam_rows = jnp.zeros((8, LANES), F32).at[0:4].set(
            jnp.stack([pad(lambda_q1[l]), pad(lambda_k1[l]), pad(lambda_q2[l]), pad(lambda_k2[l])]))
        ya = _flash_call("diff", proj, P_QA, P_KA, P_VA,
                         (consts, alibi_aug, lam_rows, subln_gain[l].astype(F32).reshape(1, LANES)))
        yc = _flash_call("fox", proj, P_QC, P_KC, P_VC)
        yb = _swa_call(proj, slopes2_b, sinks[l].astype(F32) * LOG2E, pos_aug)
        x = _post_call(x, gain3, fgain, ya, yb, yc, w_g, w_gm, w_ua, w_ub, w_uc, w_ob, l, l == depth - 1)
    return x
```

```python
import functools
import math

import numpy as np
import jax
import jax.numpy as jnp
from jax import lax
from jax.experimental import pallas as pl
from jax.experimental.pallas import tpu as pltpu

F32 = jnp.float32
BF16 = jnp.bfloat16

D_MODEL = 1024
HEAD_DIM = 64
LANES = 128
RMS_EPS = 1e-6
SUBLN_EPS = 1e-5
A_HEADS = 4
B_HEADS = 8
B_KV_HEADS = 2
B_GROUP = B_HEADS // B_KV_HEADS
C_HEADS = 8
WINDOW = 128
SWA_BLOCK = 128
N_BRANCH = 3
BRANCH_WIDTH = 512
N_PAIRS = 4
N_PIECES = 3
LOG2E = math.log2(math.e)

_SPLITS = (512, 512, 512, 512, 512, 128, 128, 512, 512, 512, 512, 8, 512, 3 * D_MODEL)
_OFF = [0]
for _s in _SPLITS:
    _OFF.append(_OFF[-1] + _s)
(O_QA, O_KA, O_VA, O_GA, O_QB, O_KB, O_VB, O_GB, O_QC, O_KC, O_VC, O_FC, O_GC, O_GM, _D_IN) = _OFF

P_QA, P_KA, P_VA = 0, 4, 8
P_QC, P_KC, P_VC = 12, 16, 20
P_QB = 24
P_KB = 32
P_VB = 33
P_MM = 35
P_AUG = 35
P_BLOCKS = 39
P_WIDTH = P_BLOCKS * LANES
MM_WIDTH = P_MM * LANES

VMEM_LIMIT = 56 * 1024 * 1024


def _cparams(sem):
    return pltpu.CompilerParams(dimension_semantics=sem, vmem_limit_bytes=VMEM_LIMIT)


def _rms(x, gain, eps):
    ms = jnp.mean(x * x, axis=-1, keepdims=True)
    return x * lax.rsqrt(ms + eps) * gain


def _dot(a, b):
    return jnp.dot(a, b, preferred_element_type=F32)


def _dot_nt(a, b):
    return lax.dot_general(a, b, (((1,), (1,)), ((), ())), preferred_element_type=F32)


def _split3(v):
    hi = v.astype(BF16)
    r = v - hi.astype(F32)
    mid = r.astype(BF16)
    lo = (r - mid.astype(F32)).astype(BF16)
    return hi, mid, lo


def _pre_kernel(x_ref, g_ref, w_ref, wf_ref, bf_ref, place_ref, proj_ref, carry_ref, *, tm, chunk):
    i = pl.program_id(1)

    @pl.when(i == 0)
    def _():
        carry_ref[...] = jnp.zeros_like(carry_ref)

    h = _rms(x_ref[...], g_ref[...], RMS_EPS).astype(BF16)

    lane = lax.broadcasted_iota(jnp.int32, (tm, LANES), 1)

    def pieces_by_group(v):
        p0, p1, p2 = (p.astype(F32) for p in _split3(v))
        by_group = jnp.where(lane < C_HEADS, p0, jnp.where(lane < 2 * C_HEADS, p1, p2))
        return jnp.where(lane < N_PIECES * C_HEADS, by_group, 0.0).astype(BF16)

    starts = list(range(0, MM_WIDTH, chunk))

    def project(first, last):
        for c0 in starts[first:last]:
            c1 = min(c0 + chunk, MM_WIDTH)
            proj_ref[:, c0:c1] = _dot(h, w_ref[:, c0:c1]).astype(BF16)

    n1, n2, n3 = len(starts) // 4, len(starts) // 2, 3 * len(starts) // 4
    project(0, n1)
    z = _dot(h, wf_ref[...]) + bf_ref[...]
    logf = jnp.minimum(z, 0.0) - jnp.log1p(jnp.exp(-jnp.abs(z)))
    project(n1, n2)
    row = lax.broadcasted_iota(jnp.int32, (tm, tm), 0)
    col = lax.broadcasted_iota(jnp.int32, (tm, tm), 1)
    lower = (col <= row).astype(BF16)
    part = _dot(lower, pieces_by_group(logf))
    total = part + pltpu.roll(part, LANES - C_HEADS, 1) + pltpu.roll(part, LANES - 2 * C_HEADS, 1)
    total = jnp.where(lane < C_HEADS, total, 0.0)
    c = total + pltpu.roll(total, C_HEADS, 1) + pltpu.roll(total, 2 * C_HEADS, 1) + carry_ref[0:1, :]
    carry_ref[...] = jnp.broadcast_to(c[tm - 1:tm, :], carry_ref.shape)
    project(n2, n3)
    proj_ref[:, MM_WIDTH:P_WIDTH] = _dot(pieces_by_group(c * (-LOG2E)), place_ref[...]).astype(BF16)
    project(n3, len(starts))


def _pre_call(x, norm_gain, w_pre, wf, b_f, place, layer, *, tm=1024, chunk=512):
    B, S, D = x.shape
    kern = functools.partial(_pre_kernel, tm=tm, chunk=chunk)
    return pl.pallas_call(
        kern,
        grid=(B, S // tm),
        in_specs=[
            pl.BlockSpec((None, tm, D), lambda b, i: (b, i, 0)),
            pl.BlockSpec((None, 1, D), lambda b, i: (layer, 0, 0)),
            pl.BlockSpec((None, D, MM_WIDTH), lambda b, i: (layer, 0, 0), pipeline_mode=pl.Buffered(1)),
            pl.BlockSpec((None, D, LANES), lambda b, i: (layer, 0, 0)),
            pl.BlockSpec((None, 1, LANES), lambda b, i: (layer, 0, 0)),
            pl.BlockSpec((LANES, N_PAIRS * LANES), lambda b, i: (0, 0)),
        ],
        out_specs=pl.BlockSpec((None, tm, P_WIDTH), lambda b, i: (b, i, 0)),
        out_shape=jax.ShapeDtypeStruct((B, S, P_WIDTH), BF16),
        scratch_shapes=[pltpu.VMEM((8, LANES), F32)],
        compiler_params=_cparams(("parallel", "arbitrary")),
        name="pre",
    )(x, norm_gain, w_pre, wf, b_f, place)


def _chunk_order(nq):
    order = [(t, t) for t in range(nq)]
    order += [(qi, c) for qi in range(1, nq) for c in range(qi)]
    return order


def _flash_kernel(*refs, mode, tq, nq, unroll):
    if mode == "diff":
        (consts_ref, tab_ref, q_ref, k_ref, v_ref, aug_ref, lam_ref, o_ref,
         qe_ref, m_ref, acc_ref, sa_ref, sb_ref) = refs
    else:
        (tab_ref, q_ref, k_ref, v_ref, aug_ref, o_ref, qe_ref, m_ref, acc_ref, sa_ref, sb_ref) = refs
    tk = tq
    rows = 2 * tq
    half = tq // 2
    n_chunks = nq * (nq + 1) // 2
    lane = lax.broadcasted_iota(jnp.int32, (half, LANES), 1)

    def build_qe(t, carry):
        for hf in range(2):
            q = q_ref[pl.ds(pl.multiple_of(t * tq + hf * half, half), half), :]
            zero = jnp.zeros_like(q)
            for m in range(2):
                in_map = (lane >= HEAD_DIM) if m else (lane < HEAD_DIM)
                first = N_PIECES * m if mode == "fox" else 0
                ones_at_pieces = jnp.where(lane >= first, 1.0, 0.0) * jnp.where(lane < first + N_PIECES, 1.0, 0.0)
                g0 = (2 * hf + m) * half
                qe_ref[t, g0:g0 + half, 0:LANES] = jnp.where(in_map, q, zero)
                qe_ref[t, g0:g0 + half, LANES:2 * LANES] = ones_at_pieces.astype(BF16)
        return carry

    lax.fori_loop(0, nq, build_qe, 0)

    def keys(start, n):
        return jnp.concatenate([k_ref[pl.ds(start, n), :], aug_ref[pl.ds(start, n), :]], axis=1)

    def values(start, n):
        return jnp.concatenate([v_ref[pl.ds(start, n), :], jnp.ones((n, LANES), BF16)], axis=1)

    def qk(t, s_ref):
        qi = tab_ref[0, t]
        start = pl.multiple_of(tab_ref[1, t] * tk, tk)
        s_ref[...] = _dot_nt(qe_ref[qi], keys(start, tk))

    def qk_diagonal(t, s_ref):
        s_ref[0:tq, 0:half] = _dot_nt(qe_ref[t, 0:tq], keys(t * tk, half))
        s_ref[tq:rows, :] = _dot_nt(qe_ref[t, tq:rows], keys(t * tk, tk))

    def probabilities(s, m_prev):
        cols = [s[:, c0:c0 + LANES] for c0 in range(0, s.shape[1], LANES)]
        mx = cols[0]
        for sc in cols[1:]:
            mx = jnp.maximum(mx, sc)
        m_new = jnp.broadcast_to(jnp.max(mx, axis=1, keepdims=True), (s.shape[0], LANES))
        if m_prev is not None:
            m_new = jnp.maximum(m_prev, m_new)
        return m_new, jnp.concatenate([jnp.exp2(sc - m_new).astype(BF16) for sc in cols], axis=1)

    def softmax_pv(s_ref, t):
        qi = tab_ref[0, t]
        start = pl.multiple_of(tab_ref[1, t] * tk, tk)
        m_prev = m_ref[qi]
        m_new, p = probabilities(s_ref[...], m_prev)
        alpha = jnp.exp2(m_prev - m_new)
        pv = _dot(p, values(start, tk))
        acc_ref[qi, :, 0:LANES] = alpha * acc_ref[qi, :, 0:LANES] + pv[:, 0:LANES]
        acc_ref[qi, :, LANES:2 * LANES] = alpha * acc_ref[qi, :, LANES:2 * LANES] + pv[:, LANES:2 * LANES]
        m_ref[qi] = m_new

    def softmax_pv_diagonal(s_ref, t):
        for r0, n_keys in ((0, half), (tq, tk)):
            r = lax.broadcasted_iota(jnp.int32, (tq, n_keys), 0)
            c = lax.broadcasted_iota(jnp.int32, (tq, n_keys), 1)
            q_pos = jnp.where(r >= half, r - half, r) + (half if r0 else 0)
            s = jnp.where(c <= q_pos, s_ref[r0:r0 + tq, 0:n_keys], -jnp.inf)
            m_new, p = probabilities(s, None)
            acc_ref[t, r0:r0 + tq, :] = _dot(p, values(t * tk, n_keys))
            m_ref[t, r0:r0 + tq, :] = m_new

    s_refs = (sa_ref, sb_ref)

    qk_diagonal(0, sa_ref)
    for t in range(nq):
        if t + 1 < nq:
            qk_diagonal(t + 1, s_refs[(t + 1) % 2])
        else:
            qk(t + 1, s_refs[(t + 1) % 2])
        softmax_pv_diagonal(s_refs[t % 2], t)

    def run(first, last, unroll):
        def body(i, carry):
            for u in range(unroll):
                t = first + i * unroll + u
                qk(t + 1, s_refs[(u + 1) % 2])
                softmax_pv(s_refs[u % 2], t)
            return carry
        lax.fori_loop(0, (last - first) // unroll, body, 0)

    main = nq + (n_chunks - nq) // unroll * unroll
    if main > nq:
        run(nq, main, unroll)
    if n_chunks > main:
        run(main, n_chunks, n_chunks - main)

    def finalize(t):
        for hf in range(2):
            a0 = acc_ref[t, (2 * hf) * half:(2 * hf + 1) * half, :]
            a1 = acc_ref[t, (2 * hf + 1) * half:(2 * hf + 2) * half, :]
            o0 = a0[:, 0:LANES] / a0[:, LANES:2 * LANES]
            o1 = a1[:, 0:LANES] / a1[:, LANES:2 * LANES]
            if mode == "diff":
                lam_init = consts_ref[0]
                lp = lam_ref[...]
                lam = (jnp.exp(jnp.sum(lp[0:1] * lp[1:2], axis=1, keepdims=True))
                       - jnp.exp(jnp.sum(lp[2:3] * lp[3:4], axis=1, keepdims=True)) + lam_init)
                y = o0 - lam * o1
            else:
                y = jnp.where(lane < HEAD_DIM, o0, o1)
            o_ref[t * tq + hf * half:t * tq + (hf + 1) * half, :] = y.astype(o_ref.dtype)

    for t in range(nq):
        finalize(t)


def _flash_call(mode, proj, q_blk, k_blk, v_blk, extra=(), *, tq=512, unroll=14):
    B, S, _ = proj.shape
    nq = S // tq
    order = _chunk_order(nq)
    assert nq % 2 == 0 and len(order) % 2 == 0, "the chunk loop handles two chunks per trip"
    table = jnp.asarray(np.array(order + [order[-1]], np.int32).T)
    kern = functools.partial(_flash_kernel, mode=mode, tq=tq, nq=nq, unroll=unroll)
    seq = lambda blk: pl.BlockSpec((None, S, LANES), lambda b, p: (b, 0, blk + p))
    smem = pl.BlockSpec(memory_space=pltpu.SMEM)
    in_specs = [smem, seq(q_blk), seq(k_blk), seq(v_blk)]
    if mode == "diff":
        consts, alibi_aug, lam_rows = extra
        in_specs = [smem] + in_specs + [
            pl.BlockSpec((None, S, LANES), lambda b, p: (p, 0, 0)),
            pl.BlockSpec((8, LANES), lambda b, p: (0, 0)),
        ]
        args = (consts, table, proj, proj, proj, alibi_aug, lam_rows)
    else:
        in_specs = in_specs + [seq(P_AUG)]
        args = (table, proj, proj, proj, proj)
    return pl.pallas_call(
        kern,
        grid=(B, N_PAIRS),
        in_specs=in_specs,
        out_specs=pl.BlockSpec((None, S, LANES), lambda b, p: (b, 0, p)),
        out_shape=jax.ShapeDtypeStruct((B, S, BRANCH_WIDTH), BF16),
        scratch_shapes=[
            pltpu.VMEM((nq, 2 * tq, 2 * LANES), BF16),
            pltpu.VMEM((nq, 2 * tq, LANES), F32),
            pltpu.VMEM((nq, 2 * tq, 2 * LANES), F32),
            pltpu.VMEM((2 * tq, tq), F32),
            pltpu.VMEM((2 * tq, tq), F32),
        ],
        compiler_params=_cparams(("parallel", "parallel")),
        name="flash_" + mode,
    )(*args)


def _swa_kernel(slopes_ref, sinks_ref, q_ref, k_ref, v_ref, pos_ref, o_ref, e_ref, *, seq, unroll):
    kv = pl.program_id(1)
    lane = lax.broadcasted_iota(jnp.int32, (SWA_BLOCK, LANES), 1)
    row = lax.broadcasted_iota(jnp.int32, (SWA_BLOCK, LANES), 0)
    r = lax.broadcasted_iota(jnp.int32, (SWA_BLOCK, 2 * SWA_BLOCK), 0)
    c = lax.broadcasted_iota(jnp.int32, (SWA_BLOCK, 2 * SWA_BLOCK), 1)
    for g in range(B_GROUP):
        first = N_PIECES * (kv * B_GROUP + g)
        ones_at_pieces = jnp.where(lane >= first, 1.0, 0.0) * jnp.where(lane < first + N_PIECES, 1.0, 0.0)
        e_ref[g * SWA_BLOCK:(g + 1) * SWA_BLOCK, :] = ones_at_pieces.astype(BF16)
    ones = jnp.ones((2 * SWA_BLOCK, LANES), BF16)

    def block(q_start):
        k_start = pl.multiple_of(jnp.maximum(q_start - SWA_BLOCK, 0), SWA_BLOCK)
        kw = jnp.concatenate([k_ref[pl.ds(k_start, 2 * SWA_BLOCK), :],
                              pos_ref[pl.ds(k_start, 2 * SWA_BLOCK), :]], axis=1)
        vw = jnp.concatenate([v_ref[pl.ds(k_start, 2 * SWA_BLOCK), :], ones], axis=1)
        qs = jnp.concatenate(
            [q_ref[pl.ds(q_start, SWA_BLOCK), g * LANES:(g + 1) * LANES] for g in range(B_GROUP)],
            axis=0)
        s_all = _dot_nt(jnp.concatenate([qs, e_ref[...]], axis=1), kw)
        dist = (q_start - k_start) + r - c
        valid = (dist >= 0) & (dist < WINDOW)
        qpos = (q_start + row).astype(F32)
        ps, tails = [], []
        for g in range(B_GROUP):
            h = kv * B_GROUP + g
            sink = sinks_ref[h] + slopes_ref[h] * qpos
            s = jnp.where(valid, s_all[g * SWA_BLOCK:(g + 1) * SWA_BLOCK], -jnp.inf)
            s0, s1 = s[:, 0:LANES], s[:, LANES:2 * LANES]
            mx = jnp.maximum(jnp.max(jnp.maximum(s0, s1), axis=1, keepdims=True), sink)
            ps.append(jnp.concatenate([jnp.exp2(s0 - mx).astype(BF16), jnp.exp2(s1 - mx).astype(BF16)], axis=1))
            tails.append(jnp.exp2(sink - mx))
        pv = _dot(jnp.concatenate(ps, axis=0), vw)
        outs = []
        for g in range(B_GROUP):
            blk = pv[g * SWA_BLOCK:(g + 1) * SWA_BLOCK]
            outs.append(blk[:, 0:LANES] / (blk[:, LANES:2 * LANES] + tails[g]))
        for h2 in range(B_GROUP // 2):
            o_ref[pl.ds(q_start, SWA_BLOCK), h2 * LANES:(h2 + 1) * LANES] = (
                jnp.where(lane < HEAD_DIM, outs[2 * h2], outs[2 * h2 + 1]).astype(o_ref.dtype))

    def body(i, carry):
        for n in range(unroll):
            block(pl.multiple_of((i * unroll + n) * SWA_BLOCK, SWA_BLOCK))
        return carry

    lax.fori_loop(0, seq // (SWA_BLOCK * unroll), body, 0)


def _swa_call(proj, slopes2, sinks2, pos_aug, *, unroll=8):
    B, S, _ = proj.shape
    kern = functools.partial(_swa_kernel, seq=S, unroll=unroll)
    qw = B_GROUP * LANES
    return pl.pallas_call(
        kern,
        grid=(B, B_KV_HEADS),
        in_specs=[
            pl.BlockSpec(memory_space=pltpu.SMEM),
            pl.BlockSpec(memory_space=pltpu.SMEM),
            pl.BlockSpec((None, S, qw), lambda b, kv: (b, 0, P_QB // B_GROUP + kv)),
            pl.BlockSpec((None, S, LANES), lambda b, kv: (b, 0, P_KB)),
            pl.BlockSpec((None, S, LANES), lambda b, kv: (b, 0, P_VB + kv)),
            pl.BlockSpec((S, LANES), lambda b, kv: (0, 0)),
        ],
        out_specs=pl.BlockSpec((None, S, 2 * LANES), lambda b, kv: (b, 0, kv)),
        out_shape=jax.ShapeDtypeStruct((B, S, BRANCH_WIDTH), BF16),
        scratch_shapes=[pltpu.VMEM((B_GROUP * SWA_BLOCK, LANES), BF16)],
        compiler_params=_cparams(("parallel", "parallel")),
        name="swa",
    )(slopes2, sinks2, proj, proj, proj, pos_aug)


def _post_kernel(x_ref, g_ref, fg_ref, sg_ref, ya_ref, yb_ref, yc_ref, wg_ref, wgm_ref, wua_ref, wub_ref, wuc_ref,
                 wo_ref, o_ref, *, final):
    x = x_ref[...]
    h = _rms(x, g_ref[...], RMS_EPS).astype(BF16)
    hw = D_MODEL // 2
    ts = []
    for br, y_ref in enumerate((ya_ref, yb_ref, yc_ref)):
        g = _dot(h, wg_ref[:, br * BRANCH_WIDTH:(br + 1) * BRANCH_WIDTH])
        y = y_ref[...].astype(F32)
        if br == 0:
            y = jnp.concatenate([_rms(y[:, c0:c0 + LANES], sg_ref[...], SUBLN_EPS)
                                 for c0 in range(0, BRANCH_WIDTH, LANES)], axis=1)
        ts.append((y * (g * jax.nn.sigmoid(g))).astype(BF16))
    merged = [None, None]
    for br, wu_ref in enumerate((wua_ref, wub_ref, wuc_ref)):
        for j in range(2):
            gate = jax.nn.sigmoid(_dot(h, wgm_ref[:, br * D_MODEL + j * hw:br * D_MODEL + (j + 1) * hw]))
            u = _dot(ts[br], wu_ref[:, j * hw:(j + 1) * hw])
            merged[j] = gate * u if merged[j] is None else merged[j] + gate * u
    mb = jnp.concatenate(merged, axis=1).astype(BF16)
    for j in range(2):
        o_ref[:, j * hw:(j + 1) * hw] = x[:, j * hw:(j + 1) * hw] + _dot(mb, wo_ref[:, j * hw:(j + 1) * hw])
    if final:
        o_ref[...] = _rms(o_ref[...], fg_ref[...], RMS_EPS)


def _post_call(x, norm_gain, final_gain, sub_gain, ya, yb, yc, w_g, w_gm, w_ua, w_ub, w_uc, w_o, layer, final, *,
               tm=1024):
    B, S, D = x.shape
    kern = functools.partial(_post_kernel, final=final)
    tok = lambda w: pl.BlockSpec((None, tm, w), lambda b, i: (b, i, 0))
    lay = lambda r, c: pl.BlockSpec((None, r, c), lambda b, i: (layer, 0, 0), pipeline_mode=pl.Buffered(1))
    return pl.pallas_call(
        kern,
        grid=(B, S // tm),
        in_specs=[
            tok(D), lay(1, D), pl.BlockSpec((1, D), lambda b, i: (0, 0)), pl.BlockSpec((1, LANES), lambda b, i: (0, 0)),
            tok(BRANCH_WIDTH), tok(BRANCH_WIDTH), tok(BRANCH_WIDTH),
            lay(D, N_BRANCH * BRANCH_WIDTH), lay(D, N_BRANCH * D),
            lay(BRANCH_WIDTH, D), lay(BRANCH_WIDTH, D), lay(BRANCH_WIDTH, D), lay(D, D),
        ],
        out_specs=tok(D),
        out_shape=jax.ShapeDtypeStruct((B, S, D), F32),
        compiler_params=_cparams(("parallel", "parallel")),
        name="post",
    )(x, norm_gain, final_gain, sub_gain, ya, yb, yc, w_g, w_gm, w_ua, w_ub, w_uc, w_o)


def _alibi_slopes(n_heads):
    return 2.0 ** (-8.0 * jnp.arange(1, n_heads + 1, dtype=F32) / n_heads)


def _prepare_weights(w_in):
    col_scale = np.ones((_D_IN,), np.float32)
    for o in (O_QA, O_QB, O_QC):
        col_scale[o:o + 512] = HEAD_DIM ** -0.5 * LOG2E
    wb = (w_in * col_scale).astype(BF16)
    sl = lambda o, n: wb[:, :, o:o + n]
    zeros64 = jnp.zeros(wb.shape[:2] + (HEAD_DIM,), BF16)
    qb_blocks = []
    for n in range(B_HEADS):
        qn = sl(O_QB + n * HEAD_DIM, HEAD_DIM)
        qb_blocks += [qn, zeros64] if n // B_GROUP == 0 else [zeros64, qn]
    v0, v1 = sl(O_VB, HEAD_DIM), sl(O_VB + HEAD_DIM, HEAD_DIM)
    w_pre = jnp.concatenate(
        [sl(O_QA, 512), sl(O_KA, 512), sl(O_VA, 512), sl(O_QC, 512), sl(O_KC, 512), sl(O_VC, 512)]
        + qb_blocks + [sl(O_KB, 128), v0, v0, v1, v1], axis=-1)
    wf = jnp.pad(jnp.tile(sl(O_FC, C_HEADS), (1, 1, N_PIECES)),
                 ((0, 0), (0, 0), (0, LANES - N_PIECES * C_HEADS)))
    w_g = jnp.concatenate([sl(O_GA, 512), sl(O_GB, 512), sl(O_GC, 512)], axis=-1)
    w_gm = sl(O_GM, N_BRANCH * D_MODEL)
    return w_pre, wf, w_g, w_gm


def _placement():
    place = np.zeros((LANES, N_PAIRS * LANES), np.float32)
    for x in range(N_PIECES):
        for h in range(C_HEADS):
            place[C_HEADS * x + h, (h // 2) * LANES + N_PIECES * (h % 2) + x] = 1.0
    return jnp.asarray(place, BF16)


def _alibi_pieces(n_heads, seq):
    slopes = 2.0 ** (-8.0 * np.arange(1, n_heads + 1, dtype=np.float64) / n_heads)
    rest = (slopes[:, None] * (LOG2E * np.arange(seq, dtype=np.float64))[None, :]).astype(np.float32)
    pieces = np.zeros((n_heads, seq, N_PIECES), np.float32)
    for x in range(N_PIECES):
        pieces[:, :, x] = rest.astype(BF16).astype(np.float32)
        rest = rest - pieces[:, :, x]
    return pieces


def _alibi_aug(seq):
    table = np.zeros((A_HEADS, seq, LANES), np.float32)
    table[:, :, 0:N_PIECES] = _alibi_pieces(A_HEADS, seq)
    return jnp.asarray(table, BF16)


def _swa_pos_aug(seq):
    table = np.zeros((seq, LANES), np.float32)
    pieces = _alibi_pieces(B_HEADS, seq)
    for h in range(B_HEADS):
        table[:, N_PIECES * h:N_PIECES * (h + 1)] = pieces[h]
    return jnp.asarray(table, BF16)


def kernel(x, norm_gain, w_in, b_forget, lambda_q1, lambda_k1, lambda_q2, lambda_k2, subln_gain, sinks,
           w_up_a, w_up_b, w_up_c, w_o, final_gain):
    B, S, D = x.shape
    depth = w_in.shape[0]
    w_pre, wf, w_g, w_gm = _prepare_weights(w_in)
    w_ua, w_ub, w_uc, w_ob = (w.astype(BF16) for w in (w_up_a, w_up_b, w_up_c, w_o))
    gain3 = norm_gain.reshape(depth, 1, D)
    fgain = final_gain.reshape(1, D)
    bf3 = jnp.pad(jnp.tile(b_forget.astype(F32), (1, N_PIECES)),
                  ((0, 0), (0, LANES - N_PIECES * C_HEADS))).reshape(depth, 1, LANES)
    place = _placement()
    alibi_aug = _alibi_aug(S)
    slopes2_b = _alibi_slopes(B_HEADS) * LOG2E
    pos_aug = _swa_pos_aug(S)
    pad = lambda v: jnp.pad(v.astype(F32), (0, LANES - v.shape[0]))

    for l in range(depth):
        lam_init = 0.8 - 0.6 * math.exp(-0.3 * l)
        proj = _pre_call(x, gain3, w_pre, wf, bf3, place, l)
        consts = jnp.array([lam_init], F32)
        lam_rows = jnp.zeros((8, LANES), F32).at[0:4].set(
            jnp.stack([pad(lambda_q1[l]), pad(lambda_k1[l]), pad(lambda_q2[l]), pad(lambda_k2[l])]))
        ya = _flash_call("diff", proj, P_QA, P_KA, P_VA, (consts, alibi_aug, lam_rows))
        yc = _flash_call("fox", proj, P_QC, P_KC, P_VC)
        yb = _swa_call(proj, slopes2_b, sinks[l].astype(F32) * LOG2E, pos_aug)
        sub_gain = (subln_gain[l].astype(F32) * (1.0 - lam_init)).reshape(1, LANES)
        x = _post_call(x, gain3, fgain, sub_gain, ya, yb, yc, w_g, w_gm, w_ua, w_ub, w_uc, w_ob, l, l == depth - 1)
    return x
```

```python
import functools
import math

import numpy as np
import jax
import jax.numpy as jnp
from jax import lax
from jax.experimental import pallas as pl
from jax.experimental.pallas import tpu as pltpu

F32 = jnp.float32
BF16 = jnp.bfloat16

D_MODEL = 1024
HEAD_DIM = 64
LANES = 128
RMS_EPS = 1e-6
SUBLN_EPS = 1e-5
A_HEADS = 4
B_HEADS = 8
B_KV_HEADS = 2
B_GROUP = B_HEADS // B_KV_HEADS
C_HEADS = 8
WINDOW = 128
SWA_BLOCK = 128
N_BRANCH = 3
BRANCH_WIDTH = 512
N_PAIRS = 4
N_PIECES = 3
LOG2E = math.log2(math.e)

_SPLITS = (512, 512, 512, 512, 512, 128, 128, 512, 512, 512, 512, 8, 512, 3 * D_MODEL)
_OFF = [0]
for _s in _SPLITS:
    _OFF.append(_OFF[-1] + _s)
(O_QA, O_KA, O_VA, O_GA, O_QB, O_KB, O_VB, O_GB, O_QC, O_KC, O_VC, O_FC, O_GC, O_GM, _D_IN) = _OFF

P_QA, P_KA, P_VA = 0, 4, 8
P_QC, P_KC, P_VC = 12, 16, 20
P_QB = 24
P_KB = 28
P_VB = 30
P_MM = 31
P_AUG = 31
P_BLOCKS = 35
P_WIDTH = P_BLOCKS * LANES
MM_WIDTH = P_MM * LANES

VMEM_LIMIT = 56 * 1024 * 1024


def _cparams(sem):
    return pltpu.CompilerParams(dimension_semantics=sem, vmem_limit_bytes=VMEM_LIMIT)


def _rms(x, gain, eps):
    ms = jnp.mean(x * x, axis=-1, keepdims=True)
    return x * lax.rsqrt(ms + eps) * gain


def _dot(a, b):
    return jnp.dot(a, b, preferred_element_type=F32)


def _dot_nt(a, b):
    return lax.dot_general(a, b, (((1,), (1,)), ((), ())), preferred_element_type=F32)


def _split3(v):
    hi = v.astype(BF16)
    r = v - hi.astype(F32)
    mid = r.astype(BF16)
    lo = (r - mid.astype(F32)).astype(BF16)
    return hi, mid, lo


def _pre_kernel(x_ref, g_ref, w_ref, wf_ref, bf_ref, place_ref, proj_ref, carry_ref, *, tm, chunk):
    i = pl.program_id(1)

    @pl.when(i == 0)
    def _():
        carry_ref[...] = jnp.zeros_like(carry_ref)

    h = _rms(x_ref[...], g_ref[...], RMS_EPS).astype(BF16)

    lane = lax.broadcasted_iota(jnp.int32, (tm, LANES), 1)

    def pieces_by_group(v):
        p0, p1, p2 = (p.astype(F32) for p in _split3(v))
        by_group = jnp.where(lane < C_HEADS, p0, jnp.where(lane < 2 * C_HEADS, p1, p2))
        return jnp.where(lane < N_PIECES * C_HEADS, by_group, 0.0).astype(BF16)

    starts = list(range(0, MM_WIDTH, chunk))

    def project(first, last):
        for c0 in starts[first:last]:
            c1 = min(c0 + chunk, MM_WIDTH)
            proj_ref[:, c0:c1] = _dot(h, w_ref[:, c0:c1]).astype(BF16)

    n1, n2, n3 = len(starts) // 4, len(starts) // 2, 3 * len(starts) // 4
    project(0, n1)
    z = _dot(h, wf_ref[...]) + bf_ref[...]
    logf = jnp.minimum(z, 0.0) - jnp.log1p(jnp.exp(-jnp.abs(z)))
    project(n1, n2)
    row = lax.broadcasted_iota(jnp.int32, (tm, tm), 0)
    col = lax.broadcasted_iota(jnp.int32, (tm, tm), 1)
    lower = (col <= row).astype(BF16)
    part = _dot(lower, pieces_by_group(logf))
    total = part + pltpu.roll(part, LANES - C_HEADS, 1) + pltpu.roll(part, LANES - 2 * C_HEADS, 1)
    total = jnp.where(lane < C_HEADS, total, 0.0)
    c = total + pltpu.roll(total, C_HEADS, 1) + pltpu.roll(total, 2 * C_HEADS, 1) + carry_ref[0:1, :]
    carry_ref[...] = jnp.broadcast_to(c[tm - 1:tm, :], carry_ref.shape)
    project(n2, n3)
    proj_ref[:, MM_WIDTH:P_WIDTH] = _dot(pieces_by_group(c * (-LOG2E)), place_ref[...]).astype(BF16)
    project(n3, len(starts))


def _pre_call(x, norm_gain, w_pre, wf, b_f, place, layer, *, tm=1024, chunk=512):
    B, S, D = x.shape
    kern = functools.partial(_pre_kernel, tm=tm, chunk=chunk)
    return pl.pallas_call(
        kern,
        grid=(B, S // tm),
        in_specs=[
            pl.BlockSpec((None, tm, D), lambda b, i: (b, i, 0)),
            pl.BlockSpec((None, 1, D), lambda b, i: (layer, 0, 0)),
            pl.BlockSpec((None, D, MM_WIDTH), lambda b, i: (layer, 0, 0), pipeline_mode=pl.Buffered(1)),
            pl.BlockSpec((None, D, LANES), lambda b, i: (layer, 0, 0)),
            pl.BlockSpec((None, 1, LANES), lambda b, i: (layer, 0, 0)),
            pl.BlockSpec((LANES, N_PAIRS * LANES), lambda b, i: (0, 0)),
        ],
        out_specs=pl.BlockSpec((None, tm, P_WIDTH), lambda b, i: (b, i, 0)),
        out_shape=jax.ShapeDtypeStruct((B, S, P_WIDTH), BF16),
        scratch_shapes=[pltpu.VMEM((8, LANES), F32)],
        compiler_params=_cparams(("parallel", "arbitrary")),
        name="pre",
    )(x, norm_gain, w_pre, wf, b_f, place)


def _chunk_order(nq):
    order = [(t, t) for t in range(nq)]
    order += [(qi, c) for qi in range(1, nq) for c in range(qi)]
    return order


def _flash_kernel(*refs, mode, tq, nq, unroll):
    if mode == "diff":
        (consts_ref, tab_ref, q_ref, k_ref, v_ref, aug_ref, lam_ref, o_ref,
         qe_ref, m_ref, acc_ref, sa_ref, sb_ref) = refs
    else:
        (tab_ref, q_ref, k_ref, v_ref, aug_ref, o_ref, qe_ref, m_ref, acc_ref, sa_ref, sb_ref) = refs
    tk = tq
    rows = 2 * tq
    half = tq // 2
    n_chunks = nq * (nq + 1) // 2
    lane = lax.broadcasted_iota(jnp.int32, (half, LANES), 1)

    def build_qe(t, carry):
        for hf in range(2):
            q = q_ref[pl.ds(pl.multiple_of(t * tq + hf * half, half), half), :]
            zero = jnp.zeros_like(q)
            for m in range(2):
                in_map = (lane >= HEAD_DIM) if m else (lane < HEAD_DIM)
                first = N_PIECES * m if mode == "fox" else 0
                ones_at_pieces = jnp.where(lane >= first, 1.0, 0.0) * jnp.where(lane < first + N_PIECES, 1.0, 0.0)
                g0 = (2 * hf + m) * half
                qe_ref[t, g0:g0 + half, 0:LANES] = jnp.where(in_map, q, zero)
                qe_ref[t, g0:g0 + half, LANES:2 * LANES] = ones_at_pieces.astype(BF16)
        return carry

    lax.fori_loop(0, nq, build_qe, 0)

    def keys(start, n):
        return jnp.concatenate([k_ref[pl.ds(start, n), :], aug_ref[pl.ds(start, n), :]], axis=1)

    def values(start, n):
        return jnp.concatenate([v_ref[pl.ds(start, n), :], jnp.ones((n, LANES), BF16)], axis=1)

    def qk(t, s_ref):
        qi = tab_ref[0, t]
        start = pl.multiple_of(tab_ref[1, t] * tk, tk)
        s_ref[...] = _dot_nt(qe_ref[qi], keys(start, tk))

    def qk_diagonal(t, s_ref):
        s_ref[0:tq, 0:half] = _dot_nt(qe_ref[t, 0:tq], keys(t * tk, half))
        s_ref[tq:rows, :] = _dot_nt(qe_ref[t, tq:rows], keys(t * tk, tk))

    def probabilities(s, m_prev):
        cols = [s[:, c0:c0 + LANES] for c0 in range(0, s.shape[1], LANES)]
        mx = cols[0]
        for sc in cols[1:]:
            mx = jnp.maximum(mx, sc)
        m_new = jnp.broadcast_to(jnp.max(mx, axis=1, keepdims=True), (s.shape[0], LANES))
        if m_prev is not None:
            m_new = jnp.maximum(m_prev, m_new)
        return m_new, jnp.concatenate([jnp.exp2(sc - m_new).astype(BF16) for sc in cols], axis=1)

    def softmax_pv(s_ref, t):
        qi = tab_ref[0, t]
        start = pl.multiple_of(tab_ref[1, t] * tk, tk)
        m_prev = m_ref[qi]
        m_new, p = probabilities(s_ref[...], m_prev)
        alpha = jnp.exp2(m_prev - m_new)
        pv = _dot(p, values(start, tk))
        acc_ref[qi, :, 0:LANES] = alpha * acc_ref[qi, :, 0:LANES] + pv[:, 0:LANES]
        acc_ref[qi, :, LANES:2 * LANES] = alpha * acc_ref[qi, :, LANES:2 * LANES] + pv[:, LANES:2 * LANES]
        m_ref[qi] = m_new

    def softmax_pv_diagonal(s_ref, t):
        for r0, n_keys in ((0, half), (tq, tk)):
            r = lax.broadcasted_iota(jnp.int32, (tq, n_keys), 0)
            c = lax.broadcasted_iota(jnp.int32, (tq, n_keys), 1)
            q_pos = jnp.where(r >= half, r - half, r) + (half if r0 else 0)
            s = jnp.where(c <= q_pos, s_ref[r0:r0 + tq, 0:n_keys], -jnp.inf)
            m_new, p = probabilities(s, None)
            acc_ref[t, r0:r0 + tq, :] = _dot(p, values(t * tk, n_keys))
            m_ref[t, r0:r0 + tq, :] = m_new

    s_refs = (sa_ref, sb_ref)

    qk_diagonal(0, sa_ref)
    for t in range(nq):
        if t + 1 < nq:
            qk_diagonal(t + 1, s_refs[(t + 1) % 2])
        else:
            qk(t + 1, s_refs[(t + 1) % 2])
        softmax_pv_diagonal(s_refs[t % 2], t)

    def run(first, last, unroll):
        def body(i, carry):
            for u in range(unroll):
                t = first + i * unroll + u
                qk(t + 1, s_refs[(u + 1) % 2])
                softmax_pv(s_refs[u % 2], t)
            return carry
        lax.fori_loop(0, (last - first) // unroll, body, 0)

    main = nq + (n_chunks - nq) // unroll * unroll
    if main > nq:
        run(nq, main, unroll)
    if n_chunks > main:
        run(main, n_chunks, n_chunks - main)

    def finalize(t):
        for hf in range(2):
            a0 = acc_ref[t, (2 * hf) * half:(2 * hf + 1) * half, :]
            a1 = acc_ref[t, (2 * hf + 1) * half:(2 * hf + 2) * half, :]
            o0 = a0[:, 0:LANES] / a0[:, LANES:2 * LANES]
            o1 = a1[:, 0:LANES] / a1[:, LANES:2 * LANES]
            if mode == "diff":
                lam_init = consts_ref[0]
                lp = lam_ref[...]
                lam = (jnp.exp(jnp.sum(lp[0:1] * lp[1:2], axis=1, keepdims=True))
                       - jnp.exp(jnp.sum(lp[2:3] * lp[3:4], axis=1, keepdims=True)) + lam_init)
                y = o0 - lam * o1
            else:
                y = jnp.where(lane < HEAD_DIM, o0, o1)
            o_ref[t * tq + hf * half:t * tq + (hf + 1) * half, :] = y.astype(o_ref.dtype)

    for t in range(nq):
        finalize(t)


def _flash_call(mode, proj, q_blk, k_blk, v_blk, extra=(), *, tq=512, unroll=14):
    B, S, _ = proj.shape
    nq = S // tq
    order = _chunk_order(nq)
    assert nq % 2 == 0 and len(order) % 2 == 0, "the chunk loop handles two chunks per trip"
    table = jnp.asarray(np.array(order + [order[-1]], np.int32).T)
    kern = functools.partial(_flash_kernel, mode=mode, tq=tq, nq=nq, unroll=unroll)
    seq = lambda blk: pl.BlockSpec((None, S, LANES), lambda b, p: (b, 0, blk + p))
    smem = pl.BlockSpec(memory_space=pltpu.SMEM)
    in_specs = [smem, seq(q_blk), seq(k_blk), seq(v_blk)]
    if mode == "diff":
        consts, alibi_aug, lam_rows = extra
        in_specs = [smem] + in_specs + [
            pl.BlockSpec((None, S, LANES), lambda b, p: (p, 0, 0)),
            pl.BlockSpec((8, LANES), lambda b, p: (0, 0)),
        ]
        args = (consts, table, proj, proj, proj, alibi_aug, lam_rows)
    else:
        in_specs = in_specs + [seq(P_AUG)]
        args = (table, proj, proj, proj, proj)
    return pl.pallas_call(
        kern,
        grid=(B, N_PAIRS),
        in_specs=in_specs,
        out_specs=pl.BlockSpec((None, S, LANES), lambda b, p: (b, 0, p)),
        out_shape=jax.ShapeDtypeStruct((B, S, BRANCH_WIDTH), BF16),
        scratch_shapes=[
            pltpu.VMEM((nq, 2 * tq, 2 * LANES), BF16),
            pltpu.VMEM((nq, 2 * tq, LANES), F32),
            pltpu.VMEM((nq, 2 * tq, 2 * LANES), F32),
            pltpu.VMEM((2 * tq, tq), F32),
            pltpu.VMEM((2 * tq, tq), F32),
        ],
        compiler_params=_cparams(("parallel", "parallel")),
        name="flash_" + mode,
    )(*args)


def _swa_kernel(slopes_ref, sinks_ref, q_ref, k_ref, v_ref, pos_ref, o_ref, e_ref, *, seq, unroll):
    kv = pl.program_id(1)
    lane = lax.broadcasted_iota(jnp.int32, (SWA_BLOCK, LANES), 1)
    row = lax.broadcasted_iota(jnp.int32, (SWA_BLOCK, LANES), 0)
    r = lax.broadcasted_iota(jnp.int32, (SWA_BLOCK, 2 * SWA_BLOCK), 0)
    c = lax.broadcasted_iota(jnp.int32, (SWA_BLOCK, 2 * SWA_BLOCK), 1)
    for g in range(B_GROUP):
        first = N_PIECES * (kv * B_GROUP + g)
        ones_at_pieces = jnp.where(lane >= first, 1.0, 0.0) * jnp.where(lane < first + N_PIECES, 1.0, 0.0)
        e_ref[g * SWA_BLOCK:(g + 1) * SWA_BLOCK, :] = ones_at_pieces.astype(BF16)
    ones = jnp.ones((2 * SWA_BLOCK, LANES), BF16)

    def block(q_start):
        k_start = pl.multiple_of(jnp.maximum(q_start - SWA_BLOCK, 0), SWA_BLOCK)
        kw = jnp.concatenate([k_ref[pl.ds(k_start, 2 * SWA_BLOCK), :],
                              pos_ref[pl.ds(k_start, 2 * SWA_BLOCK), :]], axis=1)
        vw = jnp.concatenate([v_ref[pl.ds(k_start, 2 * SWA_BLOCK), :], ones], axis=1)
        heads = []
        for j in range(B_GROUP // 2):
            pair = q_ref[pl.ds(q_start, SWA_BLOCK), j * LANES:(j + 1) * LANES]
            zero = jnp.zeros_like(pair)
            heads += [jnp.where(lane < HEAD_DIM, pair, zero), jnp.where(lane >= HEAD_DIM, pair, zero)]
        qs = jnp.concatenate(heads, axis=0)
        s_all = _dot_nt(jnp.concatenate([qs, e_ref[...]], axis=1), kw)
        dist = (q_start - k_start) + r - c
        valid = (dist >= 0) & (dist < WINDOW)
        qpos = (q_start + row).astype(F32)
        ps, tails = [], []
        for g in range(B_GROUP):
            h = kv * B_GROUP + g
            sink = sinks_ref[h] + slopes_ref[h] * qpos
            s = jnp.where(valid, s_all[g * SWA_BLOCK:(g + 1) * SWA_BLOCK], -jnp.inf)
            s0, s1 = s[:, 0:LANES], s[:, LANES:2 * LANES]
            mx = jnp.maximum(jnp.max(jnp.maximum(s0, s1), axis=1, keepdims=True), sink)
            ps.append(jnp.concatenate([jnp.exp2(s0 - mx).astype(BF16), jnp.exp2(s1 - mx).astype(BF16)], axis=1))
            tails.append(jnp.exp2(sink - mx))
        pv = _dot(jnp.concatenate(ps, axis=0), vw)
        outs = []
        for g in range(B_GROUP):
            blk = pv[g * SWA_BLOCK:(g + 1) * SWA_BLOCK]
            outs.append(blk[:, 0:LANES] / (blk[:, LANES:2 * LANES] + tails[g]))
        first_kv = kv == 0
        for h2 in range(B_GROUP // 2):
            oa, ob = outs[2 * h2], outs[2 * h2 + 1]
            moved = pltpu.roll(jnp.where(first_kv, ob, oa), HEAD_DIM, 1)
            o_ref[pl.ds(q_start, SWA_BLOCK), h2 * LANES:(h2 + 1) * LANES] = jnp.where(
                lane < HEAD_DIM, jnp.where(first_kv, oa, moved), jnp.where(first_kv, moved, ob)).astype(o_ref.dtype)

    def body(i, carry):
        for n in range(unroll):
            block(pl.multiple_of((i * unroll + n) * SWA_BLOCK, SWA_BLOCK))
        return carry

    lax.fori_loop(0, seq // (SWA_BLOCK * unroll), body, 0)


def _swa_call(proj, slopes2, sinks2, pos_aug, *, unroll=8):
    B, S, _ = proj.shape
    kern = functools.partial(_swa_kernel, seq=S, unroll=unroll)
    qw = B_GROUP * HEAD_DIM
    return pl.pallas_call(
        kern,
        grid=(B, B_KV_HEADS),
        in_specs=[
            pl.BlockSpec(memory_space=pltpu.SMEM),
            pl.BlockSpec(memory_space=pltpu.SMEM),
            pl.BlockSpec((None, S, qw), lambda b, kv: (b, 0, P_QB * LANES // qw + kv)),
            pl.BlockSpec((None, S, LANES), lambda b, kv: (b, 0, P_KB + kv)),
            pl.BlockSpec((None, S, LANES), lambda b, kv: (b, 0, P_VB)),
            pl.BlockSpec((S, LANES), lambda b, kv: (0, 0)),
        ],
        out_specs=pl.BlockSpec((None, S, 2 * LANES), lambda b, kv: (b, 0, kv)),
        out_shape=jax.ShapeDtypeStruct((B, S, BRANCH_WIDTH), BF16),
        scratch_shapes=[pltpu.VMEM((B_GROUP * SWA_BLOCK, LANES), BF16)],
        compiler_params=_cparams(("parallel", "parallel")),
        name="swa",
    )(slopes2, sinks2, proj, proj, proj, pos_aug)


def _post_kernel(x_ref, g_ref, fg_ref, sg_ref, ya_ref, yb_ref, yc_ref, wg_ref, wgm_ref, wua_ref, wub_ref, wuc_ref,
                 wo_ref, o_ref, *, final):
    x = x_ref[...]
    h = _rms(x, g_ref[...], RMS_EPS).astype(BF16)
    hw = D_MODEL // 2
    ts = []
    for br, y_ref in enumerate((ya_ref, yb_ref, yc_ref)):
        g = _dot(h, wg_ref[:, br * BRANCH_WIDTH:(br + 1) * BRANCH_WIDTH])
        y = y_ref[...].astype(F32)
        if br == 0:
            y = jnp.concatenate([_rms(y[:, c0:c0 + LANES], sg_ref[...], SUBLN_EPS)
                                 for c0 in range(0, BRANCH_WIDTH, LANES)], axis=1)
        ts.append((y * (g * jax.nn.sigmoid(g))).astype(BF16))
    merged = [None, None]
    for br, wu_ref in enumerate((wua_ref, wub_ref, wuc_ref)):
        for j in range(2):
            gate = jax.nn.sigmoid(_dot(h, wgm_ref[:, br * D_MODEL + j * hw:br * D_MODEL + (j + 1) * hw]))
            u = _dot(ts[br], wu_ref[:, j * hw:(j + 1) * hw])
            merged[j] = gate * u if merged[j] is None else merged[j] + gate * u
    mb = jnp.concatenate(merged, axis=1).astype(BF16)
    for j in range(2):
        o_ref[:, j * hw:(j + 1) * hw] = x[:, j * hw:(j + 1) * hw] + _dot(mb, wo_ref[:, j * hw:(j + 1) * hw])
    if final:
        o_ref[...] = _rms(o_ref[...], fg_ref[...], RMS_EPS)


def _post_call(x, norm_gain, final_gain, sub_gain, ya, yb, yc, w_g, w_gm, w_ua, w_ub, w_uc, w_o, layer, final, *,
               tm=1024):
    B, S, D = x.shape
    kern = functools.partial(_post_kernel, final=final)
    tok = lambda w: pl.BlockSpec((None, tm, w), lambda b, i: (b, i, 0))
    lay = lambda r, c: pl.BlockSpec((None, r, c), lambda b, i: (layer, 0, 0), pipeline_mode=pl.Buffered(1))
    return pl.pallas_call(
        kern,
        grid=(B, S // tm),
        in_specs=[
            tok(D), lay(1, D), pl.BlockSpec((1, D), lambda b, i: (0, 0)), pl.BlockSpec((1, LANES), lambda b, i: (0, 0)),
            tok(BRANCH_WIDTH), tok(BRANCH_WIDTH), tok(BRANCH_WIDTH),
            lay(D, N_BRANCH * BRANCH_WIDTH), lay(D, N_BRANCH * D),
            lay(BRANCH_WIDTH, D), lay(BRANCH_WIDTH, D), lay(BRANCH_WIDTH, D), lay(D, D),
        ],
        out_specs=tok(D),
        out_shape=jax.ShapeDtypeStruct((B, S, D), F32),
        compiler_params=_cparams(("parallel", "parallel")),
        name="post",
    )(x, norm_gain, final_gain, sub_gain, ya, yb, yc, w_g, w_gm, w_ua, w_ub, w_uc, w_o)


def _alibi_slopes(n_heads):
    return 2.0 ** (-8.0 * jnp.arange(1, n_heads + 1, dtype=F32) / n_heads)


def _prepare_weights(w_in):
    col_scale = np.ones((_D_IN,), np.float32)
    for o in (O_QA, O_QB, O_QC):
        col_scale[o:o + 512] = HEAD_DIM ** -0.5 * LOG2E
    wb = (w_in * col_scale).astype(BF16)
    sl = lambda o, n: wb[:, :, o:o + n]
    k0, k1 = sl(O_KB, HEAD_DIM), sl(O_KB + HEAD_DIM, HEAD_DIM)
    w_pre = jnp.concatenate(
        [sl(O_QA, 512), sl(O_KA, 512), sl(O_VA, 512), sl(O_QC, 512), sl(O_KC, 512), sl(O_VC, 512),
         sl(O_QB, 512), k0, k0, k1, k1, sl(O_VB, 128)], axis=-1)
    wf = jnp.pad(jnp.tile(sl(O_FC, C_HEADS), (1, 1, N_PIECES)),
                 ((0, 0), (0, 0), (0, LANES - N_PIECES * C_HEADS)))
    w_g = jnp.concatenate([sl(O_GA, 512), sl(O_GB, 512), sl(O_GC, 512)], axis=-1)
    w_gm = sl(O_GM, N_BRANCH * D_MODEL)
    return w_pre, wf, w_g, w_gm


def _placement():
    place = np.zeros((LANES, N_PAIRS * LANES), np.float32)
    for x in range(N_PIECES):
        for h in range(C_HEADS):
            place[C_HEADS * x + h, (h // 2) * LANES + N_PIECES * (h % 2) + x] = 1.0
    return jnp.asarray(place, BF16)


def _alibi_pieces(n_heads, seq):
    slopes = 2.0 ** (-8.0 * np.arange(1, n_heads + 1, dtype=np.float64) / n_heads)
    rest = (slopes[:, None] * (LOG2E * np.arange(seq, dtype=np.float64))[None, :]).astype(np.float32)
    pieces = np.zeros((n_heads, seq, N_PIECES), np.float32)
    for x in range(N_PIECES):
        pieces[:, :, x] = rest.astype(BF16).astype(np.float32)
        rest = rest - pieces[:, :, x]
    return pieces


def _alibi_aug(seq):
    table = np.zeros((A_HEADS, seq, LANES), np.float32)
    table[:, :, 0:N_PIECES] = _alibi_pieces(A_HEADS, seq)
    return jnp.asarray(table, BF16)


def _swa_pos_aug(seq):
    table = np.zeros((seq, LANES), np.float32)
    pieces = _alibi_pieces(B_HEADS, seq)
    for h in range(B_HEADS):
        table[:, N_PIECES * h:N_PIECES * (h + 1)] = pieces[h]
    return jnp.asarray(table, BF16)


def kernel(x, norm_gain, w_in, b_forget, lambda_q1, lambda_k1, lambda_q2, lambda_k2, subln_gain, sinks,
           w_up_a, w_up_b, w_up_c, w_o, final_gain):
    B, S, D = x.shape
    depth = w_in.shape[0]
    w_pre, wf, w_g, w_gm = _prepare_weights(w_in)
    w_ua, w_ub, w_uc, w_ob = (w.astype(BF16) for w in (w_up_a, w_up_b, w_up_c, w_o))
    gain3 = norm_gain.reshape(depth, 1, D)
    fgain = final_gain.reshape(1, D)
    bf3 = jnp.pad(jnp.tile(b_forget.astype(F32), (1, N_PIECES)),
                  ((0, 0), (0, LANES - N_PIECES * C_HEADS))).reshape(depth, 1, LANES)
    place = _placement()
    alibi_aug = _alibi_aug(S)
    slopes2_b = _alibi_slopes(B_HEADS) * LOG2E
    pos_aug = _swa_pos_aug(S)
    pad = lambda v: jnp.pad(v.astype(F32), (0, LANES - v.shape[0]))

    for l in range(depth):
        lam_init = 0.8 - 0.6 * math.exp(-0.3 * l)
        proj = _pre_call(x, gain3, w_pre, wf, bf3, place, l)
        consts = jnp.array([lam_init], F32)
        lam_rows = jnp.zeros((8, LANES), F32).at[0:4].set(
            jnp.stack([pad(lambda_q1[l]), pad(lambda_k1[l]), pad(lambda_q2[l]), pad(lambda_k2[l])]))
        ya = _flash_call("diff", proj, P_QA, P_KA, P_VA, (consts, alibi_aug, lam_rows))
        yc = _flash_call("fox", proj, P_QC, P_KC, P_VC)
        yb = _swa_call(proj, slopes2_b, sinks[l].astype(F32) * LOG2E, pos_aug)
        sub_gain = (subln_gain[l].astype(F32) * (1.0 - lam_init)).reshape(1, LANES)
        x = _post_call(x, gain3, fgain, sub_gain, ya, yb, yc, w_g, w_gm, w_ua, w_ub, w_uc, w_ob, l, l == depth - 1)
    return x
```

```python
import functools
import math

import numpy as np
import jax
import jax.numpy as jnp
from jax import lax
from jax.experimental import pallas as pl
from jax.experimental.pallas import tpu as pltpu

F32 = jnp.float32
BF16 = jnp.bfloat16

D_MODEL = 1024
HEAD_DIM = 64
LANES = 128
RMS_EPS = 1e-6
SUBLN_EPS = 1e-5
A_HEADS = 4
B_HEADS = 8
B_KV_HEADS = 2
B_GROUP = B_HEADS // B_KV_HEADS
C_HEADS = 8
WINDOW = 128
SWA_BLOCK = 128
N_BRANCH = 3
BRANCH_WIDTH = 512
N_PAIRS = 4
N_PIECES = 3
LOG2E = math.log2(math.e)

_SPLITS = (512, 512, 512, 512, 512, 128, 128, 512, 512, 512, 512, 8, 512, 3 * D_MODEL)
_OFF = [0]
for _s in _SPLITS:
    _OFF.append(_OFF[-1] + _s)
(O_QA, O_KA, O_VA, O_GA, O_QB, O_KB, O_VB, O_GB, O_QC, O_KC, O_VC, O_FC, O_GC, O_GM, _D_IN) = _OFF

P_QA, P_KA, P_VA = 0, 4, 8
P_QC, P_KC, P_VC = 12, 16, 20
P_QB = 24
P_KB = 28
P_VB = 30
P_MM = 31
P_AUG = 31
P_BLOCKS = 35
P_WIDTH = P_BLOCKS * LANES
MM_WIDTH = P_MM * LANES

VMEM_LIMIT = 56 * 1024 * 1024

TOKEN_TILE = 1024
PROJ_CHUNK = 512
FLASH_TILE = 512
FLASH_UNROLL = 14
SWA_UNROLL = 8


def _cparams(sem):
    return pltpu.CompilerParams(dimension_semantics=sem, vmem_limit_bytes=VMEM_LIMIT)


def _rms(x, gain, eps):
    ms = jnp.mean(x * x, axis=-1, keepdims=True)
    return x * lax.rsqrt(ms + eps) * gain


def _dot(a, b):
    return jnp.dot(a, b, preferred_element_type=F32)


def _dot_nt(a, b):
    return lax.dot_general(a, b, (((1,), (1,)), ((), ())), preferred_element_type=F32)


def _split3(v):
    hi = v.astype(BF16)
    r = v - hi.astype(F32)
    mid = r.astype(BF16)
    lo = (r - mid.astype(F32)).astype(BF16)
    return hi, mid, lo


def _pre_kernel(x_ref, g_ref, w_ref, wf_ref, bf_ref, place_ref, proj_ref, carry_ref, *, tm, chunk):
    i = pl.program_id(1)

    @pl.when(i == 0)
    def _():
        carry_ref[...] = jnp.zeros_like(carry_ref)

    h = _rms(x_ref[...], g_ref[...], RMS_EPS).astype(BF16)

    lane = lax.broadcasted_iota(jnp.int32, (tm, LANES), 1)

    def pieces_by_group(v):
        p0, p1, p2 = (p.astype(F32) for p in _split3(v))
        by_group = jnp.where(lane < C_HEADS, p0, jnp.where(lane < 2 * C_HEADS, p1, p2))
        return jnp.where(lane < N_PIECES * C_HEADS, by_group, 0.0).astype(BF16)

    starts = list(range(0, MM_WIDTH, chunk))

    def project(first, last):
        for c0 in starts[first:last]:
            c1 = min(c0 + chunk, MM_WIDTH)
            proj_ref[:, c0:c1] = _dot(h, w_ref[:, c0:c1]).astype(BF16)

    n1, n2, n3 = len(starts) // 4, len(starts) // 2, 3 * len(starts) // 4
    project(0, n1)
    z = _dot(h, wf_ref[...]) + bf_ref[...]
    logf = jnp.minimum(z, 0.0) - jnp.log1p(jnp.exp(-jnp.abs(z)))
    project(n1, n2)
    row = lax.broadcasted_iota(jnp.int32, (tm, tm), 0)
    col = lax.broadcasted_iota(jnp.int32, (tm, tm), 1)
    lower = (col <= row).astype(BF16)
    part = _dot(lower, pieces_by_group(logf))
    total = part + pltpu.roll(part, LANES - C_HEADS, 1) + pltpu.roll(part, LANES - 2 * C_HEADS, 1)
    total = jnp.where(lane < C_HEADS, total, 0.0)
    c = total + pltpu.roll(total, C_HEADS, 1) + pltpu.roll(total, 2 * C_HEADS, 1) + carry_ref[0:1, :]
    carry_ref[...] = jnp.broadcast_to(c[tm - 1:tm, :], carry_ref.shape)
    project(n2, n3)
    proj_ref[:, MM_WIDTH:P_WIDTH] = _dot(pieces_by_group(c * (-LOG2E)), place_ref[...]).astype(BF16)
    project(n3, len(starts))


def _pre_call(x, norm_gain, w_pre, wf, b_f, place, layer, *, tm=TOKEN_TILE, chunk=PROJ_CHUNK):
    B, S, D = x.shape
    kern = functools.partial(_pre_kernel, tm=tm, chunk=chunk)
    return pl.pallas_call(
        kern,
        grid=(B, S // tm),
        in_specs=[
            pl.BlockSpec((None, tm, D), lambda b, i: (b, i, 0)),
            pl.BlockSpec((None, 1, D), lambda b, i: (layer, 0, 0)),
            pl.BlockSpec((None, D, MM_WIDTH), lambda b, i: (layer, 0, 0), pipeline_mode=pl.Buffered(1)),
            pl.BlockSpec((None, D, LANES), lambda b, i: (layer, 0, 0)),
            pl.BlockSpec((None, 1, LANES), lambda b, i: (layer, 0, 0)),
            pl.BlockSpec((LANES, N_PAIRS * LANES), lambda b, i: (0, 0)),
        ],
        out_specs=pl.BlockSpec((None, tm, P_WIDTH), lambda b, i: (b, i, 0)),
        out_shape=jax.ShapeDtypeStruct((B, S, P_WIDTH), BF16),
        scratch_shapes=[pltpu.VMEM((8, LANES), F32)],
        compiler_params=_cparams(("parallel", "arbitrary")),
        name="pre",
    )(x, norm_gain, w_pre, wf, b_f, place)


def _chunk_order(nq):
    order = [(t, t) for t in range(nq)]
    order += [(qi, c) for qi in range(1, nq) for c in range(qi)]
    return order


def _flash_kernel(*refs, mode, tq, nq, unroll):
    if mode == "diff":
        (consts_ref, tab_ref, q_ref, k_ref, v_ref, aug_ref, lam_ref, o_ref,
         qe_ref, m_ref, acc_ref, sa_ref, sb_ref) = refs
    else:
        (tab_ref, q_ref, k_ref, v_ref, aug_ref, o_ref, qe_ref, m_ref, acc_ref, sa_ref, sb_ref) = refs
    tk = tq
    rows = 2 * tq
    half = tq // 2
    n_chunks = nq * (nq + 1) // 2
    lane = lax.broadcasted_iota(jnp.int32, (half, LANES), 1)

    def build_qe(t, carry):
        for hf in range(2):
            q = q_ref[pl.ds(pl.multiple_of(t * tq + hf * half, half), half), :]
            zero = jnp.zeros_like(q)
            for m in range(2):
                in_map = (lane >= HEAD_DIM) if m else (lane < HEAD_DIM)
                first = N_PIECES * m if mode == "fox" else 0
                ones_at_pieces = jnp.where(lane >= first, 1.0, 0.0) * jnp.where(lane < first + N_PIECES, 1.0, 0.0)
                g0 = (2 * hf + m) * half
                qe_ref[t, g0:g0 + half, 0:LANES] = jnp.where(in_map, q, zero)
                qe_ref[t, g0:g0 + half, LANES:2 * LANES] = ones_at_pieces.astype(BF16)
        return carry

    lax.fori_loop(0, nq, build_qe, 0)

    def keys(start, n):
        return jnp.concatenate([k_ref[pl.ds(start, n), :], aug_ref[pl.ds(start, n), :]], axis=1)

    def values(start, n):
        return jnp.concatenate([v_ref[pl.ds(start, n), :], jnp.ones((n, LANES), BF16)], axis=1)

    def qk(t, s_ref):
        qi = tab_ref[0, t]
        start = pl.multiple_of(tab_ref[1, t] * tk, tk)
        s_ref[...] = _dot_nt(qe_ref[qi], keys(start, tk))

    def qk_diagonal(t, s_ref):
        s_ref[0:tq, 0:half] = _dot_nt(qe_ref[t, 0:tq], keys(t * tk, half))
        s_ref[tq:rows, :] = _dot_nt(qe_ref[t, tq:rows], keys(t * tk, tk))

    def probabilities(s, m_prev):
        cols = [s[:, c0:c0 + LANES] for c0 in range(0, s.shape[1], LANES)]
        mx = cols[0]
        for sc in cols[1:]:
            mx = jnp.maximum(mx, sc)
        m_new = jnp.broadcast_to(jnp.max(mx, axis=1, keepdims=True), (s.shape[0], LANES))
        if m_prev is not None:
            m_new = jnp.maximum(m_prev, m_new)
        return m_new, jnp.concatenate([jnp.exp2(sc - m_new).astype(BF16) for sc in cols], axis=1)

    def softmax_pv(s_ref, t):
        qi = tab_ref[0, t]
        start = pl.multiple_of(tab_ref[1, t] * tk, tk)
        m_prev = m_ref[qi]
        m_new, p = probabilities(s_ref[...], m_prev)
        alpha = jnp.exp2(m_prev - m_new)
        pv = _dot(p, values(start, tk))
        acc_ref[qi, :, 0:LANES] = alpha * acc_ref[qi, :, 0:LANES] + pv[:, 0:LANES]
        acc_ref[qi, :, LANES:2 * LANES] = alpha * acc_ref[qi, :, LANES:2 * LANES] + pv[:, LANES:2 * LANES]
        m_ref[qi] = m_new

    def softmax_pv_diagonal(s_ref, t):
        for r0, n_keys in ((0, half), (tq, tk)):
            r = lax.broadcasted_iota(jnp.int32, (tq, n_keys), 0)
            c = lax.broadcasted_iota(jnp.int32, (tq, n_keys), 1)
            q_pos = jnp.where(r >= half, r - half, r) + (half if r0 else 0)
            s = jnp.where(c <= q_pos, s_ref[r0:r0 + tq, 0:n_keys], -jnp.inf)
            m_new, p = probabilities(s, None)
            acc_ref[t, r0:r0 + tq, :] = _dot(p, values(t * tk, n_keys))
            m_ref[t, r0:r0 + tq, :] = m_new

    s_refs = (sa_ref, sb_ref)

    qk_diagonal(0, sa_ref)
    for t in range(nq):
        if t + 1 < nq:
            qk_diagonal(t + 1, s_refs[(t + 1) % 2])
        else:
            qk(t + 1, s_refs[(t + 1) % 2])
        softmax_pv_diagonal(s_refs[t % 2], t)

    def run(first, last, unroll):
        def body(i, carry):
            for u in range(unroll):
                t = first + i * unroll + u
                qk(t + 1, s_refs[(u + 1) % 2])
                softmax_pv(s_refs[u % 2], t)
            return carry
        lax.fori_loop(0, (last - first) // unroll, body, 0)

    main = nq + (n_chunks - nq) // unroll * unroll
    if main > nq:
        run(nq, main, unroll)
    if n_chunks > main:
        run(main, n_chunks, n_chunks - main)

    def finalize(t):
        for hf in range(2):
            a0 = acc_ref[t, (2 * hf) * half:(2 * hf + 1) * half, :]
            a1 = acc_ref[t, (2 * hf + 1) * half:(2 * hf + 2) * half, :]
            o0 = a0[:, 0:LANES] / a0[:, LANES:2 * LANES]
            o1 = a1[:, 0:LANES] / a1[:, LANES:2 * LANES]
            if mode == "diff":
                lam_init = consts_ref[0]
                lp = lam_ref[...]
                lam = (jnp.exp(jnp.sum(lp[0:1] * lp[1:2], axis=1, keepdims=True))
                       - jnp.exp(jnp.sum(lp[2:3] * lp[3:4], axis=1, keepdims=True)) + lam_init)
                y = o0 - lam * o1
            else:
                y = jnp.where(lane < HEAD_DIM, o0, o1)
            o_ref[t * tq + hf * half:t * tq + (hf + 1) * half, :] = y.astype(o_ref.dtype)

    for t in range(nq):
        finalize(t)


def _flash_call(mode, proj, q_blk, k_blk, v_blk, extra=(), *, tq=FLASH_TILE, unroll=FLASH_UNROLL):
    B, S, _ = proj.shape
    nq = S // tq
    order = _chunk_order(nq)
    assert nq % 2 == 0 and len(order) % 2 == 0, "the chunk loop handles two chunks per trip"
    table = jnp.asarray(np.array(order + [order[-1]], np.int32).T)
    kern = functools.partial(_flash_kernel, mode=mode, tq=tq, nq=nq, unroll=unroll)
    seq = lambda blk: pl.BlockSpec((None, S, LANES), lambda b, p: (b, 0, blk + p))
    smem = pl.BlockSpec(memory_space=pltpu.SMEM)
    in_specs = [smem, seq(q_blk), seq(k_blk), seq(v_blk)]
    if mode == "diff":
        consts, alibi_aug, lam_rows = extra
        in_specs = [smem] + in_specs + [
            pl.BlockSpec((None, S, LANES), lambda b, p: (p, 0, 0)),
            pl.BlockSpec((8, LANES), lambda b, p: (0, 0)),
        ]
        args = (consts, table, proj, proj, proj, alibi_aug, lam_rows)
    else:
        in_specs = in_specs + [seq(P_AUG)]
        args = (table, proj, proj, proj, proj)
    return pl.pallas_call(
        kern,
        grid=(B, N_PAIRS),
        in_specs=in_specs,
        out_specs=pl.BlockSpec((None, S, LANES), lambda b, p: (b, 0, p)),
        out_shape=jax.ShapeDtypeStruct((B, S, BRANCH_WIDTH), BF16),
        scratch_shapes=[
            pltpu.VMEM((nq, 2 * tq, 2 * LANES), BF16),
            pltpu.VMEM((nq, 2 * tq, LANES), F32),
            pltpu.VMEM((nq, 2 * tq, 2 * LANES), F32),
            pltpu.VMEM((2 * tq, tq), F32),
            pltpu.VMEM((2 * tq, tq), F32),
        ],
        compiler_params=_cparams(("parallel", "parallel")),
        name="flash_" + mode,
    )(*args)


def _swa_kernel(slopes_ref, sinks_ref, q_ref, k_ref, v_ref, pos_ref, o_ref, e_ref, *, seq, unroll):
    kv = pl.program_id(1)
    lane = lax.broadcasted_iota(jnp.int32, (SWA_BLOCK, LANES), 1)
    row = lax.broadcasted_iota(jnp.int32, (SWA_BLOCK, LANES), 0)
    r = lax.broadcasted_iota(jnp.int32, (SWA_BLOCK, 2 * SWA_BLOCK), 0)
    c = lax.broadcasted_iota(jnp.int32, (SWA_BLOCK, 2 * SWA_BLOCK), 1)
    for g in range(B_GROUP):
        first = N_PIECES * (kv * B_GROUP + g)
        ones_at_pieces = jnp.where(lane >= first, 1.0, 0.0) * jnp.where(lane < first + N_PIECES, 1.0, 0.0)
        e_ref[g * SWA_BLOCK:(g + 1) * SWA_BLOCK, :] = ones_at_pieces.astype(BF16)
    ones = jnp.ones((2 * SWA_BLOCK, LANES), BF16)

    def block(q_start):
        k_start = pl.multiple_of(jnp.maximum(q_start - SWA_BLOCK, 0), SWA_BLOCK)
        kw = jnp.concatenate([k_ref[pl.ds(k_start, 2 * SWA_BLOCK), :],
                              pos_ref[pl.ds(k_start, 2 * SWA_BLOCK), :]], axis=1)
        vw = jnp.concatenate([v_ref[pl.ds(k_start, 2 * SWA_BLOCK), :], ones], axis=1)
        heads = []
        for j in range(B_GROUP // 2):
            pair = q_ref[pl.ds(q_start, SWA_BLOCK), j * LANES:(j + 1) * LANES]
            zero = jnp.zeros_like(pair)
            heads += [jnp.where(lane < HEAD_DIM, pair, zero), jnp.where(lane >= HEAD_DIM, pair, zero)]
        qs = jnp.concatenate(heads, axis=0)
        s_all = _dot_nt(jnp.concatenate([qs, e_ref[...]], axis=1), kw)
        dist = (q_start - k_start) + r - c
        valid = (dist >= 0) & (dist < WINDOW)
        qpos = (q_start + row).astype(F32)
        ps, tails = [], []
        for g in range(B_GROUP):
            h = kv * B_GROUP + g
            sink = sinks_ref[h] + slopes_ref[h] * qpos
            s = jnp.where(valid, s_all[g * SWA_BLOCK:(g + 1) * SWA_BLOCK], -jnp.inf)
            s0, s1 = s[:, 0:LANES], s[:, LANES:2 * LANES]
            mx = jnp.maximum(jnp.max(jnp.maximum(s0, s1), axis=1, keepdims=True), sink)
            ps.append(jnp.concatenate([jnp.exp2(s0 - mx).astype(BF16), jnp.exp2(s1 - mx).astype(BF16)], axis=1))
            tails.append(jnp.exp2(sink - mx))
        pv = _dot(jnp.concatenate(ps, axis=0), vw)
        outs = []
        for g in range(B_GROUP):
            blk = pv[g * SWA_BLOCK:(g + 1) * SWA_BLOCK]
            outs.append(blk[:, 0:LANES] / (blk[:, LANES:2 * LANES] + tails[g]))
        first_kv = kv == 0
        for h2 in range(B_GROUP // 2):
            oa, ob = outs[2 * h2], outs[2 * h2 + 1]
            moved = pltpu.roll(jnp.where(first_kv, ob, oa), HEAD_DIM, 1)
            o_ref[pl.ds(q_start, SWA_BLOCK), h2 * LANES:(h2 + 1) * LANES] = jnp.where(
                lane < HEAD_DIM, jnp.where(first_kv, oa, moved), jnp.where(first_kv, moved, ob)).astype(o_ref.dtype)

    def body(i, carry):
        for n in range(unroll):
            block(pl.multiple_of((i * unroll + n) * SWA_BLOCK, SWA_BLOCK))
        return carry

    lax.fori_loop(0, seq // (SWA_BLOCK * unroll), body, 0)


def _swa_call(proj, slopes2, sinks2, pos_aug, *, unroll=SWA_UNROLL):
    B, S, _ = proj.shape
    kern = functools.partial(_swa_kernel, seq=S, unroll=unroll)
    qw = B_GROUP * HEAD_DIM
    return pl.pallas_call(
        kern,
        grid=(B, B_KV_HEADS),
        in_specs=[
            pl.BlockSpec(memory_space=pltpu.SMEM),
            pl.BlockSpec(memory_space=pltpu.SMEM),
            pl.BlockSpec((None, S, qw), lambda b, kv: (b, 0, P_QB * LANES // qw + kv)),
            pl.BlockSpec((None, S, LANES), lambda b, kv: (b, 0, P_KB + kv)),
            pl.BlockSpec((None, S, LANES), lambda b, kv: (b, 0, P_VB)),
            pl.BlockSpec((S, LANES), lambda b, kv: (0, 0)),
        ],
        out_specs=pl.BlockSpec((None, S, 2 * LANES), lambda b, kv: (b, 0, kv)),
        out_shape=jax.ShapeDtypeStruct((B, S, BRANCH_WIDTH), BF16),
        scratch_shapes=[pltpu.VMEM((B_GROUP * SWA_BLOCK, LANES), BF16)],
        compiler_params=_cparams(("parallel", "parallel")),
        name="swa",
    )(slopes2, sinks2, proj, proj, proj, pos_aug)


def _post_kernel(x_ref, g_ref, fg_ref, sg_ref, ya_ref, yb_ref, yc_ref, wg_ref, wgm_ref, wua_ref, wub_ref, wuc_ref,
                 wo_ref, o_ref, *, final):
    x = x_ref[...]
    h = _rms(x, g_ref[...], RMS_EPS).astype(BF16)
    hw = D_MODEL // 2
    ts = []
    for br, y_ref in enumerate((ya_ref, yb_ref, yc_ref)):
        g = _dot(h, wg_ref[:, br * BRANCH_WIDTH:(br + 1) * BRANCH_WIDTH])
        y = y_ref[...].astype(F32)
        if br == 0:
            y = jnp.concatenate([_rms(y[:, c0:c0 + LANES], sg_ref[...], SUBLN_EPS)
                                 for c0 in range(0, BRANCH_WIDTH, LANES)], axis=1)
        ts.append((y * (g * jax.nn.sigmoid(g))).astype(BF16))
    merged = [None, None]
    for br, wu_ref in enumerate((wua_ref, wub_ref, wuc_ref)):
        for j in range(2):
            gate = jax.nn.sigmoid(_dot(h, wgm_ref[:, br * D_MODEL + j * hw:br * D_MODEL + (j + 1) * hw]))
            u = _dot(ts[br], wu_ref[:, j * hw:(j + 1) * hw])
            merged[j] = gate * u if merged[j] is None else merged[j] + gate * u
    mb = jnp.concatenate(merged, axis=1).astype(BF16)
    for j in range(2):
        o_ref[:, j * hw:(j + 1) * hw] = x[:, j * hw:(j + 1) * hw] + _dot(mb, wo_ref[:, j * hw:(j + 1) * hw])
    if final:
        o_ref[...] = _rms(o_ref[...], fg_ref[...], RMS_EPS)


def _post_call(x, norm_gain, final_gain, sub_gain, ya, yb, yc, w_g, w_gm, w_ua, w_ub, w_uc, w_o, layer, final, *,
               tm=TOKEN_TILE):
    B, S, D = x.shape
    kern = functools.partial(_post_kernel, final=final)
    tok = lambda w: pl.BlockSpec((None, tm, w), lambda b, i: (b, i, 0))
    lay = lambda r, c: pl.BlockSpec((None, r, c), lambda b, i: (layer, 0, 0), pipeline_mode=pl.Buffered(1))
    return pl.pallas_call(
        kern,
        grid=(B, S // tm),
        in_specs=[
            tok(D), lay(1, D), pl.BlockSpec((1, D), lambda b, i: (0, 0)), pl.BlockSpec((1, LANES), lambda b, i: (0, 0)),
            tok(BRANCH_WIDTH), tok(BRANCH_WIDTH), tok(BRANCH_WIDTH),
            lay(D, N_BRANCH * BRANCH_WIDTH), lay(D, N_BRANCH * D),
            lay(BRANCH_WIDTH, D), lay(BRANCH_WIDTH, D), lay(BRANCH_WIDTH, D), lay(D, D),
        ],
        out_specs=tok(D),
        out_shape=jax.ShapeDtypeStruct((B, S, D), F32),
        compiler_params=_cparams(("parallel", "parallel")),
        name="post",
    )(x, norm_gain, final_gain, sub_gain, ya, yb, yc, w_g, w_gm, w_ua, w_ub, w_uc, w_o)


def _alibi_slopes(n_heads):
    return 2.0 ** (-8.0 * jnp.arange(1, n_heads + 1, dtype=F32) / n_heads)


def _prepare_weights(w_in):
    col_scale = np.ones((_D_IN,), np.float32)
    for o in (O_QA, O_QB, O_QC):
        col_scale[o:o + 512] = HEAD_DIM ** -0.5 * LOG2E
    wb = (w_in * col_scale).astype(BF16)
    sl = lambda o, n: wb[:, :, o:o + n]
    k0, k1 = sl(O_KB, HEAD_DIM), sl(O_KB + HEAD_DIM, HEAD_DIM)
    w_pre = jnp.concatenate(
        [sl(O_QA, 512), sl(O_KA, 512), sl(O_VA, 512), sl(O_QC, 512), sl(O_KC, 512), sl(O_VC, 512),
         sl(O_QB, 512), k0, k0, k1, k1, sl(O_VB, 128)], axis=-1)
    wf = jnp.pad(jnp.tile(sl(O_FC, C_HEADS), (1, 1, N_PIECES)),
                 ((0, 0), (0, 0), (0, LANES - N_PIECES * C_HEADS)))
    w_g = jnp.concatenate([sl(O_GA, 512), sl(O_GB, 512), sl(O_GC, 512)], axis=-1)
    w_gm = sl(O_GM, N_BRANCH * D_MODEL)
    return w_pre, wf, w_g, w_gm


def _placement():
    place = np.zeros((LANES, N_PAIRS * LANES), np.float32)
    for x in range(N_PIECES):
        for h in range(C_HEADS):
            place[C_HEADS * x + h, (h // 2) * LANES + N_PIECES * (h % 2) + x] = 1.0
    return jnp.asarray(place, BF16)


def _alibi_pieces(n_heads, seq):
    slopes = 2.0 ** (-8.0 * np.arange(1, n_heads + 1, dtype=np.float64) / n_heads)
    rest = (slopes[:, None] * (LOG2E * np.arange(seq, dtype=np.float64))[None, :]).astype(np.float32)
    pieces = np.zeros((n_heads, seq, N_PIECES), np.float32)
    for x in range(N_PIECES):
        pieces[:, :, x] = rest.astype(BF16).astype(np.float32)
        rest = rest - pieces[:, :, x]
    return pieces


def _alibi_aug(seq):
    table = np.zeros((A_HEADS, seq, LANES), np.float32)
    table[:, :, 0:N_PIECES] = _alibi_pieces(A_HEADS, seq)
    return jnp.asarray(table, BF16)


def _swa_pos_aug(seq):
    table = np.zeros((seq, LANES), np.float32)
    pieces = _alibi_pieces(B_HEADS, seq)
    for h in range(B_HEADS):
        table[:, N_PIECES * h:N_PIECES * (h + 1)] = pieces[h]
    return jnp.asarray(table, BF16)


def kernel(x, norm_gain, w_in, b_forget, lambda_q1, lambda_k1, lambda_q2, lambda_k2, subln_gain, sinks,
           w_up_a, w_up_b, w_up_c, w_o, final_gain):
    B, S, D = x.shape
    depth = w_in.shape[0]
    w_pre, wf, w_g, w_gm = _prepare_weights(w_in)
    w_ua, w_ub, w_uc, w_ob = (w.astype(BF16) for w in (w_up_a, w_up_b, w_up_c, w_o))
    gain3 = norm_gain.reshape(depth, 1, D)
    fgain = final_gain.reshape(1, D)
    bf3 = jnp.pad(jnp.tile(b_forget.astype(F32), (1, N_PIECES)),
                  ((0, 0), (0, LANES - N_PIECES * C_HEADS))).reshape(depth, 1, LANES)
    place = _placement()
    alibi_aug = _alibi_aug(S)
    slopes2_b = _alibi_slopes(B_HEADS) * LOG2E
    pos_aug = _swa_pos_aug(S)
    pad = lambda v: jnp.pad(v.astype(F32), (0, LANES - v.shape[0]))

    for l in range(depth):
        lam_init = 0.8 - 0.6 * math.exp(-0.3 * l)
        proj = _pre_call(x, gain3, w_pre, wf, bf3, place, l)
        consts = jnp.array([lam_init], F32)
        lam_rows = jnp.zeros((8, LANES), F32).at[0:4].set(
            jnp.stack([pad(lambda_q1[l]), pad(lambda_k1[l]), pad(lambda_q2[l]), pad(lambda_k2[l])]))
        ya = _flash_call("diff", proj, P_QA, P_KA, P_VA, (consts, alibi_aug, lam_rows))
        yc = _flash_call("fox", proj, P_QC, P_KC, P_VC)
        yb = _swa_call(proj, slopes2_b, sinks[l].astype(F32) * LOG2E, pos_aug)
        sub_gain = (subln_gain[l].astype(F32) * (1.0 - lam_init)).reshape(1, LANES)
        x = _post_call(x, gain3, fgain, sub_gain, ya, yb, yc, w_g, w_gm, w_ua, w_ub, w_uc, w_ob, l, l == depth - 1)
    return x
```

```python
import functools
import math

import numpy as np
import jax
import jax.numpy as jnp
from jax import lax
from jax.experimental import pallas as pl
from jax.experimental.pallas import tpu as pltpu

F32 = jnp.float32
BF16 = jnp.bfloat16

D_MODEL = 1024
HEAD_DIM = 64
LANES = 128
RMS_EPS = 1e-6
SUBLN_EPS = 1e-5
A_HEADS = 4
B_HEADS = 8
B_KV_HEADS = 2
B_GROUP = B_HEADS // B_KV_HEADS
C_HEADS = 8
WINDOW = 128
SWA_BLOCK = 128
N_BRANCH = 3
BRANCH_WIDTH = 512
N_PAIRS = 4
N_PIECES = 3
LOG2E = math.log2(math.e)

_SPLITS = (512, 512, 512, 512, 512, 128, 128, 512, 512, 512, 512, 8, 512, 3 * D_MODEL)
_OFF = [0]
for _s in _SPLITS:
    _OFF.append(_OFF[-1] + _s)
(O_QA, O_KA, O_VA, O_GA, O_QB, O_KB, O_VB, O_GB, O_QC, O_KC, O_VC, O_FC, O_GC, O_GM, _D_IN) = _OFF

P_QA, P_KA, P_VA = 0, 4, 8
P_QC, P_KC, P_VC = 12, 16, 20
P_QB = 24
P_KB = 28
P_VB = 30
P_MM = 31
P_AUG = 31
P_BLOCKS = 35
P_WIDTH = P_BLOCKS * LANES
MM_WIDTH = P_MM * LANES

VMEM_LIMIT = 56 * 1024 * 1024

TOKEN_TILE = 1024
PROJ_CHUNK = 512
FLASH_TILE = 512
FLASH_UNROLL = 14
SWA_UNROLL = 8


def _cparams(sem):
    return pltpu.CompilerParams(dimension_semantics=sem, vmem_limit_bytes=VMEM_LIMIT)


def _rms(x, gain, eps):
    ms = jnp.mean(x * x, axis=-1, keepdims=True)
    return x * lax.rsqrt(ms + eps) * gain


def _dot(a, b):
    return jnp.dot(a, b, preferred_element_type=F32)


def _dot_nt(a, b):
    return lax.dot_general(a, b, (((1,), (1,)), ((), ())), preferred_element_type=F32)


def _split3(v):
    hi = v.astype(BF16)
    r = v - hi.astype(F32)
    mid = r.astype(BF16)
    lo = (r - mid.astype(F32)).astype(BF16)
    return hi, mid, lo


def _pre_kernel(x_ref, g_ref, w_ref, wf_ref, bf_ref, place_ref, proj_ref, carry_ref, *, tm, chunk):
    i = pl.program_id(1)

    @pl.when(i == 0)
    def _():
        carry_ref[...] = jnp.zeros_like(carry_ref)

    h = _rms(x_ref[...], g_ref[...], RMS_EPS).astype(BF16)

    lane = lax.broadcasted_iota(jnp.int32, (tm, LANES), 1)

    def pieces_by_group(v):
        p0, p1, p2 = (p.astype(F32) for p in _split3(v))
        by_group = jnp.where(lane < C_HEADS, p0, jnp.where(lane < 2 * C_HEADS, p1, p2))
        return jnp.where(lane < N_PIECES * C_HEADS, by_group, 0.0).astype(BF16)

    starts = list(range(0, MM_WIDTH, chunk))

    def project(first, last):
        for c0 in starts[first:last]:
            c1 = min(c0 + chunk, MM_WIDTH)
            proj_ref[:, c0:c1] = _dot_nt(h, w_ref[c0:c1, :]).astype(BF16)

    n1, n2, n3 = len(starts) // 4, len(starts) // 2, 3 * len(starts) // 4
    project(0, n1)
    z = _dot_nt(h, wf_ref[...]) + bf_ref[...]
    logf = jnp.minimum(z, 0.0) - jnp.log1p(jnp.exp(-jnp.abs(z)))
    project(n1, n2)
    row = lax.broadcasted_iota(jnp.int32, (tm, tm), 0)
    col = lax.broadcasted_iota(jnp.int32, (tm, tm), 1)
    lower = (col <= row).astype(BF16)
    part = _dot(lower, pieces_by_group(logf))
    total = part + pltpu.roll(part, LANES - C_HEADS, 1) + pltpu.roll(part, LANES - 2 * C_HEADS, 1)
    total = jnp.where(lane < C_HEADS, total, 0.0)
    c = total + pltpu.roll(total, C_HEADS, 1) + pltpu.roll(total, 2 * C_HEADS, 1) + carry_ref[0:1, :]
    carry_ref[...] = jnp.broadcast_to(c[tm - 1:tm, :], carry_ref.shape)
    project(n2, n3)
    proj_ref[:, MM_WIDTH:P_WIDTH] = _dot(pieces_by_group(c * (-LOG2E)), place_ref[...]).astype(BF16)
    project(n3, len(starts))


def _pre_call(x, norm_gain, w_pre, wf, b_f, place, layer, *, tm=TOKEN_TILE, chunk=PROJ_CHUNK):
    B, S, D = x.shape
    kern = functools.partial(_pre_kernel, tm=tm, chunk=chunk)
    return pl.pallas_call(
        kern,
        grid=(B, S // tm),
        in_specs=[
            pl.BlockSpec((None, tm, D), lambda b, i: (b, i, 0)),
            pl.BlockSpec((None, 1, D), lambda b, i: (layer, 0, 0)),
            pl.BlockSpec((None, MM_WIDTH, D), lambda b, i: (layer, 0, 0), pipeline_mode=pl.Buffered(1)),
            pl.BlockSpec((None, LANES, D), lambda b, i: (layer, 0, 0)),
            pl.BlockSpec((None, 1, LANES), lambda b, i: (layer, 0, 0)),
            pl.BlockSpec((LANES, N_PAIRS * LANES), lambda b, i: (0, 0)),
        ],
        out_specs=pl.BlockSpec((None, tm, P_WIDTH), lambda b, i: (b, i, 0)),
        out_shape=jax.ShapeDtypeStruct((B, S, P_WIDTH), BF16),
        scratch_shapes=[pltpu.VMEM((8, LANES), F32)],
        compiler_params=_cparams(("parallel", "arbitrary")),
        name="pre",
    )(x, norm_gain, w_pre, wf, b_f, place)


def _chunk_order(nq):
    order = [(t, t) for t in range(nq)]
    order += [(qi, c) for qi in range(1, nq) for c in range(qi)]
    return order


def _flash_kernel(*refs, mode, tq, nq, unroll):
    if mode == "diff":
        (consts_ref, tab_ref, q_ref, k_ref, v_ref, aug_ref, lam_ref, o_ref,
         qe_ref, m_ref, acc_ref, sa_ref, sb_ref) = refs
    else:
        (tab_ref, q_ref, k_ref, v_ref, aug_ref, o_ref, qe_ref, m_ref, acc_ref, sa_ref, sb_ref) = refs
    tk = tq
    rows = 2 * tq
    half = tq // 2
    n_chunks = nq * (nq + 1) // 2
    lane = lax.broadcasted_iota(jnp.int32, (half, LANES), 1)

    def build_qe(t, carry):
        for hf in range(2):
            q = q_ref[pl.ds(pl.multiple_of(t * tq + hf * half, half), half), :]
            zero = jnp.zeros_like(q)
            for m in range(2):
                in_map = (lane >= HEAD_DIM) if m else (lane < HEAD_DIM)
                first = N_PIECES * m if mode == "fox" else 0
                ones_at_pieces = jnp.where(lane >= first, 1.0, 0.0) * jnp.where(lane < first + N_PIECES, 1.0, 0.0)
                g0 = (2 * hf + m) * half
                qe_ref[t, g0:g0 + half, 0:LANES] = jnp.where(in_map, q, zero)
                qe_ref[t, g0:g0 + half, LANES:2 * LANES] = ones_at_pieces.astype(BF16)
        return carry

    lax.fori_loop(0, nq, build_qe, 0)

    def keys(start, n):
        return jnp.concatenate([k_ref[pl.ds(start, n), :], aug_ref[pl.ds(start, n), :]], axis=1)

    def values(start, n):
        return jnp.concatenate([v_ref[pl.ds(start, n), :], jnp.ones((n, LANES), BF16)], axis=1)

    def qk(t, s_ref):
        qi = tab_ref[0, t]
        start = pl.multiple_of(tab_ref[1, t] * tk, tk)
        s_ref[...] = _dot_nt(qe_ref[qi], keys(start, tk))

    def qk_diagonal(t, s_ref):
        s_ref[0:tq, 0:half] = _dot_nt(qe_ref[t, 0:tq], keys(t * tk, half))
        s_ref[tq:rows, :] = _dot_nt(qe_ref[t, tq:rows], keys(t * tk, tk))

    def probabilities(s, m_prev):
        cols = [s[:, c0:c0 + LANES] for c0 in range(0, s.shape[1], LANES)]
        mx = cols[0]
        for sc in cols[1:]:
            mx = jnp.maximum(mx, sc)
        m_new = jnp.broadcast_to(jnp.max(mx, axis=1, keepdims=True), (s.shape[0], LANES))
        if m_prev is not None:
            m_new = jnp.maximum(m_prev, m_new)
        return m_new, jnp.concatenate([jnp.exp2(sc - m_new).astype(BF16) for sc in cols], axis=1)

    def softmax_pv(s_ref, t):
        qi = tab_ref[0, t]
        start = pl.multiple_of(tab_ref[1, t] * tk, tk)
        m_prev = m_ref[qi]
        m_new, p = probabilities(s_ref[...], m_prev)
        alpha = jnp.exp2(m_prev - m_new)
        pv = _dot(p, values(start, tk))
        acc_ref[qi, :, 0:LANES] = alpha * acc_ref[qi, :, 0:LANES] + pv[:, 0:LANES]
        acc_ref[qi, :, LANES:2 * LANES] = alpha * acc_ref[qi, :, LANES:2 * LANES] + pv[:, LANES:2 * LANES]
        m_ref[qi] = m_new

    def softmax_pv_diagonal(s_ref, t):
        for r0, n_keys in ((0, half), (tq, tk)):
            r = lax.broadcasted_iota(jnp.int32, (tq, n_keys), 0)
            c = lax.broadcasted_iota(jnp.int32, (tq, n_keys), 1)
            q_pos = jnp.where(r >= half, r - half, r) + (half if r0 else 0)
            s = jnp.where(c <= q_pos, s_ref[r0:r0 + tq, 0:n_keys], -jnp.inf)
            m_new, p = probabilities(s, None)
            acc_ref[t, r0:r0 + tq, :] = _dot(p, values(t * tk, n_keys))
            m_ref[t, r0:r0 + tq, :] = m_new

    s_refs = (sa_ref, sb_ref)

    qk_diagonal(0, sa_ref)
    for t in range(nq):
        if t + 1 < nq:
            qk_diagonal(t + 1, s_refs[(t + 1) % 2])
        else:
            qk(t + 1, s_refs[(t + 1) % 2])
        softmax_pv_diagonal(s_refs[t % 2], t)

    def run(first, last, unroll):
        def body(i, carry):
            for u in range(unroll):
                t = first + i * unroll + u
                qk(t + 1, s_refs[(u + 1) % 2])
                softmax_pv(s_refs[u % 2], t)
            return carry
        lax.fori_loop(0, (last - first) // unroll, body, 0)

    main = nq + (n_chunks - nq) // unroll * unroll
    if main > nq:
        run(nq, main, unroll)
    if n_chunks > main:
        run(main, n_chunks, n_chunks - main)

    def finalize(t):
        for hf in range(2):
            a0 = acc_ref[t, (2 * hf) * half:(2 * hf + 1) * half, :]
            a1 = acc_ref[t, (2 * hf + 1) * half:(2 * hf + 2) * half, :]
            o0 = a0[:, 0:LANES] / a0[:, LANES:2 * LANES]
            o1 = a1[:, 0:LANES] / a1[:, LANES:2 * LANES]
            if mode == "diff":
                lam_init = consts_ref[0]
                lp = lam_ref[...]
                lam = (jnp.exp(jnp.sum(lp[0:1] * lp[1:2], axis=1, keepdims=True))
                       - jnp.exp(jnp.sum(lp[2:3] * lp[3:4], axis=1, keepdims=True)) + lam_init)
                y = o0 - lam * o1
            else:
                y = jnp.where(lane < HEAD_DIM, o0, o1)
            o_ref[t * tq + hf * half:t * tq + (hf + 1) * half, :] = y.astype(o_ref.dtype)

    for t in range(nq):
        finalize(t)


def _flash_call(mode, proj, q_blk, k_blk, v_blk, extra=(), *, tq=FLASH_TILE, unroll=FLASH_UNROLL):
    B, S, _ = proj.shape
    nq = S // tq
    order = _chunk_order(nq)
    assert nq % 2 == 0 and len(order) % 2 == 0, "the chunk loop handles two chunks per trip"
    table = jnp.asarray(np.array(order + [order[-1]], np.int32).T)
    kern = functools.partial(_flash_kernel, mode=mode, tq=tq, nq=nq, unroll=unroll)
    seq = lambda blk: pl.BlockSpec((None, S, LANES), lambda b, p: (b, 0, blk + p))
    smem = pl.BlockSpec(memory_space=pltpu.SMEM)
    in_specs = [smem, seq(q_blk), seq(k_blk), seq(v_blk)]
    if mode == "diff":
        consts, alibi_aug, lam_rows = extra
        in_specs = [smem] + in_specs + [
            pl.BlockSpec((None, S, LANES), lambda b, p: (p, 0, 0)),
            pl.BlockSpec((8, LANES), lambda b, p: (0, 0)),
        ]
        args = (consts, table, proj, proj, proj, alibi_aug, lam_rows)
    else:
        in_specs = in_specs + [seq(P_AUG)]
        args = (table, proj, proj, proj, proj)
    return pl.pallas_call(
        kern,
        grid=(B, N_PAIRS),
        in_specs=in_specs,
        out_specs=pl.BlockSpec((None, S, LANES), lambda b, p: (b, 0, p)),
        out_shape=jax.ShapeDtypeStruct((B, S, BRANCH_WIDTH), BF16),
        scratch_shapes=[
            pltpu.VMEM((nq, 2 * tq, 2 * LANES), BF16),
            pltpu.VMEM((nq, 2 * tq, LANES), F32),
            pltpu.VMEM((nq, 2 * tq, 2 * LANES), F32),
            pltpu.VMEM((2 * tq, tq), F32),
            pltpu.VMEM((2 * tq, tq), F32),
        ],
        compiler_params=_cparams(("parallel", "parallel")),
        name="flash_" + mode,
    )(*args)


def _swa_kernel(slopes_ref, sinks_ref, q_ref, k_ref, v_ref, pos_ref, o_ref, e_ref, *, seq, unroll):
    kv = pl.program_id(1)
    lane = lax.broadcasted_iota(jnp.int32, (SWA_BLOCK, LANES), 1)
    row = lax.broadcasted_iota(jnp.int32, (SWA_BLOCK, LANES), 0)
    r = lax.broadcasted_iota(jnp.int32, (SWA_BLOCK, 2 * SWA_BLOCK), 0)
    c = lax.broadcasted_iota(jnp.int32, (SWA_BLOCK, 2 * SWA_BLOCK), 1)
    for g in range(B_GROUP):
        first = N_PIECES * (kv * B_GROUP + g)
        ones_at_pieces = jnp.where(lane >= first, 1.0, 0.0) * jnp.where(lane < first + N_PIECES, 1.0, 0.0)
        e_ref[g * SWA_BLOCK:(g + 1) * SWA_BLOCK, :] = ones_at_pieces.astype(BF16)
    ones = jnp.ones((2 * SWA_BLOCK, LANES), BF16)

    def block(q_start):
        k_start = pl.multiple_of(jnp.maximum(q_start - SWA_BLOCK, 0), SWA_BLOCK)
        kw = jnp.concatenate([k_ref[pl.ds(k_start, 2 * SWA_BLOCK), :],
                              pos_ref[pl.ds(k_start, 2 * SWA_BLOCK), :]], axis=1)
        vw = jnp.concatenate([v_ref[pl.ds(k_start, 2 * SWA_BLOCK), :], ones], axis=1)
        heads = []
        for j in range(B_GROUP // 2):
            pair = q_ref[pl.ds(q_start, SWA_BLOCK), j * LANES:(j + 1) * LANES]
            zero = jnp.zeros_like(pair)
            heads += [jnp.where(lane < HEAD_DIM, pair, zero), jnp.where(lane >= HEAD_DIM, pair, zero)]
        qs = jnp.concatenate(heads, axis=0)
        s_all = _dot_nt(jnp.concatenate([qs, e_ref[...]], axis=1), kw)
        dist = (q_start - k_start) + r - c
        valid = (dist >= 0) & (dist < WINDOW)
        qpos = (q_start + row).astype(F32)
        ps, tails = [], []
        for g in range(B_GROUP):
            h = kv * B_GROUP + g
            sink = sinks_ref[h] + slopes_ref[h] * qpos
            s = jnp.where(valid, s_all[g * SWA_BLOCK:(g + 1) * SWA_BLOCK], -jnp.inf)
            s0, s1 = s[:, 0:LANES], s[:, LANES:2 * LANES]
            mx = jnp.maximum(jnp.max(jnp.maximum(s0, s1), axis=1, keepdims=True), sink)
            ps.append(jnp.concatenate([jnp.exp2(s0 - mx).astype(BF16), jnp.exp2(s1 - mx).astype(BF16)], axis=1))
            tails.append(jnp.exp2(sink - mx))
        pv = _dot(jnp.concatenate(ps, axis=0), vw)
        outs = []
        for g in range(B_GROUP):
            blk = pv[g * SWA_BLOCK:(g + 1) * SWA_BLOCK]
            outs.append(blk[:, 0:LANES] / (blk[:, LANES:2 * LANES] + tails[g]))
        first_kv = kv == 0
        for h2 in range(B_GROUP // 2):
            oa, ob = outs[2 * h2], outs[2 * h2 + 1]
            moved = pltpu.roll(jnp.where(first_kv, ob, oa), HEAD_DIM, 1)
            o_ref[pl.ds(q_start, SWA_BLOCK), h2 * LANES:(h2 + 1) * LANES] = jnp.where(
                lane < HEAD_DIM, jnp.where(first_kv, oa, moved), jnp.where(first_kv, moved, ob)).astype(o_ref.dtype)

    def body(i, carry):
        for n in range(unroll):
            block(pl.multiple_of((i * unroll + n) * SWA_BLOCK, SWA_BLOCK))
        return carry

    lax.fori_loop(0, seq // (SWA_BLOCK * unroll), body, 0)


def _swa_call(proj, slopes2, sinks2, pos_aug, *, unroll=SWA_UNROLL):
    B, S, _ = proj.shape
    kern = functools.partial(_swa_kernel, seq=S, unroll=unroll)
    qw = B_GROUP * HEAD_DIM
    return pl.pallas_call(
        kern,
        grid=(B, B_KV_HEADS),
        in_specs=[
            pl.BlockSpec(memory_space=pltpu.SMEM),
            pl.BlockSpec(memory_space=pltpu.SMEM),
            pl.BlockSpec((None, S, qw), lambda b, kv: (b, 0, P_QB * LANES // qw + kv)),
            pl.BlockSpec((None, S, LANES), lambda b, kv: (b, 0, P_KB + kv)),
            pl.BlockSpec((None, S, LANES), lambda b, kv: (b, 0, P_VB)),
            pl.BlockSpec((S, LANES), lambda b, kv: (0, 0)),
        ],
        out_specs=pl.BlockSpec((None, S, 2 * LANES), lambda b, kv: (b, 0, kv)),
        out_shape=jax.ShapeDtypeStruct((B, S, BRANCH_WIDTH), BF16),
        scratch_shapes=[pltpu.VMEM((B_GROUP * SWA_BLOCK, LANES), BF16)],
        compiler_params=_cparams(("parallel", "parallel")),
        name="swa",
    )(slopes2, sinks2, proj, proj, proj, pos_aug)


def _post_kernel(x_ref, g_ref, fg_ref, sg_ref, ya_ref, yb_ref, yc_ref, wg_ref, wgm_ref, wua_ref, wub_ref, wuc_ref,
                 wo_ref, o_ref, *, final):
    x = x_ref[...]
    h = _rms(x, g_ref[...], RMS_EPS).astype(BF16)
    hw = D_MODEL // 2
    ts = []
    for br, y_ref in enumerate((ya_ref, yb_ref, yc_ref)):
        g = _dot_nt(h, wg_ref[br * BRANCH_WIDTH:(br + 1) * BRANCH_WIDTH, :])
        y = y_ref[...].astype(F32)
        if br == 0:
            y = jnp.concatenate([_rms(y[:, c0:c0 + LANES], sg_ref[...], SUBLN_EPS)
                                 for c0 in range(0, BRANCH_WIDTH, LANES)], axis=1)
        ts.append((y * (g * jax.nn.sigmoid(g))).astype(BF16))
    merged = [None, None]
    for br, wu_ref in enumerate((wua_ref, wub_ref, wuc_ref)):
        for j in range(2):
            gate = jax.nn.sigmoid(_dot_nt(h, wgm_ref[br * D_MODEL + j * hw:br * D_MODEL + (j + 1) * hw, :]))
            u = _dot(ts[br], wu_ref[:, j * hw:(j + 1) * hw])
            merged[j] = gate * u if merged[j] is None else merged[j] + gate * u
    mb = jnp.concatenate(merged, axis=1).astype(BF16)
    for j in range(2):
        o_ref[:, j * hw:(j + 1) * hw] = x[:, j * hw:(j + 1) * hw] + _dot(mb, wo_ref[:, j * hw:(j + 1) * hw])
    if final:
        o_ref[...] = _rms(o_ref[...], fg_ref[...], RMS_EPS)


def _post_call(x, norm_gain, final_gain, sub_gain, ya, yb, yc, w_g, w_gm, w_ua, w_ub, w_uc, w_o, layer, final, *,
               tm=TOKEN_TILE):
    B, S, D = x.shape
    kern = functools.partial(_post_kernel, final=final)
    tok = lambda w: pl.BlockSpec((None, tm, w), lambda b, i: (b, i, 0))
    lay = lambda r, c: pl.BlockSpec((None, r, c), lambda b, i: (layer, 0, 0), pipeline_mode=pl.Buffered(1))
    return pl.pallas_call(
        kern,
        grid=(B, S // tm),
        in_specs=[
            tok(D), lay(1, D), pl.BlockSpec((1, D), lambda b, i: (0, 0)), pl.BlockSpec((1, LANES), lambda b, i: (0, 0)),
            tok(BRANCH_WIDTH), tok(BRANCH_WIDTH), tok(BRANCH_WIDTH),
            lay(N_BRANCH * BRANCH_WIDTH, D), lay(N_BRANCH * D, D),
            lay(BRANCH_WIDTH, D), lay(BRANCH_WIDTH, D), lay(BRANCH_WIDTH, D), lay(D, D),
        ],
        out_specs=tok(D),
        out_shape=jax.ShapeDtypeStruct((B, S, D), F32),
        compiler_params=_cparams(("parallel", "parallel")),
        name="post",
    )(x, norm_gain, final_gain, sub_gain, ya, yb, yc, w_g, w_gm, w_ua, w_ub, w_uc, w_o)


def _alibi_slopes(n_heads):
    return 2.0 ** (-8.0 * jnp.arange(1, n_heads + 1, dtype=F32) / n_heads)


def _prepare_weights(w_in):
    col_scale = np.ones((_D_IN,), np.float32)
    for o in (O_QA, O_QB, O_QC):
        col_scale[o:o + 512] = HEAD_DIM ** -0.5 * LOG2E
    wtb = (jnp.swapaxes(w_in, 1, 2) * col_scale[:, None]).astype(BF16)
    sl = lambda o, n: wtb[:, o:o + n, :]
    k0, k1 = sl(O_KB, HEAD_DIM), sl(O_KB + HEAD_DIM, HEAD_DIM)
    w_pre = jnp.concatenate(
        [sl(O_QA, 512), sl(O_KA, 512), sl(O_VA, 512), sl(O_QC, 512), sl(O_KC, 512), sl(O_VC, 512),
         sl(O_QB, 512), k0, k0, k1, k1, sl(O_VB, 128)], axis=1)
    wf = jnp.pad(jnp.tile(sl(O_FC, C_HEADS), (1, N_PIECES, 1)),
                 ((0, 0), (0, LANES - N_PIECES * C_HEADS), (0, 0)))
    w_g = jnp.concatenate([sl(O_GA, 512), sl(O_GB, 512), sl(O_GC, 512)], axis=1)
    w_gm = sl(O_GM, N_BRANCH * D_MODEL)
    return w_pre, wf, w_g, w_gm


def _placement():
    place = np.zeros((LANES, N_PAIRS * LANES), np.float32)
    for x in range(N_PIECES):
        for h in range(C_HEADS):
            place[C_HEADS * x + h, (h // 2) * LANES + N_PIECES * (h % 2) + x] = 1.0
    return jnp.asarray(place, BF16)


def _alibi_pieces(n_heads, seq):
    slopes = 2.0 ** (-8.0 * np.arange(1, n_heads + 1, dtype=np.float64) / n_heads)
    rest = (slopes[:, None] * (LOG2E * np.arange(seq, dtype=np.float64))[None, :]).astype(np.float32)
    pieces = np.zeros((n_heads, seq, N_PIECES), np.float32)
    for x in range(N_PIECES):
        pieces[:, :, x] = rest.astype(BF16).astype(np.float32)
        rest = rest - pieces[:, :, x]
    return pieces


def _alibi_aug(seq):
    table = np.zeros((A_HEADS, seq, LANES), np.float32)
    table[:, :, 0:N_PIECES] = _alibi_pieces(A_HEADS, seq)
    return jnp.asarray(table, BF16)


def _swa_pos_aug(seq):
    table = np.zeros((seq, LANES), np.float32)
    pieces = _alibi_pieces(B_HEADS, seq)
    for h in range(B_HEADS):
        table[:, N_PIECES * h:N_PIECES * (h + 1)] = pieces[h]
    return jnp.asarray(table, BF16)


def kernel(x, norm_gain, w_in, b_forget, lambda_q1, lambda_k1, lambda_q2, lambda_k2, subln_gain, sinks,
           w_up_a, w_up_b, w_up_c, w_o, final_gain):
    B, S, D = x.shape
    depth = w_in.shape[0]
    w_pre, wf, w_g, w_gm = _prepare_weights(w_in)
    w_ua, w_ub, w_uc, w_ob = (w.astype(BF16) for w in (w_up_a, w_up_b, w_up_c, w_o))
    gain3 = norm_gain.reshape(depth, 1, D)
    fgain = final_gain.reshape(1, D)
    bf3 = jnp.pad(jnp.tile(b_forget.astype(F32), (1, N_PIECES)),
                  ((0, 0), (0, LANES - N_PIECES * C_HEADS))).reshape(depth, 1, LANES)
    place = _placement()
    alibi_aug = _alibi_aug(S)
    slopes2_b = _alibi_slopes(B_HEADS) * LOG2E
    pos_aug = _swa_pos_aug(S)
    pad = lambda v: jnp.pad(v.astype(F32), (0, LANES - v.shape[0]))

    for l in range(depth):
        lam_init = 0.8 - 0.6 * math.exp(-0.3 * l)
        proj = _pre_call(x, gain3, w_pre, wf, bf3, place, l)
        consts = jnp.array([lam_init], F32)
        lam_rows = jnp.zeros((8, LANES), F32).at[0:4].set(
            jnp.stack([pad(lambda_q1[l]), pad(lambda_k1[l]), pad(lambda_q2[l]), pad(lambda_k2[l])]))
        ya = _flash_call("diff", proj, P_QA, P_KA, P_VA, (consts, alibi_aug, lam_rows))
        yc = _flash_call("fox", proj, P_QC, P_KC, P_VC)
        yb = _swa_call(proj, slopes2_b, sinks[l].astype(F32) * LOG2E, pos_aug)
        sub_gain = (subln_gain[l].astype(F32) * (1.0 - lam_init)).reshape(1, LANES)
        x = _post_call(x, gain3, fgain, sub_gain, ya, yb, yc, w_g, w_gm, w_ua, w_ub, w_uc, w_ob, l, l == depth - 1)
    return x
```

```python
import functools
import math

import numpy as np
import jax
import jax.numpy as jnp
from jax import lax
from jax.experimental import pallas as pl
from jax.experimental.pallas import tpu as pltpu

F32 = jnp.float32
BF16 = jnp.bfloat16

D_MODEL = 1024
HEAD_DIM = 64
LANES = 128
RMS_EPS = 1e-6
SUBLN_EPS = 1e-5
A_HEADS = 4
B_HEADS = 8
B_KV_HEADS = 2
B_GROUP = B_HEADS // B_KV_HEADS
C_HEADS = 8
WINDOW = 128
SWA_BLOCK = 128
N_BRANCH = 3
BRANCH_WIDTH = 512
N_PAIRS = 4
N_PIECES = 3
LOG2E = math.log2(math.e)

_SPLITS = (512, 512, 512, 512, 512, 128, 128, 512, 512, 512, 512, 8, 512, 3 * D_MODEL)
_OFF = [0]
for _s in _SPLITS:
    _OFF.append(_OFF[-1] + _s)
(O_QA, O_KA, O_VA, O_GA, O_QB, O_KB, O_VB, O_GB, O_QC, O_KC, O_VC, O_FC, O_GC, O_GM, _D_IN) = _OFF

P_QA, P_KA, P_VA = 0, 4, 8
P_QC, P_KC, P_VC = 12, 16, 20
P_QB = 24
P_KB = 28
P_VB = 30
P_MM = 31
P_AUG = 31
P_BLOCKS = 35
P_WIDTH = P_BLOCKS * LANES
MM_WIDTH = P_MM * LANES

VMEM_LIMIT = 56 * 1024 * 1024

TOKEN_TILE = 1024
PROJ_CHUNK = 512
FLASH_TILE = 512
FLASH_UNROLL = 14
SWA_UNROLL = 8


def _cparams(sem):
    return pltpu.CompilerParams(dimension_semantics=sem, vmem_limit_bytes=VMEM_LIMIT)


def _rms(x, gain, eps):
    ms = jnp.mean(x * x, axis=-1, keepdims=True)
    return x * lax.rsqrt(ms + eps) * gain


def _dot(a, b):
    return jnp.dot(a, b, preferred_element_type=F32)


def _dot_nt(a, b):
    return lax.dot_general(a, b, (((1,), (1,)), ((), ())), preferred_element_type=F32)


def _split3(v):
    hi = v.astype(BF16)
    r = v - hi.astype(F32)
    mid = r.astype(BF16)
    lo = (r - mid.astype(F32)).astype(BF16)
    return hi, mid, lo


def _pre_kernel(x_ref, g_ref, w_ref, wf_ref, bf_ref, place_ref, proj_ref, carry_ref, *, tm, chunk):
    i = pl.program_id(1)

    @pl.when(i == 0)
    def _():
        carry_ref[...] = jnp.zeros_like(carry_ref)

    h = _rms(x_ref[...], g_ref[...], RMS_EPS).astype(BF16)

    lane = lax.broadcasted_iota(jnp.int32, (tm, LANES), 1)

    def pieces_by_group(v):
        p0, p1, p2 = (p.astype(F32) for p in _split3(v))
        by_group = jnp.where(lane < C_HEADS, p0, jnp.where(lane < 2 * C_HEADS, p1, p2))
        return jnp.where(lane < N_PIECES * C_HEADS, by_group, 0.0).astype(BF16)

    starts = list(range(0, MM_WIDTH, chunk))

    def project(first, last):
        for c0 in starts[first:last]:
            c1 = min(c0 + chunk, MM_WIDTH)
            proj_ref[:, c0:c1] = _dot_nt(h, w_ref[c0:c1, :]).astype(BF16)

    n1, n2, n3 = len(starts) // 4, len(starts) // 2, 3 * len(starts) // 4
    project(0, n1)
    z = _dot_nt(h, wf_ref[...]) + bf_ref[...]
    logf = jnp.minimum(z, 0.0) - jnp.log1p(jnp.exp(-jnp.abs(z)))
    project(n1, n2)
    row = lax.broadcasted_iota(jnp.int32, (tm, tm), 0)
    col = lax.broadcasted_iota(jnp.int32, (tm, tm), 1)
    lower = (col <= row).astype(BF16)
    part = _dot(lower, pieces_by_group(logf))
    total = part + pltpu.roll(part, LANES - C_HEADS, 1) + pltpu.roll(part, LANES - 2 * C_HEADS, 1)
    total = jnp.where(lane < C_HEADS, total, 0.0)
    c = total + pltpu.roll(total, C_HEADS, 1) + pltpu.roll(total, 2 * C_HEADS, 1) + carry_ref[0:1, :]
    carry_ref[...] = jnp.broadcast_to(c[tm - 1:tm, :], carry_ref.shape)
    project(n2, n3)
    proj_ref[:, MM_WIDTH:P_WIDTH] = _dot(pieces_by_group(c * (-LOG2E)), place_ref[...]).astype(BF16)
    project(n3, len(starts))


def _pre_call(x, norm_gain, w_pre, wf, b_f, place, layer, *, tm=TOKEN_TILE, chunk=PROJ_CHUNK):
    B, S, D = x.shape
    kern = functools.partial(_pre_kernel, tm=tm, chunk=chunk)
    return pl.pallas_call(
        kern,
        grid=(B, S // tm),
        in_specs=[
            pl.BlockSpec((None, tm, D), lambda b, i: (b, i, 0)),
            pl.BlockSpec((None, 1, D), lambda b, i: (layer, 0, 0)),
            pl.BlockSpec((None, MM_WIDTH, D), lambda b, i: (layer, 0, 0), pipeline_mode=pl.Buffered(1)),
            pl.BlockSpec((None, LANES, D), lambda b, i: (layer, 0, 0)),
            pl.BlockSpec((None, 1, LANES), lambda b, i: (layer, 0, 0)),
            pl.BlockSpec((LANES, N_PAIRS * LANES), lambda b, i: (0, 0)),
        ],
        out_specs=pl.BlockSpec((None, tm, P_WIDTH), lambda b, i: (b, i, 0)),
        out_shape=jax.ShapeDtypeStruct((B, S, P_WIDTH), BF16),
        scratch_shapes=[pltpu.VMEM((8, LANES), F32)],
        compiler_params=_cparams(("parallel", "arbitrary")),
        name="pre",
    )(x, norm_gain, w_pre, wf, b_f, place)


def _chunk_order(nq):
    order = [(t, t) for t in range(nq)]
    order += [(qi, c) for qi in range(1, nq) for c in range(qi)]
    return order


def _flash_kernel(*refs, mode, tq, nq, unroll):
    if mode == "diff":
        (consts_ref, tab_ref, q_ref, k_ref, v_ref, aug_ref, lam_ref, o_ref,
         qe_ref, m_ref, acc_ref, sa_ref, sb_ref) = refs
    else:
        (tab_ref, q_ref, k_ref, v_ref, aug_ref, o_ref, qe_ref, m_ref, acc_ref, sa_ref, sb_ref) = refs
    tk = tq
    rows = 2 * tq
    half = tq // 2
    n_chunks = nq * (nq + 1) // 2
    lane = lax.broadcasted_iota(jnp.int32, (half, LANES), 1)

    def build_qe(t, carry):
        for hf in range(2):
            q = q_ref[pl.ds(pl.multiple_of(t * tq + hf * half, half), half), :]
            zero = jnp.zeros_like(q)
            for m in range(2):
                in_map = (lane >= HEAD_DIM) if m else (lane < HEAD_DIM)
                first = N_PIECES * m if mode == "fox" else 0
                ones_at_pieces = jnp.where(lane >= first, 1.0, 0.0) * jnp.where(lane < first + N_PIECES, 1.0, 0.0)
                g0 = (2 * hf + m) * half
                qe_ref[t, g0:g0 + half, 0:LANES] = jnp.where(in_map, q, zero)
                qe_ref[t, g0:g0 + half, LANES:2 * LANES] = ones_at_pieces.astype(BF16)
        return carry

    lax.fori_loop(0, nq, build_qe, 0)

    def keys(start, n):
        return jnp.concatenate([k_ref[pl.ds(start, n), :], aug_ref[pl.ds(start, n), :]], axis=1)

    def values(start, n):
        return jnp.concatenate([v_ref[pl.ds(start, n), :], jnp.ones((n, LANES), BF16)], axis=1)

    def qk(t, s_ref):
        qi = tab_ref[0, t]
        start = pl.multiple_of(tab_ref[1, t] * tk, tk)
        s_ref[...] = _dot_nt(qe_ref[qi], keys(start, tk))

    def qk_diagonal(t, s_ref):
        s_ref[0:tq, 0:half] = _dot_nt(qe_ref[t, 0:tq], keys(t * tk, half))
        s_ref[tq:rows, :] = _dot_nt(qe_ref[t, tq:rows], keys(t * tk, tk))

    def probabilities(s, m_prev):
        cols = [s[:, c0:c0 + LANES] for c0 in range(0, s.shape[1], LANES)]
        mx = cols[0]
        for sc in cols[1:]:
            mx = jnp.maximum(mx, sc)
        m_new = jnp.broadcast_to(jnp.max(mx, axis=1, keepdims=True), (s.shape[0], LANES))
        if m_prev is not None:
            m_new = jnp.maximum(m_prev, m_new)
        return m_new, jnp.concatenate([jnp.exp2(sc - m_new).astype(BF16) for sc in cols], axis=1)

    def softmax_pv(s_ref, t):
        qi = tab_ref[0, t]
        start = pl.multiple_of(tab_ref[1, t] * tk, tk)
        m_prev = m_ref[qi]
        m_new, p = probabilities(s_ref[...], m_prev)
        alpha = jnp.exp2(m_prev - m_new)
        pv = _dot(p, values(start, tk))
        acc_ref[qi, :, 0:LANES] = alpha * acc_ref[qi, :, 0:LANES] + pv[:, 0:LANES]
        acc_ref[qi, :, LANES:2 * LANES] = alpha * acc_ref[qi, :, LANES:2 * LANES] + pv[:, LANES:2 * LANES]
        m_ref[qi] = m_new

    def softmax_pv_diagonal(s_ref, t):
        for r0, n_keys in ((0, half), (tq, tk)):
            r = lax.broadcasted_iota(jnp.int32, (tq, n_keys), 0)
            c = lax.broadcasted_iota(jnp.int32, (tq, n_keys), 1)
            q_pos = jnp.where(r >= half, r - half, r) + (half if r0 else 0)
            s = jnp.where(c <= q_pos, s_ref[r0:r0 + tq, 0:n_keys], -jnp.inf)
            m_new, p = probabilities(s, None)
            acc_ref[t, r0:r0 + tq, :] = _dot(p, values(t * tk, n_keys))
            m_ref[t, r0:r0 + tq, :] = m_new

    s_refs = (sa_ref, sb_ref)

    qk_diagonal(0, sa_ref)
    for t in range(nq):
        if t + 1 < nq:
            qk_diagonal(t + 1, s_refs[(t + 1) % 2])
        else:
            qk(t + 1, s_refs[(t + 1) % 2])
        softmax_pv_diagonal(s_refs[t % 2], t)

    def run(first, last, unroll):
        def body(i, carry):
            for u in range(unroll):
                t = first + i * unroll + u
                qk(t + 1, s_refs[(u + 1) % 2])
                softmax_pv(s_refs[u % 2], t)
            return carry
        lax.fori_loop(0, (last - first) // unroll, body, 0)

    main = nq + (n_chunks - nq) // unroll * unroll
    if main > nq:
        run(nq, main, unroll)
    if n_chunks > main:
        run(main, n_chunks, n_chunks - main)

    def finalize(t):
        for hf in range(2):
            a0 = acc_ref[t, (2 * hf) * half:(2 * hf + 1) * half, :]
            a1 = acc_ref[t, (2 * hf + 1) * half:(2 * hf + 2) * half, :]
            o0 = a0[:, 0:LANES] / a0[:, LANES:2 * LANES]
            o1 = a1[:, 0:LANES] / a1[:, LANES:2 * LANES]
            if mode == "diff":
                lam_init = consts_ref[0]
                lp = lam_ref[...]
                lam = (jnp.exp(jnp.sum(lp[0:1] * lp[1:2], axis=1, keepdims=True))
                       - jnp.exp(jnp.sum(lp[2:3] * lp[3:4], axis=1, keepdims=True)) + lam_init)
                y = o0 - lam * o1
            else:
                y = jnp.where(lane < HEAD_DIM, o0, o1)
            o_ref[t * tq + hf * half:t * tq + (hf + 1) * half, :] = y.astype(o_ref.dtype)

    for t in range(nq):
        finalize(t)


def _flash_call(mode, proj, q_blk, k_blk, v_blk, extra=(), *, tq=FLASH_TILE, unroll=FLASH_UNROLL):
    B, S, _ = proj.shape
    nq = S // tq
    order = _chunk_order(nq)
    assert nq % 2 == 0 and len(order) % 2 == 0, "the chunk loop handles two chunks per trip"
    table = jnp.asarray(np.array(order + [order[-1]], np.int32).T)
    kern = functools.partial(_flash_kernel, mode=mode, tq=tq, nq=nq, unroll=unroll)
    seq = lambda blk: pl.BlockSpec((None, S, LANES), lambda b, p: (b, 0, blk + p))
    smem = pl.BlockSpec(memory_space=pltpu.SMEM)
    in_specs = [smem, seq(q_blk), seq(k_blk), seq(v_blk)]
    if mode == "diff":
        consts, alibi_aug, lam_rows = extra
        in_specs = [smem] + in_specs + [
            pl.BlockSpec((None, S, LANES), lambda b, p: (p, 0, 0)),
            pl.BlockSpec((8, LANES), lambda b, p: (0, 0)),
        ]
        args = (consts, table, proj, proj, proj, alibi_aug, lam_rows)
    else:
        in_specs = in_specs + [seq(P_AUG)]
        args = (table, proj, proj, proj, proj)
    return pl.pallas_call(
        kern,
        grid=(B, N_PAIRS),
        in_specs=in_specs,
        out_specs=pl.BlockSpec((None, S, LANES), lambda b, p: (b, 0, p)),
        out_shape=jax.ShapeDtypeStruct((B, S, BRANCH_WIDTH), BF16),
        scratch_shapes=[
            pltpu.VMEM((nq, 2 * tq, 2 * LANES), BF16),
            pltpu.VMEM((nq, 2 * tq, LANES), F32),
            pltpu.VMEM((nq, 2 * tq, 2 * LANES), F32),
            pltpu.VMEM((2 * tq, tq), F32),
            pltpu.VMEM((2 * tq, tq), F32),
        ],
        compiler_params=_cparams(("parallel", "parallel")),
        name="flash_" + mode,
    )(*args)


def _swa_kernel(slopes_ref, sinks_ref, q_ref, k_ref, v_ref, pos_ref, o_ref, e_ref, vd_ref, *, seq, unroll):
    kv = pl.program_id(1)
    lane = lax.broadcasted_iota(jnp.int32, (SWA_BLOCK, LANES), 1)
    row = lax.broadcasted_iota(jnp.int32, (SWA_BLOCK, LANES), 0)
    r = lax.broadcasted_iota(jnp.int32, (SWA_BLOCK, 2 * SWA_BLOCK), 0)
    c = lax.broadcasted_iota(jnp.int32, (SWA_BLOCK, 2 * SWA_BLOCK), 1)
    for g in range(B_GROUP):
        first = N_PIECES * (kv * B_GROUP + g)
        ones_at_pieces = jnp.where(lane >= first, 1.0, 0.0) * jnp.where(lane < first + N_PIECES, 1.0, 0.0)
        e_ref[g * SWA_BLOCK:(g + 1) * SWA_BLOCK, :] = ones_at_pieces.astype(BF16)
    ones = jnp.ones((2 * SWA_BLOCK, LANES), BF16)
    v_all = v_ref[...].astype(F32)
    v_swapped = pltpu.roll(v_all, HEAD_DIM, 1)
    lane_s = lax.broadcasted_iota(jnp.int32, v_all.shape, 1)
    own_half = jnp.where(lane_s < HEAD_DIM, 0, 1) == kv
    vd_ref[...] = jnp.where(own_half, v_all, v_swapped).astype(BF16)

    def block(q_start):
        k_start = pl.multiple_of(jnp.maximum(q_start - SWA_BLOCK, 0), SWA_BLOCK)
        kw = jnp.concatenate([k_ref[pl.ds(k_start, 2 * SWA_BLOCK), :],
                              pos_ref[pl.ds(k_start, 2 * SWA_BLOCK), :]], axis=1)
        vw = jnp.concatenate([vd_ref[pl.ds(k_start, 2 * SWA_BLOCK), :], ones], axis=1)
        heads = []
        for j in range(B_GROUP // 2):
            pair = q_ref[pl.ds(q_start, SWA_BLOCK), j * LANES:(j + 1) * LANES]
            zero = jnp.zeros_like(pair)
            heads += [jnp.where(lane < HEAD_DIM, pair, zero), jnp.where(lane >= HEAD_DIM, pair, zero)]
        qs = jnp.concatenate(heads, axis=0)
        s_all = _dot_nt(jnp.concatenate([qs, e_ref[...]], axis=1), kw)
        dist = (q_start - k_start) + r - c
        valid = (dist >= 0) & (dist < WINDOW)
        qpos = (q_start + row).astype(F32)
        ps, tails = [], []
        for g in range(B_GROUP):
            h = kv * B_GROUP + g
            sink = sinks_ref[h] + slopes_ref[h] * qpos
            s = jnp.where(valid, s_all[g * SWA_BLOCK:(g + 1) * SWA_BLOCK], -jnp.inf)
            s0, s1 = s[:, 0:LANES], s[:, LANES:2 * LANES]
            mx = jnp.maximum(jnp.max(jnp.maximum(s0, s1), axis=1, keepdims=True), sink)
            ps.append(jnp.concatenate([jnp.exp2(s0 - mx).astype(BF16), jnp.exp2(s1 - mx).astype(BF16)], axis=1))
            tails.append(jnp.exp2(sink - mx))
        pv = _dot(jnp.concatenate(ps, axis=0), vw)
        outs = []
        for g in range(B_GROUP):
            blk = pv[g * SWA_BLOCK:(g + 1) * SWA_BLOCK]
            outs.append(blk[:, 0:LANES] / (blk[:, LANES:2 * LANES] + tails[g]))
        for h2 in range(B_GROUP // 2):
            o_ref[pl.ds(q_start, SWA_BLOCK), h2 * LANES:(h2 + 1) * LANES] = (
                jnp.where(lane < HEAD_DIM, outs[2 * h2], outs[2 * h2 + 1]).astype(o_ref.dtype))

    def body(i, carry):
        for n in range(unroll):
            block(pl.multiple_of((i * unroll + n) * SWA_BLOCK, SWA_BLOCK))
        return carry

    lax.fori_loop(0, seq // (SWA_BLOCK * unroll), body, 0)


def _swa_call(proj, slopes2, sinks2, pos_aug, *, unroll=SWA_UNROLL):
    B, S, _ = proj.shape
    kern = functools.partial(_swa_kernel, seq=S, unroll=unroll)
    qw = B_GROUP * HEAD_DIM
    return pl.pallas_call(
        kern,
        grid=(B, B_KV_HEADS),
        in_specs=[
            pl.BlockSpec(memory_space=pltpu.SMEM),
            pl.BlockSpec(memory_space=pltpu.SMEM),
            pl.BlockSpec((None, S, qw), lambda b, kv: (b, 0, P_QB * LANES // qw + kv)),
            pl.BlockSpec((None, S, LANES), lambda b, kv: (b, 0, P_KB + kv)),
            pl.BlockSpec((None, S, LANES), lambda b, kv: (b, 0, P_VB)),
            pl.BlockSpec((S, LANES), lambda b, kv: (0, 0)),
        ],
        out_specs=pl.BlockSpec((None, S, 2 * LANES), lambda b, kv: (b, 0, kv)),
        out_shape=jax.ShapeDtypeStruct((B, S, BRANCH_WIDTH), BF16),
        scratch_shapes=[pltpu.VMEM((B_GROUP * SWA_BLOCK, LANES), BF16), pltpu.VMEM((S, LANES), BF16)],
        compiler_params=_cparams(("parallel", "parallel")),
        name="swa",
    )(slopes2, sinks2, proj, proj, proj, pos_aug)


def _post_kernel(x_ref, g_ref, fg_ref, sg_ref, ya_ref, yb_ref, yc_ref, wg_ref, wgm_ref, wua_ref, wub_ref, wuc_ref,
                 wo_ref, o_ref, *, final):
    x = x_ref[...]
    h = _rms(x, g_ref[...], RMS_EPS).astype(BF16)
    hw = D_MODEL // 2
    ts = []
    for br, y_ref in enumerate((ya_ref, yb_ref, yc_ref)):
        g = _dot_nt(h, wg_ref[br * BRANCH_WIDTH:(br + 1) * BRANCH_WIDTH, :])
        y = y_ref[...].astype(F32)
        if br == 0:
            y = jnp.concatenate([_rms(y[:, c0:c0 + LANES], sg_ref[...], SUBLN_EPS)
                                 for c0 in range(0, BRANCH_WIDTH, LANES)], axis=1)
        ts.append((y * (g * jax.nn.sigmoid(g))).astype(BF16))
    merged = [None, None]
    for br, wu_ref in enumerate((wua_ref, wub_ref, wuc_ref)):
        for j in range(2):
            gate = jax.nn.sigmoid(_dot_nt(h, wgm_ref[br * D_MODEL + j * hw:br * D_MODEL + (j + 1) * hw, :]))
            u = _dot(ts[br], wu_ref[:, j * hw:(j + 1) * hw])
            merged[j] = gate * u if merged[j] is None else merged[j] + gate * u
    mb = jnp.concatenate(merged, axis=1).astype(BF16)
    for j in range(2):
        o_ref[:, j * hw:(j + 1) * hw] = x[:, j * hw:(j + 1) * hw] + _dot(mb, wo_ref[:, j * hw:(j + 1) * hw])
    if final:
        o_ref[...] = _rms(o_ref[...], fg_ref[...], RMS_EPS)


def _post_call(x, norm_gain, final_gain, sub_gain, ya, yb, yc, w_g, w_gm, w_ua, w_ub, w_uc, w_o, layer, final, *,
               tm=TOKEN_TILE):
    B, S, D = x.shape
    kern = functools.partial(_post_kernel, final=final)
    tok = lambda w: pl.BlockSpec((None, tm, w), lambda b, i: (b, i, 0))
    lay = lambda r, c: pl.BlockSpec((None, r, c), lambda b, i: (layer, 0, 0), pipeline_mode=pl.Buffered(1))
    return pl.pallas_call(
        kern,
        grid=(B, S // tm),
        in_specs=[
            tok(D), lay(1, D), pl.BlockSpec((1, D), lambda b, i: (0, 0)), pl.BlockSpec((1, LANES), lambda b, i: (0, 0)),
            tok(BRANCH_WIDTH), tok(BRANCH_WIDTH), tok(BRANCH_WIDTH),
            lay(N_BRANCH * BRANCH_WIDTH, D), lay(N_BRANCH * D, D),
            lay(BRANCH_WIDTH, D), lay(BRANCH_WIDTH, D), lay(BRANCH_WIDTH, D), lay(D, D),
        ],
        out_specs=tok(D),
        out_shape=jax.ShapeDtypeStruct((B, S, D), F32),
        compiler_params=_cparams(("parallel", "parallel")),
        name="post",
    )(x, norm_gain, final_gain, sub_gain, ya, yb, yc, w_g, w_gm, w_ua, w_ub, w_uc, w_o)


def _alibi_slopes(n_heads):
    return 2.0 ** (-8.0 * jnp.arange(1, n_heads + 1, dtype=F32) / n_heads)


def _prepare_weights(w_in):
    col_scale = np.ones((_D_IN,), np.float32)
    for o in (O_QA, O_QB, O_QC):
        col_scale[o:o + 512] = HEAD_DIM ** -0.5 * LOG2E
    wtb = (jnp.swapaxes(w_in, 1, 2) * col_scale[:, None]).astype(BF16)
    sl = lambda o, n: wtb[:, o:o + n, :]
    k0, k1 = sl(O_KB, HEAD_DIM), sl(O_KB + HEAD_DIM, HEAD_DIM)
    w_pre = jnp.concatenate(
        [sl(O_QA, 512), sl(O_KA, 512), sl(O_VA, 512), sl(O_QC, 512), sl(O_KC, 512), sl(O_VC, 512),
         sl(O_QB, 512), k0, k0, k1, k1, sl(O_VB, 128)], axis=1)
    wf = jnp.pad(jnp.tile(sl(O_FC, C_HEADS), (1, N_PIECES, 1)),
                 ((0, 0), (0, LANES - N_PIECES * C_HEADS), (0, 0)))
    w_g = jnp.concatenate([sl(O_GA, 512), sl(O_GB, 512), sl(O_GC, 512)], axis=1)
    w_gm = sl(O_GM, N_BRANCH * D_MODEL)
    return w_pre, wf, w_g, w_gm


def _placement():
    place = np.zeros((LANES, N_PAIRS * LANES), np.float32)
    for x in range(N_PIECES):
        for h in range(C_HEADS):
            place[C_HEADS * x + h, (h // 2) * LANES + N_PIECES * (h % 2) + x] = 1.0
    return jnp.asarray(place, BF16)


def _alibi_pieces(n_heads, seq):
    slopes = 2.0 ** (-8.0 * np.arange(1, n_heads + 1, dtype=np.float64) / n_heads)
    rest = (slopes[:, None] * (LOG2E * np.arange(seq, dtype=np.float64))[None, :]).astype(np.float32)
    pieces = np.zeros((n_heads, seq, N_PIECES), np.float32)
    for x in range(N_PIECES):
        pieces[:, :, x] = rest.astype(BF16).astype(np.float32)
        rest = rest - pieces[:, :, x]
    return pieces


def _alibi_aug(seq):
    table = np.zeros((A_HEADS, seq, LANES), np.float32)
    table[:, :, 0:N_PIECES] = _alibi_pieces(A_HEADS, seq)
    return jnp.asarray(table, BF16)


def _swa_pos_aug(seq):
    table = np.zeros((seq, LANES), np.float32)
    pieces = _alibi_pieces(B_HEADS, seq)
    for h in range(B_HEADS):
        table[:, N_PIECES * h:N_PIECES * (h + 1)] = pieces[h]
    return jnp.asarray(table, BF16)


def kernel(x, norm_gain, w_in, b_forget, lambda_q1, lambda_k1, lambda_q2, lambda_k2, subln_gain, sinks,
           w_up_a, w_up_b, w_up_c, w_o, final_gain):
    B, S, D = x.shape
    depth = w_in.shape[0]
    w_pre, wf, w_g, w_gm = _prepare_weights(w_in)
    w_ua, w_ub, w_uc, w_ob = (w.astype(BF16) for w in (w_up_a, w_up_b, w_up_c, w_o))
    gain3 = norm_gain.reshape(depth, 1, D)
    fgain = final_gain.reshape(1, D)
    bf3 = jnp.pad(jnp.tile(b_forget.astype(F32), (1, N_PIECES)),
                  ((0, 0), (0, LANES - N_PIECES * C_HEADS))).reshape(depth, 1, LANES)
    place = _placement()
    alibi_aug = _alibi_aug(S)
    slopes2_b = _alibi_slopes(B_HEADS) * LOG2E
    pos_aug = _swa_pos_aug(S)
    pad = lambda v: jnp.pad(v.astype(F32), (0, LANES - v.shape[0]))

    for l in range(depth):
        lam_init = 0.8 - 0.6 * math.exp(-0.3 * l)
        proj = _pre_call(x, gain3, w_pre, wf, bf3, place, l)
        consts = jnp.array([lam_init], F32)
        lam_rows = jnp.zeros((8, LANES), F32).at[0:4].set(
            jnp.stack([pad(lambda_q1[l]), pad(lambda_k1[l]), pad(lambda_q2[l]), pad(lambda_k2[l])]))
        ya = _flash_call("diff", proj, P_QA, P_KA, P_VA, (consts, alibi_aug, lam_rows))
        yc = _flash_call("fox", proj, P_QC, P_KC, P_VC)
        yb = _swa_call(proj, slopes2_b, sinks[l].astype(F32) * LOG2E, pos_aug)
        sub_gain = (subln_gain[l].astype(F32) * (1.0 - lam_init)).reshape(1, LANES)
        x = _post_call(x, gain3, fgain, sub_gain, ya, yb, yc, w_g, w_gm, w_ua, w_ub, w_uc, w_ob, l, l == depth - 1)
    return x
```

```python
import functools
import math

import numpy as np
import jax
import jax.numpy as jnp
from jax import lax
from jax.experimental import pallas as pl
from jax.experimental.pallas import tpu as pltpu

F32 = jnp.float32
BF16 = jnp.bfloat16

D_MODEL = 1024
HEAD_DIM = 64
LANES = 128
RMS_EPS = 1e-6
SUBLN_EPS = 1e-5
A_HEADS = 4
B_HEADS = 8
B_KV_HEADS = 2
B_GROUP = B_HEADS // B_KV_HEADS
C_HEADS = 8
WINDOW = 128
SWA_BLOCK = 128
N_BRANCH = 3
BRANCH_WIDTH = 512
N_PAIRS = 4
N_PIECES = 3
LOG2E = math.log2(math.e)

_SPLITS = (512, 512, 512, 512, 512, 128, 128, 512, 512, 512, 512, 8, 512, 3 * D_MODEL)
_OFF = [0]
for _s in _SPLITS:
    _OFF.append(_OFF[-1] + _s)
(O_QA, O_KA, O_VA, O_GA, O_QB, O_KB, O_VB, O_GB, O_QC, O_KC, O_VC, O_FC, O_GC, O_GM, _D_IN) = _OFF

P_QA, P_KA, P_VA = 0, 4, 8
P_QC, P_KC, P_VC = 12, 16, 20
P_QB = 24
P_KB = 28
P_VB = 30
P_MM = 31
P_AUG = 31
P_BLOCKS = 35
P_WIDTH = P_BLOCKS * LANES
MM_WIDTH = P_MM * LANES

VMEM_LIMIT = 56 * 1024 * 1024

TOKEN_TILE = 1024
PROJ_CHUNK = 512
FLASH_TILE = 512
FLASH_UNROLL = 14
SWA_UNROLL = 8


def _cparams(sem):
    return pltpu.CompilerParams(dimension_semantics=sem, vmem_limit_bytes=VMEM_LIMIT)


def _rms(x, gain, eps):
    ms = jnp.mean(x * x, axis=-1, keepdims=True)
    return x * lax.rsqrt(ms + eps) * gain


def _dot(a, b):
    return jnp.dot(a, b, preferred_element_type=F32)


def _dot_nt(a, b):
    return lax.dot_general(a, b, (((1,), (1,)), ((), ())), preferred_element_type=F32)


def _split3(v):
    hi = v.astype(BF16)
    r = v - hi.astype(F32)
    mid = r.astype(BF16)
    lo = (r - mid.astype(F32)).astype(BF16)
    return hi, mid, lo


def _pre_kernel(x_ref, g_ref, w_ref, wf_ref, bf_ref, place_ref, proj_ref, carry_ref, *, tm, chunk):
    i = pl.program_id(1)

    @pl.when(i == 0)
    def _():
        carry_ref[...] = jnp.zeros_like(carry_ref)

    h = _rms(x_ref[...], g_ref[...], RMS_EPS).astype(BF16)

    lane = lax.broadcasted_iota(jnp.int32, (tm, LANES), 1)

    def pieces_by_group(v):
        p0, p1, p2 = (p.astype(F32) for p in _split3(v))
        by_group = jnp.where(lane < C_HEADS, p0, jnp.where(lane < 2 * C_HEADS, p1, p2))
        return jnp.where(lane < N_PIECES * C_HEADS, by_group, 0.0).astype(BF16)

    starts = list(range(0, MM_WIDTH, chunk))

    def project(first, last):
        for c0 in starts[first:last]:
            c1 = min(c0 + chunk, MM_WIDTH)
            proj_ref[:, c0:c1] = _dot_nt(h, w_ref[c0:c1, :]).astype(BF16)

    n1, n2, n3 = len(starts) // 4, len(starts) // 2, 3 * len(starts) // 4
    project(0, n1)
    z = _dot_nt(h, wf_ref[...]) + bf_ref[...]
    logf = jnp.minimum(z, 0.0) - jnp.log1p(jnp.exp(-jnp.abs(z)))
    project(n1, n2)
    row = lax.broadcasted_iota(jnp.int32, (tm, tm), 0)
    col = lax.broadcasted_iota(jnp.int32, (tm, tm), 1)
    lower = (col <= row).astype(BF16)
    part = _dot(lower, pieces_by_group(logf))
    total = part + pltpu.roll(part, LANES - C_HEADS, 1) + pltpu.roll(part, LANES - 2 * C_HEADS, 1)
    total = jnp.where(lane < C_HEADS, total, 0.0)
    c = total + pltpu.roll(total, C_HEADS, 1) + pltpu.roll(total, 2 * C_HEADS, 1) + carry_ref[0:1, :]
    carry_ref[...] = jnp.broadcast_to(c[tm - 1:tm, :], carry_ref.shape)
    project(n2, n3)
    proj_ref[:, MM_WIDTH:P_WIDTH] = _dot(pieces_by_group(c * (-LOG2E)), place_ref[...]).astype(BF16)
    project(n3, len(starts))


def _pre_call(x, norm_gain, w_pre, wf, b_f, place, layer, *, tm=TOKEN_TILE, chunk=PROJ_CHUNK):
    B, S, D = x.shape
    kern = functools.partial(_pre_kernel, tm=tm, chunk=chunk)
    return pl.pallas_call(
        kern,
        grid=(B, S // tm),
        in_specs=[
            pl.BlockSpec((None, tm, D), lambda b, i: (b, i, 0)),
            pl.BlockSpec((None, 1, D), lambda b, i: (layer, 0, 0)),
            pl.BlockSpec((None, MM_WIDTH, D), lambda b, i: (layer, 0, 0), pipeline_mode=pl.Buffered(1)),
            pl.BlockSpec((None, LANES, D), lambda b, i: (layer, 0, 0)),
            pl.BlockSpec((None, 1, LANES), lambda b, i: (layer, 0, 0)),
            pl.BlockSpec((LANES, N_PAIRS * LANES), lambda b, i: (0, 0)),
        ],
        out_specs=pl.BlockSpec((None, tm, P_WIDTH), lambda b, i: (b, i, 0)),
        out_shape=jax.ShapeDtypeStruct((B, S, P_WIDTH), BF16),
        scratch_shapes=[pltpu.VMEM((8, LANES), F32)],
        compiler_params=_cparams(("parallel", "arbitrary")),
        name="pre",
    )(x, norm_gain, w_pre, wf, b_f, place)


def _chunk_order(nq):
    order = [(t, t) for t in range(nq)]
    order += [(qi, c) for qi in range(1, nq) for c in range(qi)]
    return order


def _flash_kernel(*refs, mode, tq, nq, unroll):
    if mode == "diff":
        (consts_ref, tab_ref, q_ref, k_ref, v_ref, aug_ref, lam_ref, o_ref,
         qe_ref, m_ref, acc_ref, sa_ref, sb_ref) = refs
    else:
        (tab_ref, q_ref, k_ref, v_ref, aug_ref, o_ref, qe_ref, m_ref, acc_ref, sa_ref, sb_ref) = refs
    tk = tq
    rows = 2 * tq
    half = tq // 2
    n_chunks = nq * (nq + 1) // 2
    lane = lax.broadcasted_iota(jnp.int32, (half, LANES), 1)

    def build_qe(t):
        for hf in range(2):
            q = q_ref[t * tq + hf * half:t * tq + (hf + 1) * half, :]
            zero = jnp.zeros_like(q)
            for m in range(2):
                in_map = (lane >= HEAD_DIM) if m else (lane < HEAD_DIM)
                first = N_PIECES * m if mode == "fox" else 0
                ones_at_pieces = jnp.where(lane >= first, 1.0, 0.0) * jnp.where(lane < first + N_PIECES, 1.0, 0.0)
                g0 = (2 * hf + m) * half
                qe_ref[t, g0:g0 + half, 0:LANES] = jnp.where(in_map, q, zero)
                qe_ref[t, g0:g0 + half, LANES:2 * LANES] = ones_at_pieces.astype(BF16)

    for t in range(nq):
        build_qe(t)

    def keys(start, n):
        return jnp.concatenate([k_ref[pl.ds(start, n), :], aug_ref[pl.ds(start, n), :]], axis=1)

    def values(start, n):
        return jnp.concatenate([v_ref[pl.ds(start, n), :], jnp.ones((n, LANES), BF16)], axis=1)

    def qk(t, s_ref):
        qi = tab_ref[0, t]
        start = pl.multiple_of(tab_ref[1, t] * tk, tk)
        s_ref[...] = _dot_nt(qe_ref[qi], keys(start, tk))

    def qk_diagonal(t, s_ref):
        s_ref[0:tq, 0:half] = _dot_nt(qe_ref[t, 0:tq], keys(t * tk, half))
        s_ref[tq:rows, :] = _dot_nt(qe_ref[t, tq:rows], keys(t * tk, tk))

    def probabilities(s, m_prev):
        cols = [s[:, c0:c0 + LANES] for c0 in range(0, s.shape[1], LANES)]
        mx = cols[0]
        for sc in cols[1:]:
            mx = jnp.maximum(mx, sc)
        m_new = jnp.broadcast_to(jnp.max(mx, axis=1, keepdims=True), (s.shape[0], LANES))
        if m_prev is not None:
            m_new = jnp.maximum(m_prev, m_new)
        return m_new, jnp.concatenate([jnp.exp2(sc - m_new).astype(BF16) for sc in cols], axis=1)

    def softmax_pv(s_ref, t):
        qi = tab_ref[0, t]
        start = pl.multiple_of(tab_ref[1, t] * tk, tk)
        m_prev = m_ref[qi]
        m_new, p = probabilities(s_ref[...], m_prev)
        alpha = jnp.exp2(m_prev - m_new)
        pv = _dot(p, values(start, tk))
        acc_ref[qi, :, 0:LANES] = alpha * acc_ref[qi, :, 0:LANES] + pv[:, 0:LANES]
        acc_ref[qi, :, LANES:2 * LANES] = alpha * acc_ref[qi, :, LANES:2 * LANES] + pv[:, LANES:2 * LANES]
        m_ref[qi] = m_new

    def softmax_pv_diagonal(s_ref, t):
        for r0, n_keys in ((0, half), (tq, tk)):
            r = lax.broadcasted_iota(jnp.int32, (tq, n_keys), 0)
            c = lax.broadcasted_iota(jnp.int32, (tq, n_keys), 1)
            q_pos = jnp.where(r >= half, r - half, r) + (half if r0 else 0)
            s = jnp.where(c <= q_pos, s_ref[r0:r0 + tq, 0:n_keys], -jnp.inf)
            m_new, p = probabilities(s, None)
            acc_ref[t, r0:r0 + tq, :] = _dot(p, values(t * tk, n_keys))
            m_ref[t, r0:r0 + tq, :] = m_new

    s_refs = (sa_ref, sb_ref)

    qk_diagonal(0, sa_ref)
    for t in range(nq):
        if t + 1 < nq:
            qk_diagonal(t + 1, s_refs[(t + 1) % 2])
        else:
            qk(t + 1, s_refs[(t + 1) % 2])
        softmax_pv_diagonal(s_refs[t % 2], t)

    def run(first, last, unroll):
        def body(i, carry):
            for u in range(unroll):
                t = first + i * unroll + u
                qk(t + 1, s_refs[(u + 1) % 2])
                softmax_pv(s_refs[u % 2], t)
            return carry
        lax.fori_loop(0, (last - first) // unroll, body, 0)

    main = nq + (n_chunks - nq) // unroll * unroll
    if main > nq:
        run(nq, main, unroll)
    if n_chunks > main:
        run(main, n_chunks, n_chunks - main)

    def finalize(t):
        for hf in range(2):
            a0 = acc_ref[t, (2 * hf) * half:(2 * hf + 1) * half, :]
            a1 = acc_ref[t, (2 * hf + 1) * half:(2 * hf + 2) * half, :]
            o0 = a0[:, 0:LANES] / a0[:, LANES:2 * LANES]
            o1 = a1[:, 0:LANES] / a1[:, LANES:2 * LANES]
            if mode == "diff":
                lam_init = consts_ref[0]
                lp = lam_ref[...]
                lam = (jnp.exp(jnp.sum(lp[0:1] * lp[1:2], axis=1, keepdims=True))
                       - jnp.exp(jnp.sum(lp[2:3] * lp[3:4], axis=1, keepdims=True)) + lam_init)
                y = o0 - lam * o1
            else:
                y = jnp.where(lane < HEAD_DIM, o0, o1)
            o_ref[t * tq + hf * half:t * tq + (hf + 1) * half, :] = y.astype(o_ref.dtype)

    for t in range(nq):
        finalize(t)


def _flash_call(mode, proj, q_blk, k_blk, v_blk, extra=(), *, tq=FLASH_TILE, unroll=FLASH_UNROLL):
    B, S, _ = proj.shape
    nq = S // tq
    order = _chunk_order(nq)
    assert nq % 2 == 0 and len(order) % 2 == 0, "the chunk loop handles two chunks per trip"
    table = jnp.asarray(np.array(order + [order[-1]], np.int32).T)
    kern = functools.partial(_flash_kernel, mode=mode, tq=tq, nq=nq, unroll=unroll)
    seq = lambda blk: pl.BlockSpec((None, S, LANES), lambda b, p: (b, 0, blk + p))
    smem = pl.BlockSpec(memory_space=pltpu.SMEM)
    in_specs = [smem, seq(q_blk), seq(k_blk), seq(v_blk)]
    if mode == "diff":
        consts, alibi_aug, lam_rows = extra
        in_specs = [smem] + in_specs + [
            pl.BlockSpec((None, S, LANES), lambda b, p: (p, 0, 0)),
            pl.BlockSpec((8, LANES), lambda b, p: (0, 0)),
        ]
        args = (consts, table, proj, proj, proj, alibi_aug, lam_rows)
    else:
        in_specs = in_specs + [seq(P_AUG)]
        args = (table, proj, proj, proj, proj)
    return pl.pallas_call(
        kern,
        grid=(B, N_PAIRS),
        in_specs=in_specs,
        out_specs=pl.BlockSpec((None, S, LANES), lambda b, p: (b, 0, p)),
        out_shape=jax.ShapeDtypeStruct((B, S, BRANCH_WIDTH), BF16),
        scratch_shapes=[
            pltpu.VMEM((nq, 2 * tq, 2 * LANES), BF16),
            pltpu.VMEM((nq, 2 * tq, LANES), F32),
            pltpu.VMEM((nq, 2 * tq, 2 * LANES), F32),
            pltpu.VMEM((2 * tq, tq), F32),
            pltpu.VMEM((2 * tq, tq), F32),
        ],
        compiler_params=_cparams(("parallel", "parallel")),
        name="flash_" + mode,
    )(*args)


def _swa_kernel(slopes_ref, sinks_ref, q_ref, k_ref, v_ref, pos_ref, o_ref, e_ref, vd_ref, *, seq, unroll):
    kv = pl.program_id(1)
    lane = lax.broadcasted_iota(jnp.int32, (SWA_BLOCK, LANES), 1)
    row = lax.broadcasted_iota(jnp.int32, (SWA_BLOCK, LANES), 0)
    r = lax.broadcasted_iota(jnp.int32, (SWA_BLOCK, 2 * SWA_BLOCK), 0)
    c = lax.broadcasted_iota(jnp.int32, (SWA_BLOCK, 2 * SWA_BLOCK), 1)
    for g in range(B_GROUP):
        first = N_PIECES * (kv * B_GROUP + g)
        ones_at_pieces = jnp.where(lane >= first, 1.0, 0.0) * jnp.where(lane < first + N_PIECES, 1.0, 0.0)
        e_ref[g * SWA_BLOCK:(g + 1) * SWA_BLOCK, :] = ones_at_pieces.astype(BF16)
    ones = jnp.ones((2 * SWA_BLOCK, LANES), BF16)
    v_all = v_ref[...].astype(F32)
    v_swapped = pltpu.roll(v_all, HEAD_DIM, 1)
    lane_s = lax.broadcasted_iota(jnp.int32, v_all.shape, 1)
    own_half = jnp.where(lane_s < HEAD_DIM, 0, 1) == kv
    vd_ref[...] = jnp.where(own_half, v_all, v_swapped).astype(BF16)

    def block(q_start):
        k_start = pl.multiple_of(jnp.maximum(q_start - SWA_BLOCK, 0), SWA_BLOCK)
        kw = jnp.concatenate([k_ref[pl.ds(k_start, 2 * SWA_BLOCK), :],
                              pos_ref[pl.ds(k_start, 2 * SWA_BLOCK), :]], axis=1)
        vw = jnp.concatenate([vd_ref[pl.ds(k_start, 2 * SWA_BLOCK), :], ones], axis=1)
        heads = []
        for j in range(B_GROUP // 2):
            pair = q_ref[pl.ds(q_start, SWA_BLOCK), j * LANES:(j + 1) * LANES]
            zero = jnp.zeros_like(pair)
            heads += [jnp.where(lane < HEAD_DIM, pair, zero), jnp.where(lane >= HEAD_DIM, pair, zero)]
        qs = jnp.concatenate(heads, axis=0)
        s_all = _dot_nt(jnp.concatenate([qs, e_ref[...]], axis=1), kw)
        dist = (q_start - k_start) + r - c
        valid = (dist >= 0) & (dist < WINDOW)
        qpos = (q_start + row).astype(F32)
        ps, tails = [], []
        for g in range(B_GROUP):
            h = kv * B_GROUP + g
            sink = sinks_ref[h] + slopes_ref[h] * qpos
            s = jnp.where(valid, s_all[g * SWA_BLOCK:(g + 1) * SWA_BLOCK], -jnp.inf)
            s0, s1 = s[:, 0:LANES], s[:, LANES:2 * LANES]
            mx = jnp.maximum(jnp.max(jnp.maximum(s0, s1), axis=1, keepdims=True), sink)
            ps.append(jnp.concatenate([jnp.exp2(s0 - mx).astype(BF16), jnp.exp2(s1 - mx).astype(BF16)], axis=1))
            tails.append(jnp.exp2(sink - mx))
        pv = _dot(jnp.concatenate(ps, axis=0), vw)
        outs = []
        for g in range(B_GROUP):
            blk = pv[g * SWA_BLOCK:(g + 1) * SWA_BLOCK]
            outs.append(blk[:, 0:LANES] / (blk[:, LANES:2 * LANES] + tails[g]))
        for h2 in range(B_GROUP // 2):
            o_ref[pl.ds(q_start, SWA_BLOCK), h2 * LANES:(h2 + 1) * LANES] = (
                jnp.where(lane < HEAD_DIM, outs[2 * h2], outs[2 * h2 + 1]).astype(o_ref.dtype))

    def body(i, carry):
        for n in range(unroll):
            block(pl.multiple_of((i * unroll + n) * SWA_BLOCK, SWA_BLOCK))
        return carry

    lax.fori_loop(0, seq // (SWA_BLOCK * unroll), body, 0)


def _swa_call(proj, slopes2, sinks2, pos_aug, *, unroll=SWA_UNROLL):
    B, S, _ = proj.shape
    kern = functools.partial(_swa_kernel, seq=S, unroll=unroll)
    qw = B_GROUP * HEAD_DIM
    return pl.pallas_call(
        kern,
        grid=(B, B_KV_HEADS),
        in_specs=[
            pl.BlockSpec(memory_space=pltpu.SMEM),
            pl.BlockSpec(memory_space=pltpu.SMEM),
            pl.BlockSpec((None, S, qw), lambda b, kv: (b, 0, P_QB * LANES // qw + kv)),
            pl.BlockSpec((None, S, LANES), lambda b, kv: (b, 0, P_KB + kv)),
            pl.BlockSpec((None, S, LANES), lambda b, kv: (b, 0, P_VB)),
            pl.BlockSpec((S, LANES), lambda b, kv: (0, 0)),
        ],
        out_specs=pl.BlockSpec((None, S, 2 * LANES), lambda b, kv: (b, 0, kv)),
        out_shape=jax.ShapeDtypeStruct((B, S, BRANCH_WIDTH), BF16),
        scratch_shapes=[pltpu.VMEM((B_GROUP * SWA_BLOCK, LANES), BF16), pltpu.VMEM((S, LANES), BF16)],
        compiler_params=_cparams(("parallel", "parallel")),
        name="swa",
    )(slopes2, sinks2, proj, proj, proj, pos_aug)


def _post_kernel(x_ref, g_ref, fg_ref, sg_ref, ya_ref, yb_ref, yc_ref, wg_ref, wgm_ref, wua_ref, wub_ref, wuc_ref,
                 wo_ref, o_ref, *, final):
    x = x_ref[...]
    h = _rms(x, g_ref[...], RMS_EPS).astype(BF16)
    hw = D_MODEL // 2
    ts = []
    for br, y_ref in enumerate((ya_ref, yb_ref, yc_ref)):
        g = _dot_nt(h, wg_ref[br * BRANCH_WIDTH:(br + 1) * BRANCH_WIDTH, :])
        y = y_ref[...].astype(F32)
        if br == 0:
            y = jnp.concatenate([_rms(y[:, c0:c0 + LANES], sg_ref[...], SUBLN_EPS)
                                 for c0 in range(0, BRANCH_WIDTH, LANES)], axis=1)
        ts.append((y * (g * jax.nn.sigmoid(g))).astype(BF16))
    merged = [None, None]
    for br, wu_ref in enumerate((wua_ref, wub_ref, wuc_ref)):
        for j in range(2):
            gate = jax.nn.sigmoid(_dot_nt(h, wgm_ref[br * D_MODEL + j * hw:br * D_MODEL + (j + 1) * hw, :]))
            u = _dot(ts[br], wu_ref[:, j * hw:(j + 1) * hw])
            merged[j] = gate * u if merged[j] is None else merged[j] + gate * u
    mb = jnp.concatenate(merged, axis=1).astype(BF16)
    for j in range(2):
        o_ref[:, j * hw:(j + 1) * hw] = x[:, j * hw:(j + 1) * hw] + _dot(mb, wo_ref[:, j * hw:(j + 1) * hw])
    if final:
        o_ref[...] = _rms(o_ref[...], fg_ref[...], RMS_EPS)


def _post_call(x, norm_gain, final_gain, sub_gain, ya, yb, yc, w_g, w_gm, w_ua, w_ub, w_uc, w_o, layer, final, *,
               tm=TOKEN_TILE):
    B, S, D = x.shape
    kern = functools.partial(_post_kernel, final=final)
    tok = lambda w: pl.BlockSpec((None, tm, w), lambda b, i: (b, i, 0))
    lay = lambda r, c: pl.BlockSpec((None, r, c), lambda b, i: (layer, 0, 0), pipeline_mode=pl.Buffered(1))
    return pl.pallas_call(
        kern,
        grid=(B, S // tm),
        in_specs=[
            tok(D), lay(1, D), pl.BlockSpec((1, D), lambda b, i: (0, 0)), pl.BlockSpec((1, LANES), lambda b, i: (0, 0)),
            tok(BRANCH_WIDTH), tok(BRANCH_WIDTH), tok(BRANCH_WIDTH),
            lay(N_BRANCH * BRANCH_WIDTH, D), lay(N_BRANCH * D, D),
            lay(BRANCH_WIDTH, D), lay(BRANCH_WIDTH, D), lay(BRANCH_WIDTH, D), lay(D, D),
        ],
        out_specs=tok(D),
        out_shape=jax.ShapeDtypeStruct((B, S, D), F32),
        compiler_params=_cparams(("parallel", "parallel")),
        name="post",
    )(x, norm_gain, final_gain, sub_gain, ya, yb, yc, w_g, w_gm, w_ua, w_ub, w_uc, w_o)


def _alibi_slopes(n_heads):
    return 2.0 ** (-8.0 * jnp.arange(1, n_heads + 1, dtype=F32) / n_heads)


def _prepare_weights(w_in):
    col_scale = np.ones((_D_IN,), np.float32)
    for o in (O_QA, O_QB, O_QC):
        col_scale[o:o + 512] = HEAD_DIM ** -0.5 * LOG2E
    wtb = (jnp.swapaxes(w_in, 1, 2) * col_scale[:, None]).astype(BF16)
    sl = lambda o, n: wtb[:, o:o + n, :]
    k0, k1 = sl(O_KB, HEAD_DIM), sl(O_KB + HEAD_DIM, HEAD_DIM)
    w_pre = jnp.concatenate(
        [sl(O_QA, 512), sl(O_KA, 512), sl(O_VA, 512), sl(O_QC, 512), sl(O_KC, 512), sl(O_VC, 512),
         sl(O_QB, 512), k0, k0, k1, k1, sl(O_VB, 128)], axis=1)
    wf = jnp.pad(jnp.tile(sl(O_FC, C_HEADS), (1, N_PIECES, 1)),
                 ((0, 0), (0, LANES - N_PIECES * C_HEADS), (0, 0)))
    w_g = jnp.concatenate([sl(O_GA, 512), sl(O_GB, 512), sl(O_GC, 512)], axis=1)
    w_gm = sl(O_GM, N_BRANCH * D_MODEL)
    return w_pre, wf, w_g, w_gm


def _placement():
    place = np.zeros((LANES, N_PAIRS * LANES), np.float32)
    for x in range(N_PIECES):
        for h in range(C_HEADS):
            place[C_HEADS * x + h, (h // 2) * LANES + N_PIECES * (h % 2) + x] = 1.0
    return jnp.asarray(place, BF16)


def _alibi_pieces(n_heads, seq):
    slopes = 2.0 ** (-8.0 * np.arange(1, n_heads + 1, dtype=np.float64) / n_heads)
    rest = (slopes[:, None] * (LOG2E * np.arange(seq, dtype=np.float64))[None, :]).astype(np.float32)
    pieces = np.zeros((n_heads, seq, N_PIECES), np.float32)
    for x in range(N_PIECES):
        pieces[:, :, x] = rest.astype(BF16).astype(np.float32)
        rest = rest - pieces[:, :, x]
    return pieces


def _alibi_aug(seq):
    table = np.zeros((A_HEADS, seq, LANES), np.float32)
    table[:, :, 0:N_PIECES] = _alibi_pieces(A_HEADS, seq)
    return jnp.asarray(table, BF16)


def _swa_pos_aug(seq):
    table = np.zeros((seq, LANES), np.float32)
    pieces = _alibi_pieces(B_HEADS, seq)
    for h in range(B_HEADS):
        table[:, N_PIECES * h:N_PIECES * (h + 1)] = pieces[h]
    return jnp.asarray(table, BF16)


def kernel(x, norm_gain, w_in, b_forget, lambda_q1, lambda_k1, lambda_q2, lambda_k2, subln_gain, sinks,
           w_up_a, w_up_b, w_up_c, w_o, final_gain):
    B, S, D = x.shape
    depth = w_in.shape[0]
    w_pre, wf, w_g, w_gm = _prepare_weights(w_in)
    w_ua, w_ub, w_uc, w_ob = (w.astype(BF16) for w in (w_up_a, w_up_b, w_up_c, w_o))
    gain3 = norm_gain.reshape(depth, 1, D)
    fgain = final_gain.reshape(1, D)
    bf3 = jnp.pad(jnp.tile(b_forget.astype(F32), (1, N_PIECES)),
                  ((0, 0), (0, LANES - N_PIECES * C_HEADS))).reshape(depth, 1, LANES)
    place = _placement()
    alibi_aug = _alibi_aug(S)
    slopes2_b = _alibi_slopes(B_HEADS) * LOG2E
    pos_aug = _swa_pos_aug(S)
    pad = lambda v: jnp.pad(v.astype(F32), (0, LANES - v.shape[0]))

    for l in range(depth):
        lam_init = 0.8 - 0.6 * math.exp(-0.3 * l)
        proj = _pre_call(x, gain3, w_pre, wf, bf3, place, l)
        consts = jnp.array([lam_init], F32)
        lam_rows = jnp.zeros((8, LANES), F32).at[0:4].set(
            jnp.stack([pad(lambda_q1[l]), pad(lambda_k1[l]), pad(lambda_q2[l]), pad(lambda_k2[l])]))
        ya = _flash_call("diff", proj, P_QA, P_KA, P_VA, (consts, alibi_aug, lam_rows))
        yc = _flash_call("fox", proj, P_QC, P_KC, P_VC)
        yb = _swa_call(proj, slopes2_b, sinks[l].astype(F32) * LOG2E, pos_aug)
        sub_gain = (subln_gain[l].astype(F32) * (1.0 - lam_init)).reshape(1, LANES)
        x = _post_call(x, gain3, fgain, sub_gain, ya, yb, yc, w_g, w_gm, w_ua, w_ub, w_uc, w_ob, l, l == depth - 1)
    return x
```
